```python
import math
import jax, jax.numpy as jnp
from jax import lax
import numpy as np

D_MODEL = 1024
BATCH = 16
SEQ = 256
DEPTH = 4
DEC_BATCH = 4
DEC_SEQ = 4096
PAST_LEN = 512

N_AB_LAYERS = (DEPTH + 1) // 2
N_C_LAYERS = DEPTH // 2
GRID_W = 64
N_MOD = 6
EPS = 1e-6

D_SSM = D_MODEL
SSM_HEAD_DIM = 64
SSM_HEADS = D_SSM // SSM_HEAD_DIM
SSM_GROUPS = 4
SSM_STATE = 128
CONV_WIDTH = 5
CHUNK = 128
DT_MIN = 1e-3
DT_MAX = 1e-1

DA_HEADS = D_MODEL // 128
DA_QK_DIM = 64
DA_V_DIM = 2 * DA_QK_DIM
ROPE_FREQS = DA_QK_DIM // 4
ROPE_THETA = 10000.0
Q_BLOCK = 128

XBC_DIM = D_SSM + 2 * SSM_GROUPS * SSM_STATE
Z_END = D_SSM
XBC_END = Z_END + XBC_DIM
DT_END = XBC_END + 2 * SSM_HEADS
Q_END = DT_END + DA_HEADS * 2 * DA_QK_DIM
K_END = Q_END + DA_HEADS * 2 * DA_QK_DIM
IN_COLS = K_END + DA_HEADS * DA_V_DIM
PROJ_SPLITS = (Z_END, XBC_END, DT_END, Q_END, K_END)
MIX_OUT = D_SSM + DA_HEADS * DA_V_DIM

POOL_WINDOWS = (2, 4, 8, 16)
POOL_GROUP = D_MODEL // 4

MOE_GROUPS = 4
EXPERTS_PER_GROUP = 8
N_EXPERTS = MOE_GROUPS * EXPERTS_PER_GROUP
MOE_TOP_K = 2
EXPERT_HIDDEN = D_MODEL // 2
MOE_BLOCK = 128

kernel_name = 'hybrid_ssd_diffattn_pool_hmoe_diffusion_step'


def rmsnorm(x, g):
    xf = x.astype(jnp.float32)
    y = xf * lax.rsqrt(jnp.mean(xf * xf, axis=-1, keepdims=True) + EPS)
    return (y * g.astype(jnp.float32)).astype(x.dtype)


def depthwise_conv(x, w, bias):
    ch = x.shape[-1]
    y = lax.conv_general_dilated(x, w[:, None, :], (1,), 'SAME',
                                 dimension_numbers=('NWC', 'WIO', 'NWC'), feature_group_count=ch)
    return y + bias


def ssd_scan(x, dt, A, B, C, h0):
    b, L, H, P = x.shape
    G, N = B.shape[2], B.shape[3]
    R = H // G
    nc = L // CHUNK
    xc = x.reshape(b, nc, CHUNK, G, R, P)
    dtc = dt.reshape(b, nc, CHUNK, G, R)
    Bc = B.reshape(b, nc, CHUNK, G, N)
    Cc = C.reshape(b, nc, CHUNK, G, N)
    a_cs = jnp.cumsum(dtc * A.reshape(G, R), axis=2)
    xdt = xc * dtc[..., None]
    causal = jnp.tril(jnp.ones((CHUNK, CHUNK), bool))
    seg = a_cs[:, :, :, None] - a_cs[:, :, None, :]
    decay = jnp.exp(jnp.where(causal[:, :, None, None], seg, -jnp.inf))
    cb = jnp.einsum('bcign,bcjgn->bcijg', Cc, Bc)
    y_diag = jnp.einsum('bcijgr,bcjgrp->bcigrp', cb[..., None] * decay, xdt)
    to_end = jnp.exp(a_cs[:, :, -1:] - a_cs)
    chunk_states = jnp.einsum('bcjgn,bcjgrp->bcgrpn', Bc, xdt * to_end[..., None])
    chunk_decay = jnp.exp(a_cs[:, :, -1])

    def step(h, inp):
        st, dec = inp
        return h * dec[..., None, None] + st, h

    h_last, h_in = lax.scan(step, h0.reshape(b, G, R, P, N),
                            (chunk_states.transpose(1, 0, 2, 3, 4, 5), chunk_decay.transpose(1, 0, 2, 3)))
    h_in = h_in.transpose(1, 0, 2, 3, 4, 5)
    y_off = jnp.einsum('bcign,bcgrpn->bcigrp', Cc, h_in) * jnp.exp(a_cs)[..., None]
    y = (y_diag + y_off).reshape(b, L, H, P)
    return y, h_last.reshape(b, H, P, N)


def axial_rope_tables(L, dtype):
    rows = L // GRID_W
    row = jnp.broadcast_to(jnp.arange(rows)[:, None], (rows, GRID_W)).reshape(L)
    col = jnp.broadcast_to(jnp.arange(GRID_W)[None, :], (rows, GRID_W)).reshape(L)
    inv = ROPE_THETA ** (-jnp.arange(ROPE_FREQS, dtype=jnp.float32) / ROPE_FREQS)
    pos = jnp.stack([row, col], axis=-1).astype(jnp.float32)
    ang = pos[..., None] * inv
    return jnp.cos(ang).astype(dtype), jnp.sin(ang).astype(dtype)


def apply_axial_rope(x, cos, sin):
    xr = x.reshape(*x.shape[:-1], 2, 2, ROPE_FREQS)
    x1, x2 = xr[..., 0, :], xr[..., 1, :]
    c = cos[:, None, None]
    s = sin[:, None, None]
    out = jnp.stack([x1 * c - x2 * s, x2 * c + x1 * s], axis=-2)
    return out.reshape(x.shape)


def diff_attention(q, k, v, lam):
    b, Lq, H = q.shape[0], q.shape[1], q.shape[2]
    nqb = Lq // Q_BLOCK
    qb = q.reshape(b, nqb, Q_BLOCK, H, 2, DA_QK_DIM).transpose(1, 0, 2, 3, 4, 5)
    scale = 1.0 / math.sqrt(DA_QK_DIM)

    def one_block(qblk):
        s = jnp.einsum('bqhcd,bkhcd->bhcqk', qblk, k).astype(jnp.float32) * scale
        p = jax.nn.softmax(s, axis=-1)
        w = p[:, :, 0] - lam * p[:, :, 1]
        return jnp.einsum('bhqk,bkhe->bqhe', w.astype(v.dtype), v)

    o = lax.map(one_block, qb)
    return o.transpose(1, 0, 2, 3, 4).reshape(b, Lq, H, DA_V_DIM)


def ab_mixer(h, w_in, conv_w, conv_b, dt_bias, a_log, d_skip, ssm_gain, lam_vec, subln_gain, w_out,
             lam_init, rope, k_ctx, v_ctx, hf0, hb0):
    f32 = jnp.float32
    b, L, _ = h.shape
    proj = h @ w_in
    z, xbc, dt_raw, q, k, v = jnp.split(proj, PROJ_SPLITS, axis=-1)
    xbc = jax.nn.silu(depthwise_conv(xbc, conv_w, conv_b))
    xs, bs, cs = jnp.split(xbc, [D_SSM, D_SSM + SSM_GROUPS * SSM_STATE], axis=-1)
    xs = xs.reshape(b, L, SSM_HEADS, SSM_HEAD_DIM).astype(f32)
    bs = bs.reshape(b, L, SSM_GROUPS, SSM_STATE).astype(f32)
    cs = cs.reshape(b, L, SSM_GROUPS, SSM_STATE).astype(f32)
    dt = jax.nn.softplus(dt_raw.reshape(b, L, 2, SSM_HEADS).astype(f32) + dt_bias.astype(f32))
    A = -jnp.exp(a_log.astype(f32))
    dsk = d_skip.astype(f32)
    if hf0 is None:
        hf0 = jnp.zeros((b, SSM_HEADS, SSM_HEAD_DIM, SSM_STATE), f32)
        hb0 = jnp.zeros((b, SSM_HEADS, SSM_HEAD_DIM, SSM_STATE), f32)
    flip = lambda a: jnp.flip(a, axis=1)
    y_f, hf = ssd_scan(xs, dt[:, :, 0], A[0], bs, cs, hf0.astype(f32))
    y_b, hb = ssd_scan(flip(xs), flip(dt[:, :, 1]), A[1], flip(bs), flip(cs), hb0.astype(f32))
    y = y_f + flip(y_b) + (dsk[0] + dsk[1])[:, None] * xs
    y = y.reshape(b, L, D_SSM) * jax.nn.silu(z.astype(f32))
    y = rmsnorm(y, ssm_gain).astype(h.dtype)
    q = q.reshape(b, L, DA_HEADS, 2, DA_QK_DIM)
    k = k.reshape(b, L, DA_HEADS, 2, DA_QK_DIM)
    v = v.reshape(b, L, DA_HEADS, DA_V_DIM)
    if rope is not None:
        q = apply_axial_rope(q, rope[0], rope[1])
        k_att = apply_axial_rope(k, rope[0], rope[1])
    else:
        k_att = k
    lv = lam_vec.astype(f32)
    lam = jnp.exp(jnp.sum(lv[0] * lv[1])) - jnp.exp(jnp.sum(lv[2] * lv[3])) + lam_init
    if k_ctx is not None:
        keys = jnp.concatenate([k_ctx, k_att], axis=1)
        vals = jnp.concatenate([v_ctx, v], axis=1)
    else:
        keys, vals = k_att, v
    o = diff_attention(q, keys, vals, lam)
    o = rmsnorm(o, subln_gain) * (1.0 - lam_init)
    out = jnp.concatenate([y, o.reshape(b, L, DA_HEADS * DA_V_DIM).astype(y.dtype)], axis=-1) @ w_out
    return out, k, v, hf.astype(h.dtype), hb.astype(h.dtype)


def pool_mixer(h, pool_w, pool_b, pool_scale):
    b, L, D = h.shape
    hf = h.astype(jnp.float32)
    csum = jnp.concatenate([jnp.zeros((b, 1, D), jnp.float32), jnp.cumsum(hf, axis=1)], axis=1)
    t = jnp.arange(L)
    outs = []
    for g, w in enumerate(POOL_WINDOWS):
        lo = jnp.clip(t - w // 2, 0, L)
        hi = jnp.clip(t + w - w // 2, 0, L)
        cg = csum[..., g * POOL_GROUP:(g + 1) * POOL_GROUP]
        s = jnp.take(cg, hi, axis=1) - jnp.take(cg, lo, axis=1)
        mean = s / (hi - lo).astype(jnp.float32)[None, :, None]
        outs.append(mean - hf[..., g * POOL_GROUP:(g + 1) * POOL_GROUP])
    pooled = jnp.stack(outs, axis=2).astype(h.dtype)
    mixed = jnp.einsum('blgc,gce->blge', pooled, pool_w) + pool_b
    return mixed.reshape(b, L, D) * pool_scale


def moe_ffn(h, w_coarse, b_coarse, w_fine, b_fine, w_gate, w_up, w_down):
    f32 = jnp.float32
    T, D = h.shape
    pc = jax.nn.softmax((h @ w_coarse).astype(f32) + b_coarse.astype(f32), axis=-1)
    g_idx = jnp.argmax(pc, axis=-1)
    g_p = jnp.max(pc, axis=-1)
    fl = ((h @ w_fine).astype(f32) + b_fine.astype(f32)).reshape(T, MOE_GROUPS, EXPERTS_PER_GROUP)
    fl = jnp.take_along_axis(fl, g_idx[:, None, None], axis=1)[:, 0]
    pf = jax.nn.softmax(fl, axis=-1)
    top_p, top_i = lax.top_k(pf, MOE_TOP_K)
    gate = g_p[:, None] * top_p / jnp.sum(top_p, axis=-1, keepdims=True)
    expert = (g_idx[:, None] * EXPERTS_PER_GROUP + top_i).reshape(-1).astype(jnp.int32)
    weight = gate.reshape(-1)
    tok = jnp.repeat(jnp.arange(T, dtype=jnp.int32), MOE_TOP_K)
    M = T * MOE_TOP_K
    NB = -(-M // MOE_BLOCK) + N_EXPERTS
    P = NB * MOE_BLOCK
    order = jnp.argsort(expert)
    e_sorted = expert[order]
    counts = jnp.bincount(expert, length=N_EXPERTS)
    padded = ((counts + MOE_BLOCK - 1) // MOE_BLOCK) * MOE_BLOCK
    start = jnp.cumsum(counts) - counts
    pad_end = jnp.cumsum(padded)
    pad_start = pad_end - padded
    dest = pad_start[e_sorted] + (jnp.arange(M, dtype=jnp.int32) - start[e_sorted])
    slot_tok = jnp.full((P,), T, jnp.int32).at[dest].set(tok[order])
    slot_w = jnp.zeros((P,), f32).at[dest].set(weight[order])
    block_expert = jnp.clip(jnp.searchsorted(pad_end, jnp.arange(NB) * MOE_BLOCK, side='right'),
                            0, N_EXPERTS - 1)
    h_pad = jnp.concatenate([h, jnp.zeros((1, D), h.dtype)], axis=0)
    xb = h_pad[slot_tok].reshape(NB, MOE_BLOCK, D)

    def expert_block(args):
        xblk, e = args
        a = xblk @ w_gate[e]
        u = xblk @ w_up[e]
        return (jax.nn.silu(a) * u) @ w_down[e]

    yb = lax.map(expert_block, (xb, block_expert))
    y = jnp.zeros((T + 1, D), f32).at[slot_tok].add(yb.reshape(P, D).astype(f32) * slot_w[:, None])
    return y[:T].astype(h.dtype)


def setup_inputs(seed: int = 0) -> dict:
    key = jax.random.key(seed)
    ks = list(jax.random.split(key, 40))
    f32 = jnp.float32
    nrm = lambda k, shape, s: jax.random.normal(k, shape, f32) * s
    NA, NC = N_AB_LAYERS, N_C_LAYERS
    dt_init = jnp.exp(jax.random.uniform(ks[0], (NA, 2, SSM_HEADS), f32,
                                         minval=math.log(DT_MIN), maxval=math.log(DT_MAX)))
    return {
        'x_prompt': nrm(ks[1], (BATCH, SEQ, D_MODEL), 1.0),
        'x_sample': nrm(ks[2], (DEC_BATCH, DEC_SEQ, D_MODEL), 1.0),
        'cache_k': nrm(ks[3], (DEC_BATCH, NA, PAST_LEN, DA_HEADS, 2, DA_QK_DIM), 1.0),
        'cache_v': nrm(ks[4], (DEC_BATCH, NA, PAST_LEN, DA_HEADS, DA_V_DIM), 1.0),
        'state_ssm_fwd': nrm(ks[5], (DEC_BATCH, NA, SSM_HEADS, SSM_HEAD_DIM, SSM_STATE), 0.1),
        'state_ssm_bwd': nrm(ks[6], (DEC_BATCH, NA, SSM_HEADS, SSM_HEAD_DIM, SSM_STATE), 0.1),
        'c': nrm(ks[7], (DEC_BATCH, D_MODEL), 1.0),
        'c_ctx': nrm(ks[8], (D_MODEL,), 1.0),
        'w_mod': nrm(ks[9], (DEPTH, D_MODEL, N_MOD * D_MODEL), 0.5 * D_MODEL ** -0.5),
        'b_mod': nrm(ks[10], (DEPTH, N_MOD * D_MODEL), 0.02),
        'norm_mix': 1.0 + nrm(ks[11], (DEPTH, D_MODEL), 0.02),
        'norm_ffn': 1.0 + nrm(ks[12], (DEPTH, D_MODEL), 0.02),
        'norm_final': 1.0 + nrm(ks[13], (D_MODEL,), 0.02),
        'w_in': nrm(ks[14], (NA, D_MODEL, IN_COLS), D_MODEL ** -0.5),
        'conv_w': nrm(ks[15], (NA, CONV_WIDTH, XBC_DIM), CONV_WIDTH ** -0.5),
        'conv_b': nrm(ks[16], (NA, XBC_DIM), 0.02),
        'dt_bias': dt_init + jnp.log(-jnp.expm1(-dt_init)),
        'a_log': jnp.log(jax.random.uniform(ks[17], (NA, 2, SSM_HEADS), f32, minval=1.0, maxval=16.0)),
        'd_skip': 1.0 + nrm(ks[18], (NA, 2, SSM_HEADS), 0.1),
        'ssm_gain': 1.0 + nrm(ks[19], (NA, D_SSM), 0.02),
        'lam_vec': nrm(ks[20], (NA, 4, DA_QK_DIM), 0.1),
        'subln_gain': 1.0 + nrm(ks[21], (NA, DA_V_DIM), 0.02),
        'w_out': nrm(ks[22], (NA, MIX_OUT, D_MODEL), MIX_OUT ** -0.5),
        'pool_w': nrm(ks[23], (NC, 4, POOL_GROUP, POOL_GROUP), POOL_GROUP ** -0.5),
        'pool_b': nrm(ks[24], (NC, 4, POOL_GROUP), 0.02),
        'pool_scale': 1.0 + nrm(ks[25], (NC, D_MODEL), 0.02),
        'w_coarse': nrm(ks[26], (DEPTH, D_MODEL, MOE_GROUPS), D_MODEL ** -0.5),
        'b_coarse': nrm(ks[27], (DEPTH, MOE_GROUPS), 0.01),
        'w_fine': nrm(ks[28], (DEPTH, D_MODEL, N_EXPERTS), D_MODEL ** -0.5),
        'b_fine': nrm(ks[29], (DEPTH, N_EXPERTS), 0.01),
        'w_gate': nrm(ks[30], (DEPTH, N_EXPERTS, D_MODEL, EXPERT_HIDDEN), D_MODEL ** -0.5),
        'w_up': nrm(ks[31], (DEPTH, N_EXPERTS, D_MODEL, EXPERT_HIDDEN), D_MODEL ** -0.5),
        'w_down': nrm(ks[32], (DEPTH, N_EXPERTS, EXPERT_HIDDEN, D_MODEL), EXPERT_HIDDEN ** -0.5),
    }


def reference(x_prompt, x_sample, cache_k, cache_v, state_ssm_fwd, state_ssm_bwd, c, c_ctx,
              w_mod, b_mod, norm_mix, norm_ffn, norm_final, w_in, conv_w, conv_b, dt_bias, a_log,
              d_skip, ssm_gain, lam_vec, subln_gain, w_out, pool_w, pool_b, pool_scale,
              w_coarse, b_coarse, w_fine, b_fine, w_gate, w_up, w_down):

    def trunk(x, cvec, latent):
        b, L, _ = x.shape
        rope = axial_rope_tables(L, x.dtype) if latent else None
        ks, vs, sf, sb = [], [], [], []
        for l in range(DEPTH):
            j = l // 2
            m = (jax.nn.silu(cvec) @ w_mod[l] + b_mod[l]).reshape(-1, 1, N_MOD, D_MODEL)
            h = rmsnorm(x, norm_mix[l]) * (1.0 + m[:, :, 1]) + m[:, :, 0]
            if l % 2 == 0:
                lam_init = 0.8 - 0.6 * math.exp(-0.3 * l)
                if latent:
                    ctx = (cache_k[:, j], cache_v[:, j], state_ssm_fwd[:, j], state_ssm_bwd[:, j])
                else:
                    ctx = (None, None, None, None)
                out, k, v, hf, hb = ab_mixer(h, w_in[j], conv_w[j], conv_b[j], dt_bias[j], a_log[j],
                                             d_skip[j], ssm_gain[j], lam_vec[j], subln_gain[j], w_out[j],
                                             lam_init, rope, *ctx)
                if not latent:
                    ks.append(k)
                    vs.append(v)
                    sf.append(hf)
                    sb.append(hb)
            else:
                out = pool_mixer(h, pool_w[j], pool_b[j], pool_scale[j])
            x = x + m[:, :, 2] * out
            h = rmsnorm(x, norm_ffn[l]) * (1.0 + m[:, :, 4]) + m[:, :, 3]
            ffn = moe_ffn(h.reshape(b * L, D_MODEL), w_coarse[l], b_coarse[l], w_fine[l], b_fine[l],
                          w_gate[l], w_up[l], w_down[l]).reshape(b, L, D_MODEL)
            x = x + m[:, :, 5] * ffn
        return rmsnorm(x, norm_final), ks, vs, sf, sb

    y_prompt, ks, vs, sf, sb = trunk(x_prompt, c_ctx[None, :], False)
    new_cache_k = jnp.stack(ks, axis=1)
    new_cache_v = jnp.stack(vs, axis=1)
    new_state_ssm_fwd = jnp.stack(sf, axis=1)
    new_state_ssm_bwd = jnp.stack(sb, axis=1)
    y_sample = trunk(x_sample, c, True)[0]
    return (y_prompt, y_sample, new_cache_k, new_cache_v, new_state_ssm_fwd, new_state_ssm_bwd)
```

```python
import functools
import math

import jax
import jax.numpy as jnp
from jax import lax
from jax.experimental import pallas as pl
from jax.experimental.pallas import tpu as pltpu

F32 = jnp.float32
BF16 = jnp.bfloat16
I32 = jnp.int32
HIGHEST = lax.Precision.HIGHEST

EPS = 1e-6
N_MOD = 6
SSM_HEAD_DIM = 64
SSM_GROUPS = 4
SSM_STATE = 128
CHUNK = 128
CONV_WIDTH = 5
CONV_HALO = 8
DA_QK_DIM = 64
DA_V_DIM = 128
GRID_W = 64
ROPE_FREQS = 16
ROPE_THETA = 10000.0
POOL_WINDOWS = (2, 4, 8, 16)
MOE_GROUPS = 4
EXPERTS_PER_GROUP = 8
MOE_TOP_K = 2

TOKEN_TILE = 256
EXPERT_ROWS = 256
ROUTER_LANES = 128
VMEM_LIMIT_BYTES = 56 * 1024 * 1024

NT_DIMS = (((1,), (1,)), ((), ()))
TN_DIMS = (((0,), (0,)), ((), ()))


def _params(*sem):
    return pltpu.CompilerParams(dimension_semantics=sem, vmem_limit_bytes=VMEM_LIMIT_BYTES)


def _silu(x):
    return x * jax.nn.sigmoid(x)


def _softplus(x):
    return jnp.maximum(x, 0.0) + jnp.log1p(jnp.exp(-jnp.abs(x)))


def _rms(x, g):
    return x * lax.rsqrt(jnp.mean(x * x, axis=-1, keepdims=True) + EPS) * g


def _modnorm(x, g, scale, shift):
    return _rms(x, g) * (1.0 + scale) + shift


def _bdot(a, b):
    return jnp.dot(a, b, preferred_element_type=F32)


class _Rows:
    def __init__(self, n_ctx, len_ctx, n_lat, len_lat):
        self.n_ctx, self.len_ctx, self.n_lat, self.len_lat = n_ctx, len_ctx, n_lat, len_lat
        self.t_ctx = n_ctx * len_ctx
        self.t_lat = n_lat * len_lat
        self.t = self.t_ctx + self.t_lat
        assert len_ctx % TOKEN_TILE == 0 and len_lat % TOKEN_TILE == 0
        self.ctx_tiles = self.t_ctx // TOKEN_TILE
        self.tiles = self.t // TOKEN_TILE
        self.tiles_per_ctx = len_ctx // TOKEN_TILE
        self.tiles_per_lat = len_lat // TOKEN_TILE

    def mod_row(self, i):
        return jnp.where(i < self.ctx_tiles, 0, 1 + (i - self.ctx_tiles) // self.tiles_per_lat)

    def seq_pos(self, i):
        is_ctx = i < self.ctx_tiles
        pos = jnp.where(is_ctx, i % self.tiles_per_ctx, (i - self.ctx_tiles) % self.tiles_per_lat)
        n = jnp.where(is_ctx, self.tiles_per_ctx, self.tiles_per_lat)
        return pos, n


def _row_spec(width, tile=TOKEN_TILE):
    return pl.BlockSpec((tile, width), lambda i: (i, 0))


def _const_spec(shape):
    nd = len(shape)
    return pl.BlockSpec(shape, lambda i: (0,) * nd)


def _mod_kernel(c_ref, w_ref, b_ref, o_ref):
    s = _silu(c_ref[...])
    o_ref[0] = jnp.dot(s, w_ref[0], precision=HIGHEST, preferred_element_type=F32) + b_ref[0]


def _modulation(cvecs, w_mod, b_mod):
    depth, d, n = w_mod.shape
    tn = n // 4
    return pl.pallas_call(
        _mod_kernel,
        grid=(depth, n // tn),
        in_specs=[pl.BlockSpec(cvecs.shape, lambda l, j: (0, 0)),
                  pl.BlockSpec((1, d, tn), lambda l, j: (l, 0, j)),
                  pl.BlockSpec((1, 1, tn), lambda l, j: (l, 0, j))],
        out_specs=pl.BlockSpec((1, cvecs.shape[0], tn), lambda l, j: (l, 0, j)),
        out_shape=jax.ShapeDtypeStruct((depth, cvecs.shape[0], n), F32),
        compiler_params=_params("parallel", "parallel"),
        name="modulation",
    )(cvecs, w_mod, b_mod.reshape(depth, 1, n))


def _inproj_kernel(x_ref, mod_ref, g_ref, wz, wxbc, wdt, wdtT, wq, wk, wv,
                   z_o, xbc_o, dt_o, dtT_o, q_o, k_o, v_o):
    m = mod_ref[0]
    h = _modnorm(x_ref[...], g_ref[...], m[1:2], m[0:1]).astype(BF16)
    z_o[...] = _bdot(h, wz[...])
    xbc_o[...] = _bdot(h, wxbc[...])
    dt_o[...] = _bdot(h, wdt[...])
    dtT_o[...] = lax.dot_general(wdtT[...], h, NT_DIMS, preferred_element_type=F32)
    q_o[...] = _bdot(h, wq[...])
    k_o[...] = _bdot(h, wk[...])
    v_o[...] = _bdot(h, wv[...])


def _inproj(rows, x, mod_l, g, weights):
    t, d = x.shape
    wz, wxbc, wdt, wdtT, wq, wk, wv = weights
    widths = [wz.shape[1], wxbc.shape[1], wdt.shape[1], None, wq.shape[1], wk.shape[1], wv.shape[1]]
    out_shape, out_specs = [], []
    for w in widths:
        if w is None:
            out_shape.append(jax.ShapeDtypeStruct((wdtT.shape[0], t), F32))
            out_specs.append(pl.BlockSpec((wdtT.shape[0], TOKEN_TILE), lambda i: (0, i)))
        else:
            out_shape.append(jax.ShapeDtypeStruct((t, w), F32))
            out_specs.append(_row_spec(w))
    return pl.pallas_call(
        _inproj_kernel,
        grid=(rows.tiles,),
        in_specs=[_row_spec(d),
                  pl.BlockSpec((1, N_MOD, d), lambda i: (rows.mod_row(i), 0, 0)),
                  _const_spec((1, d))] + [_const_spec(w.shape) for w in weights],
        out_specs=out_specs,
        out_shape=out_shape,
        compiler_params=_params("parallel"),
        name="inproj",
    )(x, mod_l, g.reshape(1, d), *weights)


def _conv_kernel(cur_ref, prev_ref, next_ref, w_ref, b_ref, xs_o, b_o, c_o, ext_ref, *, rows, d_ssm, gn):
    i = pl.program_id(0)
    pos, n = rows.seq_pos(i)
    tile = cur_ref.shape[0]
    ext_ref[0:CONV_HALO, :] = jnp.where(pos == 0, 0.0, prev_ref[...])
    ext_ref[CONV_HALO:CONV_HALO + tile, :] = cur_ref[...]
    ext_ref[CONV_HALO + tile:, :] = jnp.where(pos == n - 1, 0.0, next_ref[...])
    width = cur_ref.shape[1]
    step = 512
    for c0 in range(0, width, step):
        acc = jnp.broadcast_to(b_ref[:, c0:c0 + step], (tile, step))
        for k in range(CONV_WIDTH):
            off = CONV_HALO - CONV_WIDTH // 2 + k
            acc = acc + w_ref[k:k + 1, c0:c0 + step] * ext_ref[off:off + tile, c0:c0 + step]
        y = _silu(acc)
        if c0 < d_ssm:
            xs_o[:, c0:c0 + step] = y
        elif c0 < d_ssm + gn:
            b_o[:, c0 - d_ssm:c0 - d_ssm + step] = y.astype(BF16)
        else:
            c_o[:, c0 - d_ssm - gn:c0 - d_ssm - gn + step] = y.astype(BF16)


def _conv(rows, xbc, conv_w, conv_b, d_ssm):
    t, width = xbc.shape
    gn = (width - d_ssm) // 2
    hb = TOKEN_TILE // CONV_HALO
    last = t // CONV_HALO - 1
    return pl.pallas_call(
        functools.partial(_conv_kernel, rows=rows, d_ssm=d_ssm, gn=gn),
        grid=(rows.tiles,),
        in_specs=[_row_spec(width),
                  pl.BlockSpec((CONV_HALO, width), lambda i: (jnp.maximum(i * hb - 1, 0), 0)),
                  pl.BlockSpec((CONV_HALO, width), lambda i: (jnp.minimum((i + 1) * hb, last), 0)),
                  _const_spec(conv_w.shape), _const_spec((1, width))],
        out_specs=[_row_spec(d_ssm), _row_spec(gn), _row_spec(gn)],
        out_shape=[jax.ShapeDtypeStruct((t, d_ssm), F32),
                   jax.ShapeDtypeStruct((t, gn), BF16),
                   jax.ShapeDtypeStruct((t, gn), BF16)],
        scratch_shapes=[pltpu.VMEM((TOKEN_TILE + 2 * CONV_HALO, width), F32)],
        compiler_params=_params("parallel"),
        name="conv",
    )(xbc, xbc, xbc, conv_w, conv_b.reshape(1, width))


def _ssd_direction(x_ref, b_ref, c_ref, dt_ref, dtT_ref, bias_row, bias_col, a_row, a_col,
                   h_ref, y_ref, reverse):
    q = CHUNK
    heads = a_row.shape[1]
    dt_c = _softplus(dt_ref[...] + bias_row)
    dt_r = _softplus(dtT_ref[...] + bias_col)
    ii = lax.broadcasted_iota(I32, (q, q), 0)
    jj = lax.broadcasted_iota(I32, (q, q), 1)
    keep = (jj >= ii) if reverse else (jj <= ii)
    tri = keep.astype(F32)
    acs_c = jnp.dot(tri, dt_c * a_row, precision=HIGHEST, preferred_element_type=F32)
    acs_r = lax.dot_general(dt_r * a_col, tri, NT_DIMS, precision=HIGHEST, preferred_element_type=F32)
    total = acs_c[0:1, :] if reverse else acs_c[q - 1:q, :]
    to_end = jnp.exp(total - acs_c)
    grow = jnp.exp(acs_c)
    carry = jnp.exp(total)
    per_group = heads // SSM_GROUPS
    p = SSM_HEAD_DIM
    for g in range(SSM_GROUPS):
        bg = b_ref[:, g * SSM_STATE:(g + 1) * SSM_STATE]
        cg = c_ref[:, g * SSM_STATE:(g + 1) * SSM_STATE]
        cb = lax.dot_general(cg, bg, NT_DIMS, preferred_element_type=F32)
        for r in range(per_group):
            h = g * per_group + r
            seg = acs_c[:, h:h + 1] - acs_r[h:h + 1, :]
            decay = jnp.exp(jnp.where(keep, seg, -jnp.inf))
            lmat = (cb * decay).astype(BF16)
            xdt = x_ref[:, h * p:(h + 1) * p] * dt_c[:, h:h + 1]
            y_diag = _bdot(lmat, xdt.astype(BF16))
            h_in = h_ref[h * p:(h + 1) * p, :]
            y_off = lax.dot_general(cg, h_in.astype(BF16), NT_DIMS, preferred_element_type=F32)
            y_ref[:, h * p:(h + 1) * p] = y_diag + y_off * grow[:, h:h + 1]
            xw = (xdt * to_end[:, h:h + 1]).astype(BF16)
            state = lax.dot_general(xw, bg, TN_DIMS, preferred_element_type=F32)
            h_ref[h * p:(h + 1) * p, :] = h_in * carry[:, h:h + 1] + state


def _ssd_kernel(xf, bf, cf, dtf, dtTf, xb, bb, cb, dtb, dtTb, bias_ref, biasT_ref, alog_ref, alogT_ref,
                hf0, hb0, yf_o, yb_o, hf_o, hb_o, hf_scr, hb_scr, *, geom):
    s = pl.program_id(0)
    is_ctx, _, c, nc, _ = geom(s)
    heads = alog_ref.shape[1]

    @pl.when(c == 0)
    def _():
        hf_scr[...] = jnp.where(is_ctx, 0.0, hf0[0])
        hb_scr[...] = jnp.where(is_ctx, 0.0, hb0[0])

    a_row = -jnp.exp(alog_ref[...])
    a_col = -jnp.exp(alogT_ref[...])
    _ssd_direction(xf, bf, cf, dtf[:, 0:heads], dtTf[0:heads, :], bias_ref[0:1, :], biasT_ref[:, 0:1],
                   a_row[0:1, :], a_col[:, 0:1], hf_scr, yf_o, False)
    _ssd_direction(xb, bb, cb, dtb[:, heads:2 * heads], dtTb[heads:2 * heads, :], bias_ref[1:2, :],
                   biasT_ref[:, 1:2], a_row[1:2, :], a_col[:, 1:2], hb_scr, yb_o, True)

    @pl.when(c == nc - 1)
    def _():
        hf_o[0] = hf_scr[...]
        hb_o[0] = hb_scr[...]


def _ssd(rows, xs, bm, cm, dt, dtT, dt_bias, a_log, hf0, hb0):
    t, d_ssm = xs.shape
    gn = bm.shape[1]
    heads2 = dt.shape[1]
    ncc, ncl = rows.len_ctx // CHUNK, rows.len_lat // CHUNK
    s_ctx, s_lat = rows.n_ctx * ncc, rows.n_lat * ncl
    n_seq = rows.n_ctx + rows.n_lat

    def geom(s):
        is_ctx = s < s_ctx
        sl = s - s_ctx
        seq = jnp.where(is_ctx, s // ncc, rows.n_ctx + sl // ncl)
        c = jnp.where(is_ctx, s % ncc, sl % ncl)
        nc = jnp.where(is_ctx, ncc, ncl)
        base = jnp.where(is_ctx, (s // ncc) * ncc, s_ctx + (sl // ncl) * ncl)
        return is_ctx, seq, c, nc, base

    def fwd(s):
        _, _, c, _, base = geom(s)
        return base + c

    def bwd(s):
        _, _, c, nc, base = geom(s)
        return base + nc - 1 - c

    def lat_seq(s):
        return jnp.maximum(geom(s)[1] - rows.n_ctx, 0)

    def side(blk):
        return [pl.BlockSpec((CHUNK, d_ssm), lambda s: (blk(s), 0)),
                pl.BlockSpec((CHUNK, gn), lambda s: (blk(s), 0)),
                pl.BlockSpec((CHUNK, gn), lambda s: (blk(s), 0)),
                pl.BlockSpec((CHUNK, heads2), lambda s: (blk(s), 0)),
                pl.BlockSpec((heads2, CHUNK), lambda s: (0, blk(s)))]

    state_shape = (SSM_HEAD_DIM * (heads2 // 2), SSM_STATE)
    state_spec_in = pl.BlockSpec((1,) + state_shape, lambda s: (lat_seq(s), 0, 0))
    state_spec_out = pl.BlockSpec((1,) + state_shape, lambda s: (geom(s)[1], 0, 0))
    return pl.pallas_call(
        functools.partial(_ssd_kernel, geom=geom),
        grid=(s_ctx + s_lat,),
        in_specs=side(fwd) + side(bwd) + [_const_spec(dt_bias.shape), _const_spec(dt_bias.T.shape),
                                          _const_spec(a_log.shape), _const_spec(a_log.T.shape),
                                          state_spec_in, state_spec_in],
        out_specs=[pl.BlockSpec((CHUNK, d_ssm), lambda s: (fwd(s), 0)),
                   pl.BlockSpec((CHUNK, d_ssm), lambda s: (bwd(s), 0)),
                   state_spec_out, state_spec_out],
        out_shape=[jax.ShapeDtypeStruct((t, d_ssm), F32), jax.ShapeDtypeStruct((t, d_ssm), F32),
                   jax.ShapeDtypeStruct((n_seq,) + state_shape, F32),
                   jax.ShapeDtypeStruct((n_seq,) + state_shape, F32)],
        scratch_shapes=[pltpu.VMEM(state_shape, F32), pltpu.VMEM(state_shape, F32)],
        compiler_params=_params("arbitrary"),
        name="ssd",
    )(xs, bm, cm, dt, dtT, xs, bm, cm, dt, dtT, dt_bias, dt_bias.T, a_log, a_log.T, hf0, hb0)


def _prep_kernel(q_ref, k_ref, v_ref, cos_ref, sa_ref, sb_ref, qs_o, kr_o, vb_o, *, ctx_tiles):
    is_lat = pl.program_id(0) >= ctx_tiles
    cos = jnp.where(is_lat, cos_ref[...], 1.0)
    sa = jnp.where(is_lat, sa_ref[...], 0.0)
    sb = jnp.where(is_lat, sb_ref[...], 0.0)
    tile = q_ref.shape[0]
    first_map = lax.broadcasted_iota(I32, (tile, DA_V_DIM), 1) < DA_QK_DIM
    scale = 1.0 / math.sqrt(DA_QK_DIM)

    def rope(x):
        return (x * cos + pltpu.roll(x, DA_V_DIM - ROPE_FREQS, 1) * sa + pltpu.roll(x, ROPE_FREQS, 1) * sb)

    for h in range(q_ref.shape[1] // DA_V_DIM):
        sl = slice(h * DA_V_DIM, (h + 1) * DA_V_DIM)
        qr = rope(q_ref[:, sl]) * scale
        qs_o[0, :, sl] = jnp.where(first_map, qr, 0.0).astype(BF16)
        qs_o[1, :, sl] = jnp.where(first_map, 0.0, qr).astype(BF16)
        kr_o[:, sl] = rope(k_ref[:, sl]).astype(BF16)
    vb_o[...] = v_ref[...].astype(BF16)


def _rope_tables(length):
    t = jnp.arange(length)
    pos = jnp.stack([t // GRID_W, t % GRID_W], axis=-1).astype(F32)
    inv = ROPE_THETA ** (-jnp.arange(ROPE_FREQS, dtype=F32) / ROPE_FREQS)
    ang = pos[..., None] * inv
    lane = jnp.arange(DA_V_DIM)
    axis = (lane % DA_QK_DIM) // (2 * ROPE_FREQS)
    freq = lane % ROPE_FREQS
    second_half = (lane % (2 * ROPE_FREQS)) >= ROPE_FREQS
    a = ang[:, axis, freq]
    cos, sin = jnp.cos(a), jnp.sin(a)
    return cos, jnp.where(second_half, 0.0, -sin), jnp.where(second_half, sin, 0.0)


def _prep(rows, q, k, v):
    t, w = q.shape
    tables = _rope_tables(rows.len_lat)

    def tab(i):
        return (jnp.maximum(i - rows.ctx_tiles, 0) % rows.tiles_per_lat, 0)

    return pl.pallas_call(
        functools.partial(_prep_kernel, ctx_tiles=rows.ctx_tiles),
        grid=(rows.tiles,),
        in_specs=[_row_spec(w)] * 3 + [pl.BlockSpec((TOKEN_TILE, DA_V_DIM), tab)] * 3,
        out_specs=[pl.BlockSpec((2, TOKEN_TILE, w), lambda i: (0, i, 0)), _row_spec(w), _row_spec(w)],
        out_shape=[jax.ShapeDtypeStruct((2, t, w), BF16), jax.ShapeDtypeStruct((t, w), BF16),
                   jax.ShapeDtypeStruct((t, w), BF16)],
        compiler_params=_params("parallel"),
        name="attn_prep",
    )(q, k, v, *tables)


def _attn_kernel(*refs, tq, cache_steps, n_kv, lam_init):
    if cache_steps:
        qs_ref, kc_ref, vc_ref, k_ref, v_ref, lamv_ref, gain_ref, o_ref, m_scr, l_scr, acc_scr = refs
    else:
        qs_ref, k_ref, v_ref, lamv_ref, gain_ref, o_ref, m_scr, l_scr, acc_scr = refs
    kv = pl.program_id(3)

    @pl.when(kv == 0)
    def _():
        m_scr[...] = jnp.full(m_scr.shape, -jnp.inf, F32)
        l_scr[...] = jnp.zeros(l_scr.shape, F32)
        acc_scr[...] = jnp.zeros(acc_scr.shape, F32)

    q = qs_ref[...].reshape(2 * tq, DA_V_DIM)
    if cache_steps:
        use_cache = kv < cache_steps
        k = jnp.where(use_cache, kc_ref[0], k_ref[...])
        v = jnp.where(use_cache, vc_ref[0], v_ref[...])
    else:
        k, v = k_ref[...], v_ref[...]
    s = lax.dot_general(q, k, NT_DIMS, preferred_element_type=F32)
    m_prev = m_scr[...]
    m_new = jnp.maximum(m_prev, jnp.max(s, axis=1, keepdims=True))
    alpha = jnp.exp(m_prev - m_new)
    p = jnp.exp(s - m_new)
    l_scr[...] = alpha * l_scr[...] + jnp.sum(p, axis=1, keepdims=True)
    acc_scr[...] = alpha * acc_scr[...] + _bdot(p.astype(BF16), v)
    m_scr[...] = m_new

    @pl.when(kv == n_kv - 1)
    def _():
        lv = lamv_ref[...]
        lam = (jnp.exp(jnp.sum(lv[0:1] * lv[1:2], axis=1, keepdims=True))
               - jnp.exp(jnp.sum(lv[2:3] * lv[3:4], axis=1, keepdims=True)) + lam_init)
        on = acc_scr[...] / l_scr[...]
        o = on[0:tq] - lam * on[tq:2 * tq]
        o_ref[...] = (_rms(o, gain_ref[...]) * (1.0 - lam_init)).astype(BF16)


def _attention(qs, kr, vb, lam_vec, gain, lam_init, *, n_seq, length, row0, cache_k=None, cache_v=None):
    w = kr.shape[1]
    heads = w // DA_V_DIM
    tq = TOKEN_TILE
    tk = 512 if cache_k is not None else min(length, 512)
    assert length % tq == 0 and length % tk == 0 and row0 % tk == 0
    nq = length // tq
    cache_steps = 0
    if cache_k is not None:
        assert cache_k.shape[1] % tk == 0
        cache_steps = cache_k.shape[1] // tk
    n_kv = cache_steps + length // tk
    q_blk0, k_blk0 = row0 // tq, row0 // tk

    def q_map(b, h, i, j):
        return (0, q_blk0 + b * nq + i, h)

    def kv_map(b, h, i, j):
        return (k_blk0 + b * (length // tk) + jnp.maximum(j - cache_steps, 0), h)

    def cache_map(b, h, i, j):
        return (b, jnp.minimum(j, cache_steps - 1), h)

    in_specs = [pl.BlockSpec((2, tq, DA_V_DIM), q_map)]
    args = [qs]
    if cache_steps:
        in_specs += [pl.BlockSpec((1, tk, DA_V_DIM), cache_map)] * 2
        args += [cache_k, cache_v]
    in_specs += [pl.BlockSpec((tk, DA_V_DIM), kv_map)] * 2
    in_specs += [pl.BlockSpec(lam_vec.shape, lambda b, h, i, j: (0, 0)),
                 pl.BlockSpec((1, DA_V_DIM), lambda b, h, i, j: (0, 0))]
    args += [kr, vb, lam_vec, gain.reshape(1, DA_V_DIM)]
    return pl.pallas_call(
        functools.partial(_attn_kernel, tq=tq, cache_steps=cache_steps, n_kv=n_kv, lam_init=lam_init),
        grid=(n_seq, heads, nq, n_kv),
        in_specs=in_specs,
        out_specs=pl.BlockSpec((tq, DA_V_DIM), lambda b, h, i, j: (b * nq + i, h)),
        out_shape=jax.ShapeDtypeStruct((n_seq * length, w), BF16),
        scratch_shapes=[pltpu.VMEM((2 * tq, 1), F32), pltpu.VMEM((2 * tq, 1), F32),
                        pltpu.VMEM((2 * tq, DA_V_DIM), F32)],
        compiler_params=_params("parallel", "parallel", "parallel", "arbitrary"),
        name="attention_lat" if cache_steps else "attention_ctx",
    )(*args)


def _outproj_kernel(yf_ref, yb_ref, xs_ref, z_ref, oc_ref, ol_ref, x_ref, mod_ref, dsk_ref, gain_ref,
                    wy_ref, wo_ref, out_ref, *, ctx_tiles):
    is_ctx = pl.program_id(0) < ctx_tiles
    y = yf_ref[...] + yb_ref[...] + (dsk_ref[0:1, :] + dsk_ref[1:2, :]) * xs_ref[...]
    y = _rms(y * _silu(z_ref[...]), gain_ref[...]).astype(BF16)
    o = jnp.where(is_ctx, oc_ref[...], ol_ref[...])
    mixed = _bdot(y, wy_ref[...]) + _bdot(o, wo_ref[...])
    out_ref[...] = x_ref[...] + mod_ref[0][2:3] * mixed


def _outproj(rows, yf, yb, xs, z, o_ctx, o_lat, x, mod_l, d_skip, gain, wy, wo):
    t, d = x.shape
    d_ssm = xs.shape[1]
    wa = o_ctx.shape[1]
    return pl.pallas_call(
        functools.partial(_outproj_kernel, ctx_tiles=rows.ctx_tiles),
        grid=(rows.tiles,),
        in_specs=[_row_spec(d_ssm)] * 4 + [
            pl.BlockSpec((TOKEN_TILE, wa), lambda i: (jnp.minimum(i, rows.ctx_tiles - 1), 0)),
            pl.BlockSpec((TOKEN_TILE, wa), lambda i: (jnp.maximum(i - rows.ctx_tiles, 0), 0)),
            _row_spec(d),
            pl.BlockSpec((1, N_MOD, d), lambda i: (rows.mod_row(i), 0, 0)),
            _const_spec(d_skip.shape), _const_spec((1, d_ssm)), _const_spec(wy.shape), _const_spec(wo.shape)],
        out_specs=_row_spec(d),
        out_shape=jax.ShapeDtypeStruct((t, d), F32),
        compiler_params=_params("parallel"),
        name="outproj",
    )(yf, yb, xs, z, o_ctx, o_lat, x, mod_l, d_skip, gain.reshape(1, d_ssm), wy, wo)


def _pool_kernel(cur_ref, prev_ref, next_ref, mod_ref, g_ref, w_ref, b_ref, sc_ref, out_ref, ext_ref,
                 *, rows, len_ctx, len_lat):
    i = pl.program_id(0)
    pos, n = rows.seq_pos(i)
    tile = cur_ref.shape[0]
    m = mod_ref[0]
    g = g_ref[...]

    def norm(x):
        return _modnorm(x, g, m[1:2], m[0:1])

    x = cur_ref[...]
    ext_ref[0:CONV_HALO, :] = jnp.where(pos == 0, 0.0, norm(prev_ref[...]))
    ext_ref[CONV_HALO:CONV_HALO + tile, :] = norm(x)
    ext_ref[CONV_HALO + tile:, :] = jnp.where(pos == n - 1, 0.0, norm(next_ref[...]))
    length = jnp.where(i < rows.ctx_tiles, len_ctx, len_lat)
    tpos = pos * tile + lax.broadcasted_iota(I32, (tile, 1), 0)
    pg = cur_ref.shape[1] // len(POOL_WINDOWS)
    for gi, win in enumerate(POOL_WINDOWS):
        cs = slice(gi * pg, (gi + 1) * pg)
        half = win // 2
        acc = ext_ref[CONV_HALO - half:CONV_HALO - half + tile, cs]
        for off in range(1 - half, half):
            acc = acc + ext_ref[CONV_HALO + off:CONV_HALO + off + tile, cs]
        cnt = jnp.minimum(tpos + half, length) - jnp.maximum(tpos - half, 0)
        pooled = acc / cnt.astype(F32) - ext_ref[CONV_HALO:CONV_HALO + tile, cs]
        mixed = (_bdot(pooled.astype(BF16), w_ref[gi]) + b_ref[gi:gi + 1, :]) * sc_ref[:, cs]
        out_ref[:, cs] = x[:, cs] + m[2:3, cs] * mixed


def _pool(rows, x, mod_l, g, pool_w, pool_b, pool_scale):
    t, d = x.shape
    hb = TOKEN_TILE // CONV_HALO
    last = t // CONV_HALO - 1
    return pl.pallas_call(
        functools.partial(_pool_kernel, rows=rows, len_ctx=rows.len_ctx, len_lat=rows.len_lat),
        grid=(rows.tiles,),
        in_specs=[_row_spec(d),
                  pl.BlockSpec((CONV_HALO, d), lambda i: (jnp.maximum(i * hb - 1, 0), 0)),
                  pl.BlockSpec((CONV_HALO, d), lambda i: (jnp.minimum((i + 1) * hb, last), 0)),
                  pl.BlockSpec((1, N_MOD, d), lambda i: (rows.mod_row(i), 0, 0)),
                  _const_spec((1, d)), _const_spec(pool_w.shape), _const_spec(pool_b.shape),
                  _const_spec((1, d))],
        out_specs=_row_spec(d),
        out_shape=jax.ShapeDtypeStruct((t, d), F32),
        scratch_shapes=[pltpu.VMEM((TOKEN_TILE + 2 * CONV_HALO, d), F32)],
        compiler_params=_params("parallel"),
        name="pool",
    )(x, x, x, mod_l, g.reshape(1, d), pool_w, pool_b, pool_scale.reshape(1, d))


def _router_kernel(x_ref, mod_ref, g_ref, whi_ref, wlo_ref, bias_ref, hn_o, ri_o, rf_o, cnt_o, cnt_scr,
                   *, n_experts):
    i = pl.program_id(0)

    @pl.when(i == 0)
    def _():
        cnt_scr[...] = jnp.zeros(cnt_scr.shape, F32)

    m = mod_ref[0]
    h = _modnorm(x_ref[...], g_ref[...], m[4:5], m[3:4])
    hn_o[...] = h
    tm = h.shape[0]
    h_hi = h.astype(BF16)
    h_lo = (h - h_hi.astype(F32)).astype(BF16)
    lg = (lax.dot_general(whi_ref[...], h_hi, NT_DIMS, preferred_element_type=F32)
          + lax.dot_general(wlo_ref[...], h_hi, NT_DIMS, preferred_element_type=F32)
          + lax.dot_general(whi_ref[...], h_lo, NT_DIMS, preferred_element_type=F32)) + bias_ref[...]
    row = lax.broadcasted_iota(I32, (n_experts, tm), 0)
    big = jnp.int32(n_experts + MOE_GROUPS)
    lc = lg[n_experts:n_experts + 8, :]
    crow = lax.broadcasted_iota(I32, (8, tm), 0)
    lc = jnp.where(crow < MOE_GROUPS, lc, -jnp.inf)
    mc = jnp.max(lc, axis=0, keepdims=True)
    group_p = 1.0 / jnp.sum(jnp.exp(lc - mc), axis=0, keepdims=True)
    g_idx = jnp.min(jnp.where(lc == mc, crow, big), axis=0, keepdims=True)
    lf = jnp.where(row // EXPERTS_PER_GROUP == g_idx, lg[0:n_experts, :], -jnp.inf)
    m1 = jnp.max(lf, axis=0, keepdims=True)
    e1 = jnp.min(jnp.where(lf == m1, row, big), axis=0, keepdims=True)
    lf2 = jnp.where(row == e1, -jnp.inf, lf)
    m2 = jnp.max(lf2, axis=0, keepdims=True)
    e2 = jnp.min(jnp.where(lf2 == m2, row, big), axis=0, keepdims=True)
    r2 = jnp.exp(m2 - m1)
    gate1 = group_p / (1.0 + r2)
    gate2 = group_p * r2 / (1.0 + r2)
    oh1 = (row == e1)
    oh2 = (row == e2)
    oh = jnp.where(oh1 | oh2, 1.0, 0.0)
    before = (lax.broadcasted_iota(I32, (tm, tm), 0) < lax.broadcasted_iota(I32, (tm, tm), 1))
    prior = _bdot(oh.astype(BF16), before.astype(BF16)) + cnt_scr[...]
    rank1 = jnp.sum(jnp.where(oh1, prior, 0.0), axis=0, keepdims=True)
    rank2 = jnp.sum(jnp.where(oh2, prior, 0.0), axis=0, keepdims=True)
    cnt_scr[...] = cnt_scr[...] + jnp.sum(oh, axis=1, keepdims=True)
    zi = jnp.zeros((4, tm), I32)
    ri_o[0] = jnp.concatenate([e1, e2, rank1.astype(I32), rank2.astype(I32), zi], axis=0)
    rf_o[0] = jnp.concatenate([gate1, gate2, jnp.zeros((6, tm), F32)], axis=0)
    cnt_o[...] = jnp.broadcast_to(cnt_scr[...], cnt_o.shape)


def _router(rows, x, mod_l, g, w_hi, w_lo, bias, n_experts):
    t, d = x.shape
    tm = TOKEN_TILE
    nt = t // tm
    return pl.pallas_call(
        functools.partial(_router_kernel, n_experts=n_experts),
        grid=(nt,),
        in_specs=[_row_spec(d), pl.BlockSpec((1, N_MOD, d), lambda i: (rows.mod_row(i), 0, 0)),
                  _const_spec((1, d)), _const_spec(w_hi.shape), _const_spec(w_lo.shape),
                  _const_spec(bias.shape)],
        out_specs=[_row_spec(d), pl.BlockSpec((1, 8, tm), lambda i: (i, 0, 0)),
                   pl.BlockSpec((1, 8, tm), lambda i: (i, 0, 0)),
                   _const_spec((n_experts, ROUTER_LANES))],
        out_shape=[jax.ShapeDtypeStruct((t, d), F32), jax.ShapeDtypeStruct((nt, 8, tm), I32),
                   jax.ShapeDtypeStruct((nt, 8, tm), F32),
                   jax.ShapeDtypeStruct((n_experts, ROUTER_LANES), F32)],
        scratch_shapes=[pltpu.VMEM((n_experts, 1), F32)],
        compiler_params=_params("arbitrary"),
        name="router",
    )(x, mod_l, g.reshape(1, d), w_hi, w_lo, bias)


def _dispatch_kernel(dest_ref, hn_ref, xs_in_ref, xs_ref, sem):
    del xs_in_ref
    i = pl.program_id(0)
    tm = dest_ref.shape[2]

    def copy(row, slot):
        return pltpu.make_async_copy(hn_ref.at[pl.ds(row, 1)], xs_ref.at[pl.ds(slot, 1)], sem)

    def start(t, carry):
        for k in range(MOE_TOP_K):
            copy(i * tm + t, dest_ref[0, k, t]).start()
        return carry

    def wait(t, carry):
        for k in range(MOE_TOP_K):
            copy(0, 0).wait()
        return carry

    lax.fori_loop(0, tm, start, 0)
    lax.fori_loop(0, tm, wait, 0)


def _dispatch(hn, dest, n_slots):
    t, d = hn.shape
    nt, _, tm = dest.shape
    xs0 = jnp.zeros((n_slots, d), F32)
    return pl.pallas_call(
        _dispatch_kernel,
        grid=(nt,),
        in_specs=[pl.BlockSpec((1, MOE_TOP_K, tm), lambda i: (i, 0, 0), memory_space=pltpu.SMEM),
                  pl.BlockSpec(memory_space=pl.ANY), pl.BlockSpec(memory_space=pl.ANY)],
        out_specs=pl.BlockSpec(memory_space=pl.ANY),
        out_shape=jax.ShapeDtypeStruct((n_slots, d), F32),
        scratch_shapes=[pltpu.SemaphoreType.DMA(())],
        input_output_aliases={2: 0},
        compiler_params=_params("arbitrary"),
        name="moe_dispatch",
    )(dest, hn, xs0)


def _expert_kernel(be_ref, nu_ref, x_ref, wg_ref, wu_ref, wd_ref, y_ref):
    j = pl.program_id(0)

    @pl.when(j < nu_ref[0])
    def _():
        x = x_ref[...].astype(BF16)
        a = _bdot(x, wg_ref[0].astype(BF16))
        u = _bdot(x, wu_ref[0].astype(BF16))
        y_ref[...] = _bdot((_silu(a) * u).astype(BF16), wd_ref[0].astype(BF16))

    @pl.when(j >= nu_ref[0])
    def _():
        y_ref[...] = jnp.zeros(y_ref.shape, F32)


def _experts(xs, block_expert, n_used, w_gate, w_up, w_down):
    p, d = xs.shape
    hid = w_gate.shape[2]
    nb = p // EXPERT_ROWS
    return pl.pallas_call(
        _expert_kernel,
        grid_spec=pltpu.PrefetchScalarGridSpec(
            num_scalar_prefetch=2,
            grid=(nb,),
            in_specs=[pl.BlockSpec((EXPERT_ROWS, d), lambda j, be, nu: (j, 0)),
                      pl.BlockSpec((1, d, hid), lambda j, be, nu: (be[j], 0, 0)),
                      pl.BlockSpec((1, d, hid), lambda j, be, nu: (be[j], 0, 0)),
                      pl.BlockSpec((1, hid, d), lambda j, be, nu: (be[j], 0, 0))],
            out_specs=pl.BlockSpec((EXPERT_ROWS, d), lambda j, be, nu: (j, 0))),
        out_shape=jax.ShapeDtypeStruct((p, d), F32),
        compiler_params=_params("arbitrary"),
        name="moe_experts",
    )(block_expert, n_used, xs, w_gate, w_up, w_down)


def _combine_kernel(dest_ref, x_ref, mod_ref, gates_ref, ys_ref, out_ref, buf1, buf2, sem):
    tm = x_ref.shape[0]
    bufs = (buf1, buf2)

    def copy(slot, t, k):
        return pltpu.make_async_copy(ys_ref.at[pl.ds(slot, 1)], bufs[k].at[pl.ds(t, 1)], sem)

    def start(t, carry):
        for k in range(MOE_TOP_K):
            copy(dest_ref[0, k, t], t, k).start()
        return carry

    def wait(t, carry):
        for k in range(MOE_TOP_K):
            copy(0, 0, k).wait()
        return carry

    lax.fori_loop(0, tm, start, 0)
    eye = (lax.broadcasted_iota(I32, (tm, tm), 0) == lax.broadcasted_iota(I32, (tm, tm), 1))
    gates = gates_ref[0]
    g1 = jnp.sum(jnp.where(eye, gates[0:1, :], 0.0), axis=1, keepdims=True)
    g2 = jnp.sum(jnp.where(eye, gates[1:2, :], 0.0), axis=1, keepdims=True)
    lax.fori_loop(0, tm, wait, 0)
    ffn = g1 * buf1[...] + g2 * buf2[...]
    out_ref[...] = x_ref[...] + mod_ref[0][5:6] * ffn


def _combine(rows, x, mod_l, dest, gates, ys):
    t, d = x.shape
    nt, _, tm = dest.shape
    return pl.pallas_call(
        _combine_kernel,
        grid=(nt,),
        in_specs=[pl.BlockSpec((1, MOE_TOP_K, tm), lambda i: (i, 0, 0), memory_space=pltpu.SMEM),
                  _row_spec(d), pl.BlockSpec((1, N_MOD, d), lambda i: (rows.mod_row(i), 0, 0)),
                  pl.BlockSpec((1, 8, tm), lambda i: (i, 0, 0)),
                  pl.BlockSpec(memory_space=pl.ANY)],
        out_specs=_row_spec(d),
        out_shape=jax.ShapeDtypeStruct((t, d), F32),
        scratch_shapes=[pltpu.VMEM((tm, d), F32), pltpu.VMEM((tm, d), F32), pltpu.SemaphoreType.DMA(())],
        compiler_params=_params("arbitrary"),
        name="moe_combine",
    )(dest, x, mod_l, gates, ys)


def _moe(rows, x, mod_l, g, w_coarse, b_coarse, w_fine, b_fine, w_gate, w_up, w_down):
    t, d = x.shape
    n_experts = w_fine.shape[1]
    w_t = jnp.zeros((ROUTER_LANES, d), F32).at[:n_experts].set(w_fine.T)
    w_t = w_t.at[n_experts:n_experts + MOE_GROUPS].set(w_coarse.T)
    w_hi = w_t.astype(BF16)
    w_lo = (w_t - w_hi.astype(F32)).astype(BF16)
    bias = jnp.zeros((ROUTER_LANES, 1), F32).at[:n_experts, 0].set(b_fine)
    bias = bias.at[n_experts:n_experts + MOE_GROUPS, 0].set(b_coarse)
    hn, ri, gates, counts = _router(rows, x, mod_l, g, w_hi, w_lo, bias, n_experts)
    counts = counts[:, 0].astype(I32)
    padded = ((counts + EXPERT_ROWS - 1) // EXPERT_ROWS) * EXPERT_ROWS
    pad_end = jnp.cumsum(padded)
    pad_start = pad_end - padded
    nb = -(-(t * MOE_TOP_K) // EXPERT_ROWS) + n_experts
    block_expert = jnp.clip(jnp.searchsorted(pad_end, jnp.arange(nb, dtype=I32) * EXPERT_ROWS, side='right'),
                            0, n_experts - 1).astype(I32)
    n_used = (pad_end[-1:] // EXPERT_ROWS).astype(I32)
    dest = pad_start[ri[:, 0:MOE_TOP_K, :]] + ri[:, MOE_TOP_K:2 * MOE_TOP_K, :]
    xs = _dispatch(hn, dest, nb * EXPERT_ROWS)
    ys = _experts(xs, block_expert, n_used, w_gate, w_up, w_down)
    return _combine(rows, x, mod_l, dest, gates, ys)


def _final_kernel(x_ref, g_ref, o_ref):
    o_ref[...] = _rms(x_ref[...], g_ref[...])


def _final_norm(x, g, row0, n_rows):
    d = x.shape[1]
    blk0 = row0 // TOKEN_TILE
    return pl.pallas_call(
        _final_kernel,
        grid=(n_rows // TOKEN_TILE,),
        in_specs=[pl.BlockSpec((TOKEN_TILE, d), lambda i: (blk0 + i, 0)), _const_spec((1, d))],
        out_specs=_row_spec(d),
        out_shape=jax.ShapeDtypeStruct((n_rows, d), F32),
        compiler_params=_params("parallel"),
        name="final_norm",
    )(x, g.reshape(1, d))


def kernel(x_prompt, x_sample, cache_k, cache_v, state_ssm_fwd, state_ssm_bwd, c, c_ctx, w_mod, b_mod, norm_mix, norm_ffn, norm_final, w_in, conv_w, conv_b, dt_bias, a_log, d_skip, ssm_gain, lam_vec, subln_gain, w_out, pool_w, pool_b, pool_scale, w_coarse, b_coarse, w_fine, b_fine, w_gate, w_up, w_down):
    n_ctx, len_ctx, d = x_prompt.shape
    n_lat, len_lat, _ = x_sample.shape
    rows = _Rows(n_ctx, len_ctx, n_lat, len_lat)
    depth = w_mod.shape[0]
    d_ssm = ssm_gain.shape[1]
    heads = a_log.shape[2]
    da_heads = cache_k.shape[3]
    qk_cols = da_heads * 2 * DA_QK_DIM
    v_cols = da_heads * DA_V_DIM
    xbc_cols = conv_w.shape[2]
    past = cache_k.shape[2]

    x = jnp.concatenate([x_prompt.reshape(rows.t_ctx, d), x_sample.reshape(rows.t_lat, d)], axis=0)
    n_vec = -(-(1 + n_lat) // 8) * 8
    cvecs = jnp.zeros((n_vec, d), F32).at[0].set(c_ctx).at[1:1 + n_lat].set(c)
    mod = _modulation(cvecs, w_mod, b_mod).reshape(depth, n_vec, N_MOD, d)

    ks, vs, sf, sb = [], [], [], []
    for l in range(depth):
        j = l // 2
        if l % 2 == 0:
            lam_init = 0.8 - 0.6 * math.exp(-0.3 * l)
            splits = [d_ssm, d_ssm + xbc_cols, d_ssm + xbc_cols + 2 * heads]
            splits += [splits[-1] + qk_cols, splits[-1] + 2 * qk_cols]
            wz, wxbc, wdt, wq, wk, wv = [w.astype(BF16) for w in jnp.split(w_in[j], splits, axis=1)]
            z, xbc, dt, dtT, q, k, v = _inproj(rows, x, mod[l], norm_mix[l], (wz, wxbc, wdt, wdt.T, wq, wk, wv))
            xs, bm, cm = _conv(rows, xbc, conv_w[j], conv_b[j], d_ssm)
            hf0 = state_ssm_fwd[:, j].reshape(n_lat, heads * SSM_HEAD_DIM, SSM_STATE)
            hb0 = state_ssm_bwd[:, j].reshape(n_lat, heads * SSM_HEAD_DIM, SSM_STATE)
            yf, yb, hf, hb = _ssd(rows, xs, bm, cm, dt, dtT, dt_bias[j], a_log[j], hf0, hb0)
            qs, kr, vb = _prep(rows, q, k, v)
            o_ctx = _attention(qs, kr, vb, lam_vec[j], subln_gain[j], lam_init,
                               n_seq=n_ctx, length=len_ctx, row0=0)
            ck = cache_k[:, j].reshape(n_lat, past, qk_cols).astype(BF16)
            cv = cache_v[:, j].reshape(n_lat, past, v_cols).astype(BF16)
            o_lat = _attention(qs, kr, vb, lam_vec[j], subln_gain[j], lam_init,
                               n_seq=n_lat, length=len_lat, row0=rows.t_ctx, cache_k=ck, cache_v=cv)
            dsk = jnp.repeat(d_skip[j], SSM_HEAD_DIM, axis=1)
            wo = w_out[j].astype(BF16)
            x = _outproj(rows, yf, yb, xs, z, o_ctx, o_lat, x, mod[l], dsk, ssm_gain[j],
                         wo[:d_ssm], wo[d_ssm:])
            ks.append(k[:rows.t_ctx].reshape(n_ctx, len_ctx, da_heads, 2, DA_QK_DIM))
            vs.append(v[:rows.t_ctx].reshape(n_ctx, len_ctx, da_heads, DA_V_DIM))
            sf.append(hf[:n_ctx].reshape(n_ctx, heads, SSM_HEAD_DIM, SSM_STATE))
            sb.append(hb[:n_ctx].reshape(n_ctx, heads, SSM_HEAD_DIM, SSM_STATE))
        else:
            x = _pool(rows, x, mod[l], norm_mix[l], pool_w[j].astype(BF16), pool_b[j], pool_scale[j])
        x = _moe(rows, x, mod[l], norm_ffn[l], w_coarse[l], b_coarse[l], w_fine[l], b_fine[l],
                 w_gate[l], w_up[l], w_down[l])

    y_prompt = _final_norm(x, norm_final, 0, rows.t_ctx).reshape(n_ctx, len_ctx, d)
    y_sample = _final_norm(x, norm_final, rows.t_ctx, rows.t_lat).reshape(n_lat, len_lat, d)
    return (y_prompt, y_sample, jnp.stack(ks, axis=1), jnp.stack(vs, axis=1),
            jnp.stack(sf, axis=1), jnp.stack(sb, axis=1))
```

```python
import functools
import math

import jax
import jax.numpy as jnp
from jax import lax
from jax.experimental import pallas as pl
from jax.experimental.pallas import tpu as pltpu

F32 = jnp.float32
BF16 = jnp.bfloat16
I32 = jnp.int32
HIGHEST = lax.Precision.HIGHEST

EPS = 1e-6
N_MOD = 6
SSM_HEAD_DIM = 64
SSM_GROUPS = 4
SSM_STATE = 128
CHUNK = 128
CONV_WIDTH = 5
CONV_HALO = 8
DA_QK_DIM = 64
DA_V_DIM = 128
GRID_W = 64
ROPE_FREQS = 16
ROPE_THETA = 10000.0
POOL_WINDOWS = (2, 4, 8, 16)
MOE_GROUPS = 4
EXPERTS_PER_GROUP = 8
MOE_TOP_K = 2

TOKEN_TILE = 256
EXPERT_ROWS = 256
ATTN_KEY_BLOCK = 512
ROUTER_LANES = 128
VMEM_LIMIT_BYTES = 56 * 1024 * 1024

NT_DIMS = (((1,), (1,)), ((), ()))
TN_DIMS = (((0,), (0,)), ((), ()))


def _params(*sem):
    return pltpu.CompilerParams(dimension_semantics=sem, vmem_limit_bytes=VMEM_LIMIT_BYTES)


def _silu(x):
    return x * jax.nn.sigmoid(x)


def _softplus(x):
    return jnp.maximum(x, 0.0) + jnp.log1p(jnp.exp(-jnp.abs(x)))


def _rms(x, g):
    return x * lax.rsqrt(jnp.mean(x * x, axis=-1, keepdims=True) + EPS) * g


def _modnorm(x, g, scale, shift):
    return _rms(x, g) * (1.0 + scale) + shift


def _bdot(a, b):
    return jnp.dot(a, b, preferred_element_type=F32)


class _Rows:
    def __init__(self, n_ctx, len_ctx, n_lat, len_lat):
        self.n_ctx, self.len_ctx, self.n_lat, self.len_lat = n_ctx, len_ctx, n_lat, len_lat
        self.t_ctx = n_ctx * len_ctx
        self.t_lat = n_lat * len_lat
        self.t = self.t_ctx + self.t_lat
        assert len_ctx % TOKEN_TILE == 0 and len_lat % TOKEN_TILE == 0
        self.ctx_tiles = self.t_ctx // TOKEN_TILE
        self.tiles = self.t // TOKEN_TILE
        self.tiles_per_ctx = len_ctx // TOKEN_TILE
        self.tiles_per_lat = len_lat // TOKEN_TILE

    def mod_row(self, i):
        return jnp.where(i < self.ctx_tiles, 0, 1 + (i - self.ctx_tiles) // self.tiles_per_lat)

    def seq_pos(self, i):
        is_ctx = i < self.ctx_tiles
        pos = jnp.where(is_ctx, i % self.tiles_per_ctx, (i - self.ctx_tiles) % self.tiles_per_lat)
        n = jnp.where(is_ctx, self.tiles_per_ctx, self.tiles_per_lat)
        return pos, n


def _row_spec(width, tile=TOKEN_TILE):
    return pl.BlockSpec((tile, width), lambda i: (i, 0))


def _const_spec(shape):
    nd = len(shape)
    return pl.BlockSpec(shape, lambda i: (0,) * nd)


def _mod_kernel(c_ref, w_ref, b_ref, o_ref):
    s = _silu(c_ref[...])
    o_ref[0] = jnp.dot(s, w_ref[0], precision=HIGHEST, preferred_element_type=F32) + b_ref[0]


def _modulation(cvecs, w_mod, b_mod):
    depth, d, n = w_mod.shape
    tn = n // 4
    return pl.pallas_call(
        _mod_kernel,
        grid=(depth, n // tn),
        in_specs=[pl.BlockSpec(cvecs.shape, lambda l, j: (0, 0)),
                  pl.BlockSpec((1, d, tn), lambda l, j: (l, 0, j)),
                  pl.BlockSpec((1, 1, tn), lambda l, j: (l, 0, j))],
        out_specs=pl.BlockSpec((1, cvecs.shape[0], tn), lambda l, j: (l, 0, j)),
        out_shape=jax.ShapeDtypeStruct((depth, cvecs.shape[0], n), F32),
        compiler_params=_params("parallel", "parallel"),
        name="modulation",
    )(cvecs, w_mod, b_mod.reshape(depth, 1, n))


def _inproj_kernel(x_ref, mod_ref, g_ref, wz, wxbc, wdt, wdtT, wq, wk, wv,
                   z_o, xbc_o, dt_o, dtT_o, q_o, k_o, v_o):
    m = mod_ref[0]
    h = _modnorm(x_ref[...], g_ref[...], m[1:2], m[0:1]).astype(BF16)
    z_o[...] = _bdot(h, wz[...])
    xbc_o[...] = _bdot(h, wxbc[...])
    dt_o[...] = _bdot(h, wdt[...])
    dtT_o[...] = lax.dot_general(wdtT[...], h, NT_DIMS, preferred_element_type=F32)
    q_o[...] = _bdot(h, wq[...])
    k_o[...] = _bdot(h, wk[...])
    v_o[...] = _bdot(h, wv[...])


def _inproj(rows, x, mod_l, g, weights):
    t, d = x.shape
    wz, wxbc, wdt, wdtT, wq, wk, wv = weights
    widths = [wz.shape[1], wxbc.shape[1], wdt.shape[1], None, wq.shape[1], wk.shape[1], wv.shape[1]]
    out_shape, out_specs = [], []
    for w in widths:
        if w is None:
            out_shape.append(jax.ShapeDtypeStruct((wdtT.shape[0], t), F32))
            out_specs.append(pl.BlockSpec((wdtT.shape[0], TOKEN_TILE), lambda i: (0, i)))
        else:
            out_shape.append(jax.ShapeDtypeStruct((t, w), F32))
            out_specs.append(_row_spec(w))
    return pl.pallas_call(
        _inproj_kernel,
        grid=(rows.tiles,),
        in_specs=[_row_spec(d),
                  pl.BlockSpec((1, N_MOD, d), lambda i: (rows.mod_row(i), 0, 0)),
                  _const_spec((1, d))] + [_const_spec(w.shape) for w in weights],
        out_specs=out_specs,
        out_shape=out_shape,
        compiler_params=_params("parallel"),
        name="inproj",
    )(x, mod_l, g.reshape(1, d), *weights)


def _conv_kernel(cur_ref, prev_ref, next_ref, w_ref, b_ref, xs_o, b_o, c_o, ext_ref, *, rows, d_ssm, gn):
    i = pl.program_id(0)
    pos, n = rows.seq_pos(i)
    tile = cur_ref.shape[0]
    ext_ref[0:CONV_HALO, :] = jnp.where(pos == 0, 0.0, prev_ref[...])
    ext_ref[CONV_HALO:CONV_HALO + tile, :] = cur_ref[...]
    ext_ref[CONV_HALO + tile:, :] = jnp.where(pos == n - 1, 0.0, next_ref[...])
    width = cur_ref.shape[1]
    step = 512
    for c0 in range(0, width, step):
        acc = jnp.broadcast_to(b_ref[:, c0:c0 + step], (tile, step))
        for k in range(CONV_WIDTH):
            off = CONV_HALO - CONV_WIDTH // 2 + k
            acc = acc + w_ref[k:k + 1, c0:c0 + step] * ext_ref[off:off + tile, c0:c0 + step]
        y = _silu(acc)
        if c0 < d_ssm:
            xs_o[:, c0:c0 + step] = y
        elif c0 < d_ssm + gn:
            b_o[:, c0 - d_ssm:c0 - d_ssm + step] = y.astype(BF16)
        else:
            c_o[:, c0 - d_ssm - gn:c0 - d_ssm - gn + step] = y.astype(BF16)


def _conv(rows, xbc, conv_w, conv_b, d_ssm):
    t, width = xbc.shape
    gn = (width - d_ssm) // 2
    hb = TOKEN_TILE // CONV_HALO
    last = t // CONV_HALO - 1
    return pl.pallas_call(
        functools.partial(_conv_kernel, rows=rows, d_ssm=d_ssm, gn=gn),
        grid=(rows.tiles,),
        in_specs=[_row_spec(width),
                  pl.BlockSpec((CONV_HALO, width), lambda i: (jnp.maximum(i * hb - 1, 0), 0)),
                  pl.BlockSpec((CONV_HALO, width), lambda i: (jnp.minimum((i + 1) * hb, last), 0)),
                  _const_spec(conv_w.shape), _const_spec((1, width))],
        out_specs=[_row_spec(d_ssm), _row_spec(gn), _row_spec(gn)],
        out_shape=[jax.ShapeDtypeStruct((t, d_ssm), F32),
                   jax.ShapeDtypeStruct((t, gn), BF16),
                   jax.ShapeDtypeStruct((t, gn), BF16)],
        scratch_shapes=[pltpu.VMEM((TOKEN_TILE + 2 * CONV_HALO, width), F32)],
        compiler_params=_params("parallel"),
        name="conv",
    )(xbc, xbc, xbc, conv_w, conv_b.reshape(1, width))


def _ssd_direction(x_ref, b_ref, c_ref, dt_ref, dtT_ref, bias_row, bias_col, a_row, a_col,
                   h_ref, y_ref, reverse):
    q = CHUNK
    heads = a_row.shape[1]
    dt_c = _softplus(dt_ref[...] + bias_row)
    dt_r = _softplus(dtT_ref[...] + bias_col)
    ii = lax.broadcasted_iota(I32, (q, q), 0)
    jj = lax.broadcasted_iota(I32, (q, q), 1)
    keep = (jj >= ii) if reverse else (jj <= ii)
    tri = keep.astype(F32)
    acs_c = jnp.dot(tri, dt_c * a_row, precision=HIGHEST, preferred_element_type=F32)
    acs_r = lax.dot_general(dt_r * a_col, tri, NT_DIMS, precision=HIGHEST, preferred_element_type=F32)
    total = acs_c[0:1, :] if reverse else acs_c[q - 1:q, :]
    to_end = jnp.exp(total - acs_c)
    grow = jnp.exp(acs_c)
    carry = jnp.exp(total)
    per_group = heads // SSM_GROUPS
    p = SSM_HEAD_DIM
    for g in range(SSM_GROUPS):
        bg = b_ref[:, g * SSM_STATE:(g + 1) * SSM_STATE]
        cg = c_ref[:, g * SSM_STATE:(g + 1) * SSM_STATE]
        cb = lax.dot_general(cg, bg, NT_DIMS, preferred_element_type=F32)
        for r in range(per_group):
            h = g * per_group + r
            seg = acs_c[:, h:h + 1] - acs_r[h:h + 1, :]
            decay = jnp.exp(jnp.where(keep, seg, -jnp.inf))
            lmat = (cb * decay).astype(BF16)
            xdt = x_ref[:, h * p:(h + 1) * p] * dt_c[:, h:h + 1]
            y_diag = _bdot(lmat, xdt.astype(BF16))
            h_in = h_ref[h * p:(h + 1) * p, :]
            y_off = lax.dot_general(cg, h_in.astype(BF16), NT_DIMS, preferred_element_type=F32)
            y_ref[:, h * p:(h + 1) * p] = y_diag + y_off * grow[:, h:h + 1]
            xw = (xdt * to_end[:, h:h + 1]).astype(BF16)
            state = lax.dot_general(xw, bg, TN_DIMS, preferred_element_type=F32)
            h_ref[h * p:(h + 1) * p, :] = h_in * carry[:, h:h + 1] + state


def _ssd_kernel(xf, bf, cf, dtf, dtTf, xb, bb, cb, dtb, dtTb, bias_ref, biasT_ref, alog_ref, alogT_ref,
                hf0, hb0, yf_o, yb_o, hf_o, hb_o, hf_scr, hb_scr, *, geom):
    s = pl.program_id(0)
    is_ctx, _, c, nc, _ = geom(s)
    heads = alog_ref.shape[1]

    @pl.when(c == 0)
    def _():
        hf_scr[...] = jnp.where(is_ctx, 0.0, hf0[0])
        hb_scr[...] = jnp.where(is_ctx, 0.0, hb0[0])

    a_row = -jnp.exp(alog_ref[...])
    a_col = -jnp.exp(alogT_ref[...])
    _ssd_direction(xf, bf, cf, dtf[:, 0:heads], dtTf[0:heads, :], bias_ref[0:1, :], biasT_ref[:, 0:1],
                   a_row[0:1, :], a_col[:, 0:1], hf_scr, yf_o, False)
    _ssd_direction(xb, bb, cb, dtb[:, heads:2 * heads], dtTb[heads:2 * heads, :], bias_ref[1:2, :],
                   biasT_ref[:, 1:2], a_row[1:2, :], a_col[:, 1:2], hb_scr, yb_o, True)

    @pl.when(c == nc - 1)
    def _():
        hf_o[0] = hf_scr[...]
        hb_o[0] = hb_scr[...]


def _ssd(rows, xs, bm, cm, dt, dtT, dt_bias, a_log, hf0, hb0):
    t, d_ssm = xs.shape
    gn = bm.shape[1]
    heads2 = dt.shape[1]
    ncc, ncl = rows.len_ctx // CHUNK, rows.len_lat // CHUNK
    s_ctx, s_lat = rows.n_ctx * ncc, rows.n_lat * ncl
    n_seq = rows.n_ctx + rows.n_lat

    def geom(s):
        is_ctx = s < s_ctx
        sl = s - s_ctx
        seq = jnp.where(is_ctx, s // ncc, rows.n_ctx + sl // ncl)
        c = jnp.where(is_ctx, s % ncc, sl % ncl)
        nc = jnp.where(is_ctx, ncc, ncl)
        base = jnp.where(is_ctx, (s // ncc) * ncc, s_ctx + (sl // ncl) * ncl)
        return is_ctx, seq, c, nc, base

    def fwd(s):
        _, _, c, _, base = geom(s)
        return base + c

    def bwd(s):
        _, _, c, nc, base = geom(s)
        return base + nc - 1 - c

    def lat_seq(s):
        return jnp.maximum(geom(s)[1] - rows.n_ctx, 0)

    def side(blk):
        return [pl.BlockSpec((CHUNK, d_ssm), lambda s: (blk(s), 0)),
                pl.BlockSpec((CHUNK, gn), lambda s: (blk(s), 0)),
                pl.BlockSpec((CHUNK, gn), lambda s: (blk(s), 0)),
                pl.BlockSpec((CHUNK, heads2), lambda s: (blk(s), 0)),
                pl.BlockSpec((heads2, CHUNK), lambda s: (0, blk(s)))]

    state_shape = (SSM_HEAD_DIM * (heads2 // 2), SSM_STATE)
    state_spec_in = pl.BlockSpec((1,) + state_shape, lambda s: (lat_seq(s), 0, 0))
    state_spec_out = pl.BlockSpec((1,) + state_shape, lambda s: (geom(s)[1], 0, 0))
    return pl.pallas_call(
        functools.partial(_ssd_kernel, geom=geom),
        grid=(s_ctx + s_lat,),
        in_specs=side(fwd) + side(bwd) + [_const_spec(dt_bias.shape), _const_spec(dt_bias.T.shape),
                                          _const_spec(a_log.shape), _const_spec(a_log.T.shape),
                                          state_spec_in, state_spec_in],
        out_specs=[pl.BlockSpec((CHUNK, d_ssm), lambda s: (fwd(s), 0)),
                   pl.BlockSpec((CHUNK, d_ssm), lambda s: (bwd(s), 0)),
                   state_spec_out, state_spec_out],
        out_shape=[jax.ShapeDtypeStruct((t, d_ssm), F32), jax.ShapeDtypeStruct((t, d_ssm), F32),
                   jax.ShapeDtypeStruct((n_seq,) + state_shape, F32),
                   jax.ShapeDtypeStruct((n_seq,) + state_shape, F32)],
        scratch_shapes=[pltpu.VMEM(state_shape, F32), pltpu.VMEM(state_shape, F32)],
        compiler_params=_params("arbitrary"),
        name="ssd",
    )(xs, bm, cm, dt, dtT, xs, bm, cm, dt, dtT, dt_bias, dt_bias.T, a_log, a_log.T, hf0, hb0)


def _prep_kernel(q_ref, k_ref, v_ref, cos_ref, sa_ref, sb_ref, qs_o, kr_o, vt_o, *, ctx_tiles):
    is_lat = pl.program_id(0) >= ctx_tiles
    cos = jnp.where(is_lat, cos_ref[...], 1.0)
    sa = jnp.where(is_lat, sa_ref[...], 0.0)
    sb = jnp.where(is_lat, sb_ref[...], 0.0)
    tile = q_ref.shape[0]
    first_map = lax.broadcasted_iota(I32, (tile, DA_V_DIM), 1) < DA_QK_DIM
    scale = 1.0 / math.sqrt(DA_QK_DIM)

    def rope(x):
        return (x * cos + pltpu.roll(x, DA_V_DIM - ROPE_FREQS, 1) * sa + pltpu.roll(x, ROPE_FREQS, 1) * sb)

    for h in range(q_ref.shape[1] // DA_V_DIM):
        sl = slice(h * DA_V_DIM, (h + 1) * DA_V_DIM)
        qr = rope(q_ref[:, sl]) * scale
        qs_o[0, :, sl] = jnp.where(first_map, qr, 0.0).astype(BF16)
        qs_o[1, :, sl] = jnp.where(first_map, 0.0, qr).astype(BF16)
        kr_o[:, sl] = rope(k_ref[:, sl]).astype(BF16)
        vt_o[sl, :] = v_ref[:, sl].T.astype(BF16)


def _rope_tables(length):
    t = jnp.arange(length)
    pos = jnp.stack([t // GRID_W, t % GRID_W], axis=-1).astype(F32)
    inv = ROPE_THETA ** (-jnp.arange(ROPE_FREQS, dtype=F32) / ROPE_FREQS)
    ang = pos[..., None] * inv
    lane = jnp.arange(DA_V_DIM)
    axis = (lane % DA_QK_DIM) // (2 * ROPE_FREQS)
    freq = lane % ROPE_FREQS
    second_half = (lane % (2 * ROPE_FREQS)) >= ROPE_FREQS
    a = ang[:, axis, freq]
    cos, sin = jnp.cos(a), jnp.sin(a)
    return cos, jnp.where(second_half, 0.0, -sin), jnp.where(second_half, sin, 0.0)


def _prep(rows, q, k, v):
    t, w = q.shape
    tables = _rope_tables(rows.len_lat)

    def tab(i):
        return (jnp.maximum(i - rows.ctx_tiles, 0) % rows.tiles_per_lat, 0)

    return pl.pallas_call(
        functools.partial(_prep_kernel, ctx_tiles=rows.ctx_tiles),
        grid=(rows.tiles,),
        in_specs=[_row_spec(w)] * 3 + [pl.BlockSpec((TOKEN_TILE, DA_V_DIM), tab)] * 3,
        out_specs=[pl.BlockSpec((2, TOKEN_TILE, w), lambda i: (0, i, 0)), _row_spec(w),
                   pl.BlockSpec((w, TOKEN_TILE), lambda i: (0, i))],
        out_shape=[jax.ShapeDtypeStruct((2, t, w), BF16), jax.ShapeDtypeStruct((t, w), BF16),
                   jax.ShapeDtypeStruct((w, t), BF16)],
        compiler_params=_params("parallel"),
        name="attn_prep",
    )(q, k, v, *tables)


def _attn_kernel(*refs, tq, tk, has_cache, lam_init):
    if has_cache:
        qs_ref, kc_ref, vct_ref, k_ref, vt_ref, lamv_ref, gain_ref, o_ref = refs
    else:
        qs_ref, k_ref, vt_ref, lamv_ref, gain_ref, o_ref = refs
    q = qs_ref[...].reshape(2 * tq, DA_V_DIM)
    blocks = []
    if has_cache:
        blocks += [(kc_ref, vct_ref, c, True) for c in range(kc_ref.shape[1] // tk)]
    blocks += [(k_ref, vt_ref, c, False) for c in range(k_ref.shape[0] // tk)]
    m = jnp.full((1, 2 * tq), -jnp.inf, F32)
    l = jnp.zeros((1, 2 * tq), F32)
    acc = jnp.zeros((DA_V_DIM, 2 * tq), F32)
    for kref, vref, c, cached in blocks:
        ks = slice(c * tk, (c + 1) * tk)
        kb = kref[0, ks, :] if cached else kref[ks, :]
        vb = vref[0, :, ks] if cached else vref[:, ks]
        s = lax.dot_general(kb, q, NT_DIMS, preferred_element_type=F32)
        m_new = jnp.maximum(m, jnp.max(s, axis=0, keepdims=True))
        alpha = jnp.exp(m - m_new)
        p = jnp.exp(s - m_new)
        l = alpha * l + jnp.sum(p, axis=0, keepdims=True)
        acc = alpha * acc + _bdot(vb, p.astype(BF16))
        m = m_new
    lv = lamv_ref[...]
    lam = (jnp.exp(jnp.sum(lv[0:1] * lv[1:2], axis=1, keepdims=True))
           - jnp.exp(jnp.sum(lv[2:3] * lv[3:4], axis=1, keepdims=True)) + lam_init)
    on = acc / l
    ot = on[:, 0:tq] - lam * on[:, tq:2 * tq]
    ot = ot * lax.rsqrt(jnp.mean(ot * ot, axis=0, keepdims=True) + EPS) * gain_ref[...] * (1.0 - lam_init)
    o_ref[...] = ot.T.astype(BF16)


def _attention(qs, kr, vt, lam_vec, gain, lam_init, *, n_seq, length, row0, cache_k=None, cache_vt=None):
    w = kr.shape[1]
    heads = w // DA_V_DIM
    tq = TOKEN_TILE
    tk = min(length, ATTN_KEY_BLOCK)
    assert length % tq == 0 and length % tk == 0 and row0 % length == 0
    nq = length // tq
    has_cache = cache_k is not None
    q_blk0, seq0 = row0 // tq, row0 // length

    in_specs = [pl.BlockSpec((2, tq, DA_V_DIM), lambda b, h, i: (0, q_blk0 + b * nq + i, h))]
    args = [qs]
    if has_cache:
        past = cache_k.shape[1]
        assert past % tk == 0
        in_specs += [pl.BlockSpec((1, past, DA_V_DIM), lambda b, h, i: (b, 0, h)),
                     pl.BlockSpec((1, DA_V_DIM, past), lambda b, h, i: (b, h, 0))]
        args += [cache_k, cache_vt]
    in_specs += [pl.BlockSpec((length, DA_V_DIM), lambda b, h, i: (seq0 + b, h)),
                 pl.BlockSpec((DA_V_DIM, length), lambda b, h, i: (h, seq0 + b)),
                 pl.BlockSpec(lam_vec.shape, lambda b, h, i: (0, 0)),
                 pl.BlockSpec((DA_V_DIM, 1), lambda b, h, i: (0, 0))]
    args += [kr, vt, lam_vec, gain.reshape(DA_V_DIM, 1)]
    return pl.pallas_call(
        functools.partial(_attn_kernel, tq=tq, tk=tk, has_cache=has_cache, lam_init=lam_init),
        grid=(n_seq, heads, nq),
        in_specs=in_specs,
        out_specs=pl.BlockSpec((tq, DA_V_DIM), lambda b, h, i: (b * nq + i, h)),
        out_shape=jax.ShapeDtypeStruct((n_seq * length, w), BF16),
        compiler_params=_params("parallel", "parallel", "arbitrary"),
        name="attention_lat" if has_cache else "attention_ctx",
    )(*args)


def _outproj_kernel(yf_ref, yb_ref, xs_ref, z_ref, oc_ref, ol_ref, x_ref, mod_ref, dsk_ref, gain_ref,
                    wy_ref, wo_ref, out_ref, *, ctx_tiles):
    is_ctx = pl.program_id(0) < ctx_tiles
    y = yf_ref[...] + yb_ref[...] + (dsk_ref[0:1, :] + dsk_ref[1:2, :]) * xs_ref[...]
    y = _rms(y * _silu(z_ref[...]), gain_ref[...]).astype(BF16)
    o = jnp.where(is_ctx, oc_ref[...], ol_ref[...])
    mixed = _bdot(y, wy_ref[...]) + _bdot(o, wo_ref[...])
    out_ref[...] = x_ref[...] + mod_ref[0][2:3] * mixed


def _outproj(rows, yf, yb, xs, z, o_ctx, o_lat, x, mod_l, d_skip, gain, wy, wo):
    t, d = x.shape
    d_ssm = xs.shape[1]
    wa = o_ctx.shape[1]
    return pl.pallas_call(
        functools.partial(_outproj_kernel, ctx_tiles=rows.ctx_tiles),
        grid=(rows.tiles,),
        in_specs=[_row_spec(d_ssm)] * 4 + [
            pl.BlockSpec((TOKEN_TILE, wa), lambda i: (jnp.minimum(i, rows.ctx_tiles - 1), 0)),
            pl.BlockSpec((TOKEN_TILE, wa), lambda i: (jnp.maximum(i - rows.ctx_tiles, 0), 0)),
            _row_spec(d),
            pl.BlockSpec((1, N_MOD, d), lambda i: (rows.mod_row(i), 0, 0)),
            _const_spec(d_skip.shape), _const_spec((1, d_ssm)), _const_spec(wy.shape), _const_spec(wo.shape)],
        out_specs=_row_spec(d),
        out_shape=jax.ShapeDtypeStruct((t, d), F32),
        compiler_params=_params("parallel"),
        name="outproj",
    )(yf, yb, xs, z, o_ctx, o_lat, x, mod_l, d_skip, gain.reshape(1, d_ssm), wy, wo)


def _pool_kernel(cur_ref, prev_ref, next_ref, mod_ref, g_ref, w_ref, b_ref, sc_ref, out_ref, ext_ref,
                 *, rows, len_ctx, len_lat):
    i = pl.program_id(0)
    pos, n = rows.seq_pos(i)
    tile = cur_ref.shape[0]
    m = mod_ref[0]
    g = g_ref[...]

    def norm(x):
        return _modnorm(x, g, m[1:2], m[0:1])

    x = cur_ref[...]
    ext_ref[0:CONV_HALO, :] = jnp.where(pos == 0, 0.0, norm(prev_ref[...]))
    ext_ref[CONV_HALO:CONV_HALO + tile, :] = norm(x)
    ext_ref[CONV_HALO + tile:, :] = jnp.where(pos == n - 1, 0.0, norm(next_ref[...]))
    length = jnp.where(i < rows.ctx_tiles, len_ctx, len_lat)
    tpos = pos * tile + lax.broadcasted_iota(I32, (tile, 1), 0)
    pg = cur_ref.shape[1] // len(POOL_WINDOWS)
    for gi, win in enumerate(POOL_WINDOWS):
        cs = slice(gi * pg, (gi + 1) * pg)
        half = win // 2
        acc = ext_ref[CONV_HALO - half:CONV_HALO - half + tile, cs]
        for off in range(1 - half, half):
            acc = acc + ext_ref[CONV_HALO + off:CONV_HALO + off + tile, cs]
        cnt = jnp.minimum(tpos + half, length) - jnp.maximum(tpos - half, 0)
        pooled = acc / cnt.astype(F32) - ext_ref[CONV_HALO:CONV_HALO + tile, cs]
        mixed = (_bdot(pooled.astype(BF16), w_ref[gi]) + b_ref[gi:gi + 1, :]) * sc_ref[:, cs]
        out_ref[:, cs] = x[:, cs] + m[2:3, cs] * mixed


def _pool(rows, x, mod_l, g, pool_w, pool_b, pool_scale):
    t, d = x.shape
    hb = TOKEN_TILE // CONV_HALO
    last = t // CONV_HALO - 1
    return pl.pallas_call(
        functools.partial(_pool_kernel, rows=rows, len_ctx=rows.len_ctx, len_lat=rows.len_lat),
        grid=(rows.tiles,),
        in_specs=[_row_spec(d),
                  pl.BlockSpec((CONV_HALO, d), lambda i: (jnp.maximum(i * hb - 1, 0), 0)),
                  pl.BlockSpec((CONV_HALO, d), lambda i: (jnp.minimum((i + 1) * hb, last), 0)),
                  pl.BlockSpec((1, N_MOD, d), lambda i: (rows.mod_row(i), 0, 0)),
                  _const_spec((1, d)), _const_spec(pool_w.shape), _const_spec(pool_b.shape),
                  _const_spec((1, d))],
        out_specs=_row_spec(d),
        out_shape=jax.ShapeDtypeStruct((t, d), F32),
        scratch_shapes=[pltpu.VMEM((TOKEN_TILE + 2 * CONV_HALO, d), F32)],
        compiler_params=_params("parallel"),
        name="pool",
    )(x, x, x, mod_l, g.reshape(1, d), pool_w, pool_b, pool_scale.reshape(1, d))


def _router_kernel(x_ref, mod_ref, g_ref, whi_ref, wlo_ref, bias_ref, ri_o, rf_o, cnt_o, cnt_scr,
                   *, n_experts):
    i = pl.program_id(0)

    @pl.when(i == 0)
    def _():
        cnt_scr[...] = jnp.zeros(cnt_scr.shape, F32)

    m = mod_ref[0]
    h = _modnorm(x_ref[...], g_ref[...], m[4:5], m[3:4])
    tm = h.shape[0]
    h_hi = h.astype(BF16)
    h_lo = (h - h_hi.astype(F32)).astype(BF16)
    lg = (lax.dot_general(whi_ref[...], h_hi, NT_DIMS, preferred_element_type=F32)
          + lax.dot_general(wlo_ref[...], h_hi, NT_DIMS, preferred_element_type=F32)
          + lax.dot_general(whi_ref[...], h_lo, NT_DIMS, preferred_element_type=F32)) + bias_ref[...]
    row = lax.broadcasted_iota(I32, (n_experts, tm), 0)
    big = jnp.int32(n_experts + MOE_GROUPS)
    lc = lg[n_experts:n_experts + 8, :]
    crow = lax.broadcasted_iota(I32, (8, tm), 0)
    lc = jnp.where(crow < MOE_GROUPS, lc, -jnp.inf)
    mc = jnp.max(lc, axis=0, keepdims=True)
    group_p = 1.0 / jnp.sum(jnp.exp(lc - mc), axis=0, keepdims=True)
    g_idx = jnp.min(jnp.where(lc == mc, crow, big), axis=0, keepdims=True)
    lf = jnp.where(row // EXPERTS_PER_GROUP == g_idx, lg[0:n_experts, :], -jnp.inf)
    m1 = jnp.max(lf, axis=0, keepdims=True)
    e1 = jnp.min(jnp.where(lf == m1, row, big), axis=0, keepdims=True)
    lf2 = jnp.where(row == e1, -jnp.inf, lf)
    m2 = jnp.max(lf2, axis=0, keepdims=True)
    e2 = jnp.min(jnp.where(lf2 == m2, row, big), axis=0, keepdims=True)
    r2 = jnp.exp(m2 - m1)
    gate1 = group_p / (1.0 + r2)
    gate2 = group_p * r2 / (1.0 + r2)
    oh1 = (row == e1)
    oh2 = (row == e2)
    oh = jnp.where(oh1 | oh2, 1.0, 0.0)
    before = (lax.broadcasted_iota(I32, (tm, tm), 0) < lax.broadcasted_iota(I32, (tm, tm), 1))
    prior = _bdot(oh.astype(BF16), before.astype(BF16)) + cnt_scr[...]
    rank1 = jnp.sum(jnp.where(oh1, prior, 0.0), axis=0, keepdims=True)
    rank2 = jnp.sum(jnp.where(oh2, prior, 0.0), axis=0, keepdims=True)
    cnt_scr[...] = cnt_scr[...] + jnp.sum(oh, axis=1, keepdims=True)
    zi = jnp.zeros((4, tm), I32)
    ri_o[0] = jnp.concatenate([e1, e2, rank1.astype(I32), rank2.astype(I32), zi], axis=0)
    rf_o[0] = jnp.concatenate([gate1, gate2, jnp.zeros((6, tm), F32)], axis=0)
    cnt_o[...] = jnp.broadcast_to(cnt_scr[...], cnt_o.shape)


def _router(rows, x, mod_l, g, w_hi, w_lo, bias, n_experts):
    t, d = x.shape
    tm = TOKEN_TILE
    nt = t // tm
    return pl.pallas_call(
        functools.partial(_router_kernel, n_experts=n_experts),
        grid=(nt,),
        in_specs=[_row_spec(d), pl.BlockSpec((1, N_MOD, d), lambda i: (rows.mod_row(i), 0, 0)),
                  _const_spec((1, d)), _const_spec(w_hi.shape), _const_spec(w_lo.shape),
                  _const_spec(bias.shape)],
        out_specs=[pl.BlockSpec((1, 8, tm), lambda i: (i, 0, 0)),
                   pl.BlockSpec((1, 8, tm), lambda i: (i, 0, 0)),
                   _const_spec((n_experts, ROUTER_LANES))],
        out_shape=[jax.ShapeDtypeStruct((nt, 8, tm), I32),
                   jax.ShapeDtypeStruct((nt, 8, tm), F32),
                   jax.ShapeDtypeStruct((n_experts, ROUTER_LANES), F32)],
        scratch_shapes=[pltpu.VMEM((n_experts, 1), F32)],
        compiler_params=_params("arbitrary"),
        name="router",
    )(x, mod_l, g.reshape(1, d), w_hi, w_lo, bias)


def _dispatch_kernel(dest_ref, x_ref, mod_ref, g_ref, xs_in_ref, xs_ref, h_scr, sem):
    del xs_in_ref
    tm = x_ref.shape[0]
    m = mod_ref[0]
    h_scr[...] = _modnorm(x_ref[...], g_ref[...], m[4:5], m[3:4])

    def copy(t, slot):
        return pltpu.make_async_copy(h_scr.at[pl.ds(t, 1)], xs_ref.at[pl.ds(slot, 1)], sem)

    def start(t, carry):
        for k in range(MOE_TOP_K):
            copy(t, dest_ref[0, k, t]).start()
        return carry

    def wait(t, carry):
        for k in range(MOE_TOP_K):
            copy(0, 0).wait()
        return carry

    lax.fori_loop(0, tm, start, 0)
    lax.fori_loop(0, tm, wait, 0)


def _dispatch(rows, x, mod_l, g, dest, n_slots):
    t, d = x.shape
    nt, _, tm = dest.shape
    xs0 = jnp.zeros((n_slots, d), F32)
    return pl.pallas_call(
        _dispatch_kernel,
        grid=(nt,),
        in_specs=[pl.BlockSpec((1, MOE_TOP_K, tm), lambda i: (i, 0, 0), memory_space=pltpu.SMEM),
                  _row_spec(d), pl.BlockSpec((1, N_MOD, d), lambda i: (rows.mod_row(i), 0, 0)),
                  _const_spec((1, d)), pl.BlockSpec(memory_space=pl.ANY)],
        out_specs=pl.BlockSpec(memory_space=pl.ANY),
        out_shape=jax.ShapeDtypeStruct((n_slots, d), F32),
        scratch_shapes=[pltpu.VMEM((tm, d), F32), pltpu.SemaphoreType.DMA(())],
        input_output_aliases={4: 0},
        compiler_params=_params("arbitrary"),
        name="moe_dispatch",
    )(dest, x, mod_l, g.reshape(1, d), xs0)


def _expert_kernel(be_ref, nu_ref, x_ref, wg_ref, wu_ref, wd_ref, y_ref):
    j = pl.program_id(0)

    @pl.when(j < nu_ref[0])
    def _():
        x = x_ref[...].astype(BF16)
        a = _bdot(x, wg_ref[0, 0].astype(BF16))
        u = _bdot(x, wu_ref[0, 0].astype(BF16))
        y_ref[...] = _bdot((_silu(a) * u).astype(BF16), wd_ref[0, 0].astype(BF16))

    @pl.when(j >= nu_ref[0])
    def _():
        y_ref[...] = jnp.zeros(y_ref.shape, F32)


def _experts(xs, block_expert, n_used, w_gate, w_up, w_down, layer):
    p, d = xs.shape
    hid = w_gate.shape[3]
    nb = p // EXPERT_ROWS
    return pl.pallas_call(
        _expert_kernel,
        grid_spec=pltpu.PrefetchScalarGridSpec(
            num_scalar_prefetch=2,
            grid=(nb,),
            in_specs=[pl.BlockSpec((EXPERT_ROWS, d), lambda j, be, nu: (j, 0)),
                      pl.BlockSpec((1, 1, d, hid), lambda j, be, nu: (layer, be[j], 0, 0)),
                      pl.BlockSpec((1, 1, d, hid), lambda j, be, nu: (layer, be[j], 0, 0)),
                      pl.BlockSpec((1, 1, hid, d), lambda j, be, nu: (layer, be[j], 0, 0))],
            out_specs=pl.BlockSpec((EXPERT_ROWS, d), lambda j, be, nu: (j, 0))),
        out_shape=jax.ShapeDtypeStruct((p, d), F32),
        compiler_params=_params("arbitrary"),
        name="moe_experts",
    )(block_expert, n_used, xs, w_gate, w_up, w_down)


def _combine_kernel(dest_ref, x_ref, mod_ref, gates_ref, ys_ref, out_ref, buf1, buf2, sem):
    tm = x_ref.shape[0]
    bufs = (buf1, buf2)

    def copy(slot, t, k):
        return pltpu.make_async_copy(ys_ref.at[pl.ds(slot, 1)], bufs[k].at[pl.ds(t, 1)], sem)

    def start(t, carry):
        for k in range(MOE_TOP_K):
            copy(dest_ref[0, k, t], t, k).start()
        return carry

    def wait(t, carry):
        for k in range(MOE_TOP_K):
            copy(0, 0, k).wait()
        return carry

    lax.fori_loop(0, tm, start, 0)
    eye = (lax.broadcasted_iota(I32, (tm, tm), 0) == lax.broadcasted_iota(I32, (tm, tm), 1))
    gates = gates_ref[0]
    g1 = jnp.sum(jnp.where(eye, gates[0:1, :], 0.0), axis=1, keepdims=True)
    g2 = jnp.sum(jnp.where(eye, gates[1:2, :], 0.0), axis=1, keepdims=True)
    lax.fori_loop(0, tm, wait, 0)
    ffn = g1 * buf1[...] + g2 * buf2[...]
    out_ref[...] = x_ref[...] + mod_ref[0][5:6] * ffn


def _combine(rows, x, mod_l, dest, gates, ys):
    t, d = x.shape
    nt, _, tm = dest.shape
    return pl.pallas_call(
        _combine_kernel,
        grid=(nt,),
        in_specs=[pl.BlockSpec((1, MOE_TOP_K, tm), lambda i: (i, 0, 0), memory_space=pltpu.SMEM),
                  _row_spec(d), pl.BlockSpec((1, N_MOD, d), lambda i: (rows.mod_row(i), 0, 0)),
                  pl.BlockSpec((1, 8, tm), lambda i: (i, 0, 0)),
                  pl.BlockSpec(memory_space=pl.ANY)],
        out_specs=_row_spec(d),
        out_shape=jax.ShapeDtypeStruct((t, d), F32),
        scratch_shapes=[pltpu.VMEM((tm, d), F32), pltpu.VMEM((tm, d), F32), pltpu.SemaphoreType.DMA(())],
        compiler_params=_params("arbitrary"),
        name="moe_combine",
    )(dest, x, mod_l, gates, ys)


def _moe(rows, x, mod_l, g, w_coarse, b_coarse, w_fine, b_fine, w_gate, w_up, w_down, layer):
    t, d = x.shape
    n_experts = w_fine.shape[1]
    w_t = jnp.zeros((ROUTER_LANES, d), F32).at[:n_experts].set(w_fine.T)
    w_t = w_t.at[n_experts:n_experts + MOE_GROUPS].set(w_coarse.T)
    w_hi = w_t.astype(BF16)
    w_lo = (w_t - w_hi.astype(F32)).astype(BF16)
    bias = jnp.zeros((ROUTER_LANES, 1), F32).at[:n_experts, 0].set(b_fine)
    bias = bias.at[n_experts:n_experts + MOE_GROUPS, 0].set(b_coarse)
    ri, gates, counts = _router(rows, x, mod_l, g, w_hi, w_lo, bias, n_experts)
    counts = counts[:, 0].astype(I32)
    padded = ((counts + EXPERT_ROWS - 1) // EXPERT_ROWS) * EXPERT_ROWS
    pad_end = jnp.cumsum(padded)
    pad_start = pad_end - padded
    nb = -(-(t * MOE_TOP_K) // EXPERT_ROWS) + n_experts
    block_row0 = jnp.arange(nb, dtype=I32) * EXPERT_ROWS
    block_expert = jnp.minimum(jnp.sum((pad_end[None, :] <= block_row0[:, None]).astype(I32), axis=1),
                               n_experts - 1)
    n_used = (pad_end[-1:] // EXPERT_ROWS).astype(I32)
    chosen = ri[:, 0:MOE_TOP_K, :, None] == jnp.arange(n_experts, dtype=I32)
    dest = jnp.sum(jnp.where(chosen, pad_start, 0), axis=-1) + ri[:, MOE_TOP_K:2 * MOE_TOP_K, :]
    xs = _dispatch(rows, x, mod_l, g, dest, nb * EXPERT_ROWS)
    ys = _experts(xs, block_expert, n_used, w_gate, w_up, w_down, layer)
    return _combine(rows, x, mod_l, dest, gates, ys)


def _final_kernel(x_ref, g_ref, o_ref):
    o_ref[...] = _rms(x_ref[...], g_ref[...])


def _final_norm(x, g, row0, n_rows):
    d = x.shape[1]
    blk0 = row0 // TOKEN_TILE
    return pl.pallas_call(
        _final_kernel,
        grid=(n_rows // TOKEN_TILE,),
        in_specs=[pl.BlockSpec((TOKEN_TILE, d), lambda i: (blk0 + i, 0)), _const_spec((1, d))],
        out_specs=_row_spec(d),
        out_shape=jax.ShapeDtypeStruct((n_rows, d), F32),
        compiler_params=_params("parallel"),
        name="final_norm",
    )(x, g.reshape(1, d))


def kernel(x_prompt, x_sample, cache_k, cache_v, state_ssm_fwd, state_ssm_bwd, c, c_ctx, w_mod, b_mod, norm_mix, norm_ffn, norm_final, w_in, conv_w, conv_b, dt_bias, a_log, d_skip, ssm_gain, lam_vec, subln_gain, w_out, pool_w, pool_b, pool_scale, w_coarse, b_coarse, w_fine, b_fine, w_gate, w_up, w_down):
    n_ctx, len_ctx, d = x_prompt.shape
    n_lat, len_lat, _ = x_sample.shape
    rows = _Rows(n_ctx, len_ctx, n_lat, len_lat)
    depth = w_mod.shape[0]
    d_ssm = ssm_gain.shape[1]
    heads = a_log.shape[2]
    da_heads = cache_k.shape[3]
    qk_cols = da_heads * 2 * DA_QK_DIM
    v_cols = da_heads * DA_V_DIM
    xbc_cols = conv_w.shape[2]
    past = cache_k.shape[2]

    x = jnp.concatenate([x_prompt.reshape(rows.t_ctx, d), x_sample.reshape(rows.t_lat, d)], axis=0)
    n_vec = -(-(1 + n_lat) // 8) * 8
    cvecs = jnp.zeros((n_vec, d), F32).at[0].set(c_ctx).at[1:1 + n_lat].set(c)
    mod = _modulation(cvecs, w_mod, b_mod).reshape(depth, n_vec, N_MOD, d)

    ks, vs, sf, sb = [], [], [], []
    for l in range(depth):
        j = l // 2
        if l % 2 == 0:
            lam_init = 0.8 - 0.6 * math.exp(-0.3 * l)
            splits = [d_ssm, d_ssm + xbc_cols, d_ssm + xbc_cols + 2 * heads]
            splits += [splits[-1] + qk_cols, splits[-1] + 2 * qk_cols]
            wz, wxbc, wdt, wq, wk, wv = [w.astype(BF16) for w in jnp.split(w_in[j], splits, axis=1)]
            z, xbc, dt, dtT, q, k, v = _inproj(rows, x, mod[l], norm_mix[l], (wz, wxbc, wdt, wdt.T, wq, wk, wv))
            xs, bm, cm = _conv(rows, xbc, conv_w[j], conv_b[j], d_ssm)
            hf0 = state_ssm_fwd[:, j].reshape(n_lat, heads * SSM_HEAD_DIM, SSM_STATE)
            hb0 = state_ssm_bwd[:, j].reshape(n_lat, heads * SSM_HEAD_DIM, SSM_STATE)
            yf, yb, hf, hb = _ssd(rows, xs, bm, cm, dt, dtT, dt_bias[j], a_log[j], hf0, hb0)
            qs, kr, vt = _prep(rows, q, k, v)
            o_ctx = _attention(qs, kr, vt, lam_vec[j], subln_gain[j], lam_init,
                               n_seq=n_ctx, length=len_ctx, row0=0)
            ck = cache_k[:, j].reshape(n_lat, past, qk_cols).astype(BF16)
            cvt = jnp.swapaxes(cache_v[:, j].reshape(n_lat, past, v_cols), 1, 2).astype(BF16)
            o_lat = _attention(qs, kr, vt, lam_vec[j], subln_gain[j], lam_init,
                               n_seq=n_lat, length=len_lat, row0=rows.t_ctx, cache_k=ck, cache_vt=cvt)
            dsk = jnp.repeat(d_skip[j], SSM_HEAD_DIM, axis=1)
            wo = w_out[j].astype(BF16)
            x = _outproj(rows, yf, yb, xs, z, o_ctx, o_lat, x, mod[l], dsk, ssm_gain[j],
                         wo[:d_ssm], wo[d_ssm:])
            ks.append(k[:rows.t_ctx].reshape(n_ctx, len_ctx, da_heads, 2, DA_QK_DIM))
            vs.append(v[:rows.t_ctx].reshape(n_ctx, len_ctx, da_heads, DA_V_DIM))
            sf.append(hf[:n_ctx].reshape(n_ctx, heads, SSM_HEAD_DIM, SSM_STATE))
            sb.append(hb[:n_ctx].reshape(n_ctx, heads, SSM_HEAD_DIM, SSM_STATE))
        else:
            x = _pool(rows, x, mod[l], norm_mix[l], pool_w[j].astype(BF16), pool_b[j], pool_scale[j])
        x = _moe(rows, x, mod[l], norm_ffn[l], w_coarse[l], b_coarse[l], w_fine[l], b_fine[l],
                 w_gate, w_up, w_down, l)

    y_prompt = _final_norm(x, norm_final, 0, rows.t_ctx).reshape(n_ctx, len_ctx, d)
    y_sample = _final_norm(x, norm_final, rows.t_ctx, rows.t_lat).reshape(n_lat, len_lat, d)
    return (y_prompt, y_sample, jnp.stack(ks, axis=1), jnp.stack(vs, axis=1),
            jnp.stack(sf, axis=1), jnp.stack(sb, axis=1))
```

```python
import functools
import math

import jax
import jax.numpy as jnp
from jax import lax
from jax.experimental import pallas as pl
from jax.experimental.pallas import tpu as pltpu

F32 = jnp.float32
BF16 = jnp.bfloat16
I32 = jnp.int32
HIGHEST = lax.Precision.HIGHEST

EPS = 1e-6
N_MOD = 6
SSM_HEAD_DIM = 64
SSM_GROUPS = 4
SSM_STATE = 128
CHUNK = 128
CONV_WIDTH = 5
CONV_HALO = 8
DA_QK_DIM = 64
DA_V_DIM = 128
GRID_W = 64
ROPE_FREQS = 16
ROPE_THETA = 10000.0
POOL_WINDOWS = (2, 4, 8, 16)
MOE_GROUPS = 4
EXPERTS_PER_GROUP = 8
MOE_TOP_K = 2

TOKEN_TILE = 256
EXPERT_ROWS = 256
ATTN_KEY_BLOCK = 512
DMA_ISSUE_UNROLL = 8
ROUTER_LANES = 128
VMEM_LIMIT_BYTES = 56 * 1024 * 1024

NT_DIMS = (((1,), (1,)), ((), ()))
TN_DIMS = (((0,), (0,)), ((), ()))


def _params(*sem):
    return pltpu.CompilerParams(dimension_semantics=sem, vmem_limit_bytes=VMEM_LIMIT_BYTES)


def _silu(x):
    return x * jax.nn.sigmoid(x)


def _softplus(x):
    return jnp.maximum(x, 0.0) + jnp.log1p(jnp.exp(-jnp.abs(x)))


def _rms(x, g):
    return x * lax.rsqrt(jnp.mean(x * x, axis=-1, keepdims=True) + EPS) * g


def _modnorm(x, g, scale, shift):
    return _rms(x, g) * (1.0 + scale) + shift


def _bdot(a, b):
    return jnp.dot(a, b, preferred_element_type=F32)


class _Rows:
    def __init__(self, n_ctx, len_ctx, n_lat, len_lat):
        self.n_ctx, self.len_ctx, self.n_lat, self.len_lat = n_ctx, len_ctx, n_lat, len_lat
        self.t_ctx = n_ctx * len_ctx
        self.t_lat = n_lat * len_lat
        self.t = self.t_ctx + self.t_lat
        assert len_ctx % TOKEN_TILE == 0 and len_lat % TOKEN_TILE == 0
        self.ctx_tiles = self.t_ctx // TOKEN_TILE
        self.tiles = self.t // TOKEN_TILE
        self.tiles_per_ctx = len_ctx // TOKEN_TILE
        self.tiles_per_lat = len_lat // TOKEN_TILE

    def mod_row(self, i):
        return jnp.where(i < self.ctx_tiles, 0, 1 + (i - self.ctx_tiles) // self.tiles_per_lat)

    def seq_pos(self, i):
        is_ctx = i < self.ctx_tiles
        pos = jnp.where(is_ctx, i % self.tiles_per_ctx, (i - self.ctx_tiles) % self.tiles_per_lat)
        n = jnp.where(is_ctx, self.tiles_per_ctx, self.tiles_per_lat)
        return pos, n


def _row_spec(width, tile=TOKEN_TILE):
    return pl.BlockSpec((tile, width), lambda i: (i, 0))


def _const_spec(shape):
    nd = len(shape)
    return pl.BlockSpec(shape, lambda i: (0,) * nd)


def _mod_kernel(c_ref, w_ref, b_ref, o_ref):
    s = _silu(c_ref[...])
    o_ref[0] = jnp.dot(s, w_ref[0], precision=HIGHEST, preferred_element_type=F32) + b_ref[0]


def _modulation(cvecs, w_mod, b_mod):
    depth, d, n = w_mod.shape
    tn = n // 4
    return pl.pallas_call(
        _mod_kernel,
        grid=(depth, n // tn),
        in_specs=[pl.BlockSpec(cvecs.shape, lambda l, j: (0, 0)),
                  pl.BlockSpec((1, d, tn), lambda l, j: (l, 0, j)),
                  pl.BlockSpec((1, 1, tn), lambda l, j: (l, 0, j))],
        out_specs=pl.BlockSpec((1, cvecs.shape[0], tn), lambda l, j: (l, 0, j)),
        out_shape=jax.ShapeDtypeStruct((depth, cvecs.shape[0], n), F32),
        compiler_params=_params("parallel", "parallel"),
        name="modulation",
    )(cvecs, w_mod, b_mod.reshape(depth, 1, n))


def _inproj_kernel(x_ref, mod_ref, g_ref, wzT, wxbc, wdt, wdtT, wq, wk, wv,
                   zT_o, xbc_o, dt_o, dtT_o, q_o, k_o, v_o):
    m = mod_ref[0]
    h = _modnorm(x_ref[...], g_ref[...], m[1:2], m[0:1]).astype(BF16)
    zT_o[...] = lax.dot_general(wzT[...], h, NT_DIMS, preferred_element_type=F32)
    xbc_o[...] = _bdot(h, wxbc[...])
    dt_o[...] = _bdot(h, wdt[...])
    dtT_o[...] = lax.dot_general(wdtT[...], h, NT_DIMS, preferred_element_type=F32)
    q_o[...] = _bdot(h, wq[...])
    k_o[...] = _bdot(h, wk[...])
    v_o[...] = _bdot(h, wv[...])


def _inproj(rows, x, mod_l, g, weights, transposed):
    t, d = x.shape
    out_shape, out_specs = [], []
    for w, is_t in zip(weights, transposed):
        if is_t:
            out_shape.append(jax.ShapeDtypeStruct((w.shape[0], t), F32))
            out_specs.append(pl.BlockSpec((w.shape[0], TOKEN_TILE), lambda i: (0, i)))
        else:
            out_shape.append(jax.ShapeDtypeStruct((t, w.shape[1]), F32))
            out_specs.append(_row_spec(w.shape[1]))
    return pl.pallas_call(
        _inproj_kernel,
        grid=(rows.tiles,),
        in_specs=[_row_spec(d),
                  pl.BlockSpec((1, N_MOD, d), lambda i: (rows.mod_row(i), 0, 0)),
                  _const_spec((1, d))] + [_const_spec(w.shape) for w in weights],
        out_specs=out_specs,
        out_shape=out_shape,
        compiler_params=_params("parallel"),
        name="inproj",
    )(x, mod_l, g.reshape(1, d), *weights)


def _conv_kernel(cur_ref, prev_ref, next_ref, w_ref, b_ref, xsT_o, b_o, c_o, ext_ref, *, rows, d_ssm, gn):
    i = pl.program_id(0)
    pos, n = rows.seq_pos(i)
    tile = cur_ref.shape[0]
    ext_ref[0:CONV_HALO, :] = jnp.where(pos == 0, 0.0, prev_ref[...])
    ext_ref[CONV_HALO:CONV_HALO + tile, :] = cur_ref[...]
    ext_ref[CONV_HALO + tile:, :] = jnp.where(pos == n - 1, 0.0, next_ref[...])
    width = cur_ref.shape[1]
    step = 512
    for c0 in range(0, width, step):
        acc = jnp.broadcast_to(b_ref[:, c0:c0 + step], (tile, step))
        for k in range(CONV_WIDTH):
            off = CONV_HALO - CONV_WIDTH // 2 + k
            acc = acc + w_ref[k:k + 1, c0:c0 + step] * ext_ref[off:off + tile, c0:c0 + step]
        y = _silu(acc)
        if c0 < d_ssm:
            xsT_o[c0:c0 + step, :] = y.T
        elif c0 < d_ssm + gn:
            b_o[:, c0 - d_ssm:c0 - d_ssm + step] = y.astype(BF16)
        else:
            c_o[:, c0 - d_ssm - gn:c0 - d_ssm - gn + step] = y.astype(BF16)


def _conv(rows, xbc, conv_w, conv_b, d_ssm):
    t, width = xbc.shape
    gn = (width - d_ssm) // 2
    hb = TOKEN_TILE // CONV_HALO
    last = t // CONV_HALO - 1
    return pl.pallas_call(
        functools.partial(_conv_kernel, rows=rows, d_ssm=d_ssm, gn=gn),
        grid=(rows.tiles,),
        in_specs=[_row_spec(width),
                  pl.BlockSpec((CONV_HALO, width), lambda i: (jnp.maximum(i * hb - 1, 0), 0)),
                  pl.BlockSpec((CONV_HALO, width), lambda i: (jnp.minimum((i + 1) * hb, last), 0)),
                  _const_spec(conv_w.shape), _const_spec((1, width))],
        out_specs=[pl.BlockSpec((d_ssm, TOKEN_TILE), lambda i: (0, i)), _row_spec(gn), _row_spec(gn)],
        out_shape=[jax.ShapeDtypeStruct((d_ssm, t), F32),
                   jax.ShapeDtypeStruct((t, gn), BF16),
                   jax.ShapeDtypeStruct((t, gn), BF16)],
        scratch_shapes=[pltpu.VMEM((TOKEN_TILE + 2 * CONV_HALO, width), F32)],
        compiler_params=_params("parallel"),
        name="conv",
    )(xbc, xbc, xbc, conv_w, conv_b.reshape(1, width))


def _contributes(q, reverse, target_rows):
    r = lax.broadcasted_iota(I32, (q, q), 0)
    c = lax.broadcasted_iota(I32, (q, q), 1)
    tgt, src = (r, c) if target_rows else (c, r)
    return (src >= tgt) if reverse else (src <= tgt)


def _ssd_direction(xT_ref, b_ref, c_ref, dt_c, dt_r, bias_row, bias_col, a_row, a_col,
                   h_refs, yT_ref, reverse):
    q = CHUNK
    heads = a_row.shape[1]
    dt_c = _softplus(dt_c + bias_row)
    dt_r = _softplus(dt_r + bias_col)
    keep_t = _contributes(q, reverse, False)
    acs_c = jnp.dot(_contributes(q, reverse, True).astype(F32), dt_c * a_row,
                    precision=HIGHEST, preferred_element_type=F32)
    acs_r = jnp.dot(dt_r * a_col, keep_t.astype(F32), precision=HIGHEST, preferred_element_type=F32)
    last = 0 if reverse else q - 1
    total_r = acs_r[:, last:last + 1]
    grow_r = jnp.exp(acs_r)
    to_end_r = jnp.exp(total_r - acs_r)
    carry_r = jnp.exp(total_r)
    per_group = heads // SSM_GROUPS
    p = SSM_HEAD_DIM
    for g in range(SSM_GROUPS):
        bg = b_ref[:, g * SSM_STATE:(g + 1) * SSM_STATE]
        cg = c_ref[:, g * SSM_STATE:(g + 1) * SSM_STATE]
        cb_t = lax.dot_general(bg, cg, NT_DIMS, preferred_element_type=F32)
        h_in = h_refs[g][...]
        y_off = lax.dot_general(h_in.astype(BF16), cg, NT_DIMS, preferred_element_type=F32)
        xw, h_decayed = [], []
        for r in range(per_group):
            h = g * per_group + r
            rs = slice(r * p, (r + 1) * p)
            seg = acs_r[h:h + 1, :] - acs_c[:, h:h + 1]
            decay = jnp.exp(jnp.where(keep_t, seg, -jnp.inf))
            l_t = (cb_t * decay).astype(BF16)
            xdt = xT_ref[h * p:(h + 1) * p, :] * dt_r[h:h + 1, :]
            yT_ref[h * p:(h + 1) * p, :] = _bdot(xdt.astype(BF16), l_t) + y_off[rs, :] * grow_r[h:h + 1, :]
            xw.append((xdt * to_end_r[h:h + 1, :]).astype(BF16))
            h_decayed.append(h_in[rs, :] * carry_r[h:h + 1, :])
        state = _bdot(jnp.concatenate(xw, axis=0), bg)
        h_refs[g][...] = jnp.concatenate(h_decayed, axis=0) + state


def _ssd_kernel(xf, bf, cf, dtf, dtTf, xb, bb, cb, dtb, dtTb, bias_ref, biasT_ref, alog_ref, alogT_ref,
                hf0, hb0, yf_o, yb_o, hf_o, hb_o, *scratch, geom):
    s = pl.program_id(0)
    is_ctx, _, c, nc, _ = geom(s)
    heads = alog_ref.shape[1]
    hf_scr, hb_scr = scratch[:SSM_GROUPS], scratch[SSM_GROUPS:]
    rows_g = hf_scr[0].shape[0]

    @pl.when(c == 0)
    def _():
        for g in range(SSM_GROUPS):
            hf_scr[g][...] = jnp.where(is_ctx, 0.0, hf0[0, g * rows_g:(g + 1) * rows_g, :])
            hb_scr[g][...] = jnp.where(is_ctx, 0.0, hb0[0, g * rows_g:(g + 1) * rows_g, :])

    a_row = -jnp.exp(alog_ref[...])
    a_col = -jnp.exp(alogT_ref[...])
    _ssd_direction(xf, bf, cf, dtf[:, 0:heads], dtTf[0:heads, :], bias_ref[0:1, :], biasT_ref[:, 0:1],
                   a_row[0:1, :], a_col[:, 0:1], hf_scr, yf_o, False)
    _ssd_direction(xb, bb, cb, dtb[:, heads:2 * heads], dtTb[heads:2 * heads, :], bias_ref[1:2, :],
                   biasT_ref[:, 1:2], a_row[1:2, :], a_col[:, 1:2], hb_scr, yb_o, True)

    @pl.when(c == nc - 1)
    def _():
        for g in range(SSM_GROUPS):
            hf_o[0, g * rows_g:(g + 1) * rows_g, :] = hf_scr[g][...]
            hb_o[0, g * rows_g:(g + 1) * rows_g, :] = hb_scr[g][...]


def _ssd(rows, xsT, bm, cm, dt, dtT, dt_bias, a_log, hf0, hb0):
    d_ssm, t = xsT.shape
    gn = bm.shape[1]
    heads2 = dt.shape[1]
    ncc, ncl = rows.len_ctx // CHUNK, rows.len_lat // CHUNK
    s_ctx, s_lat = rows.n_ctx * ncc, rows.n_lat * ncl
    n_seq = rows.n_ctx + rows.n_lat

    def geom(s):
        is_ctx = s < s_ctx
        sl = s - s_ctx
        seq = jnp.where(is_ctx, s // ncc, rows.n_ctx + sl // ncl)
        c = jnp.where(is_ctx, s % ncc, sl % ncl)
        nc = jnp.where(is_ctx, ncc, ncl)
        base = jnp.where(is_ctx, (s // ncc) * ncc, s_ctx + (sl // ncl) * ncl)
        return is_ctx, seq, c, nc, base

    def fwd(s):
        _, _, c, _, base = geom(s)
        return base + c

    def bwd(s):
        _, _, c, nc, base = geom(s)
        return base + nc - 1 - c

    def lat_seq(s):
        return jnp.maximum(geom(s)[1] - rows.n_ctx, 0)

    def side(blk):
        return [pl.BlockSpec((d_ssm, CHUNK), lambda s: (0, blk(s))),
                pl.BlockSpec((CHUNK, gn), lambda s: (blk(s), 0)),
                pl.BlockSpec((CHUNK, gn), lambda s: (blk(s), 0)),
                pl.BlockSpec((CHUNK, heads2), lambda s: (blk(s), 0)),
                pl.BlockSpec((heads2, CHUNK), lambda s: (0, blk(s)))]

    state_shape = (SSM_HEAD_DIM * (heads2 // 2), SSM_STATE)
    group_shape = (state_shape[0] // SSM_GROUPS, SSM_STATE)
    state_spec_in = pl.BlockSpec((1,) + state_shape, lambda s: (lat_seq(s), 0, 0))
    state_spec_out = pl.BlockSpec((1,) + state_shape, lambda s: (geom(s)[1], 0, 0))
    return pl.pallas_call(
        functools.partial(_ssd_kernel, geom=geom),
        grid=(s_ctx + s_lat,),
        in_specs=side(fwd) + side(bwd) + [_const_spec(dt_bias.shape), _const_spec(dt_bias.T.shape),
                                          _const_spec(a_log.shape), _const_spec(a_log.T.shape),
                                          state_spec_in, state_spec_in],
        out_specs=[pl.BlockSpec((d_ssm, CHUNK), lambda s: (0, fwd(s))),
                   pl.BlockSpec((d_ssm, CHUNK), lambda s: (0, bwd(s))),
                   state_spec_out, state_spec_out],
        out_shape=[jax.ShapeDtypeStruct((d_ssm, t), F32), jax.ShapeDtypeStruct((d_ssm, t), F32),
                   jax.ShapeDtypeStruct((n_seq,) + state_shape, F32),
                   jax.ShapeDtypeStruct((n_seq,) + state_shape, F32)],
        scratch_shapes=[pltpu.VMEM(group_shape, F32)] * (2 * SSM_GROUPS),
        compiler_params=_params("arbitrary"),
        name="ssd",
    )(xsT, bm, cm, dt, dtT, xsT, bm, cm, dt, dtT, dt_bias, dt_bias.T, a_log, a_log.T, hf0, hb0)


def _prep_kernel(q_ref, k_ref, v_ref, cos_ref, sa_ref, sb_ref, qs_o, kr_o, vt_o, *, ctx_tiles):
    is_lat = pl.program_id(0) >= ctx_tiles
    cos = jnp.where(is_lat, cos_ref[...], 1.0)
    sa = jnp.where(is_lat, sa_ref[...], 0.0)
    sb = jnp.where(is_lat, sb_ref[...], 0.0)
    tile = q_ref.shape[0]
    first_map = lax.broadcasted_iota(I32, (tile, DA_V_DIM), 1) < DA_QK_DIM
    scale = 1.0 / math.sqrt(DA_QK_DIM)

    def rope(x):
        return (x * cos + pltpu.roll(x, DA_V_DIM - ROPE_FREQS, 1) * sa + pltpu.roll(x, ROPE_FREQS, 1) * sb)

    for h in range(q_ref.shape[1] // DA_V_DIM):
        sl = slice(h * DA_V_DIM, (h + 1) * DA_V_DIM)
        qr = rope(q_ref[:, sl]) * scale
        qs_o[0, :, sl] = jnp.where(first_map, qr, 0.0).astype(BF16)
        qs_o[1, :, sl] = jnp.where(first_map, 0.0, qr).astype(BF16)
        kr_o[:, sl] = rope(k_ref[:, sl]).astype(BF16)
        vt_o[sl, :] = v_ref[:, sl].T.astype(BF16)


def _rope_tables(length):
    t = jnp.arange(length)
    pos = jnp.stack([t // GRID_W, t % GRID_W], axis=-1).astype(F32)
    inv = ROPE_THETA ** (-jnp.arange(ROPE_FREQS, dtype=F32) / ROPE_FREQS)
    ang = pos[..., None] * inv
    lane = jnp.arange(DA_V_DIM)
    axis = (lane % DA_QK_DIM) // (2 * ROPE_FREQS)
    freq = lane % ROPE_FREQS
    second_half = (lane % (2 * ROPE_FREQS)) >= ROPE_FREQS
    a = ang[:, axis, freq]
    cos, sin = jnp.cos(a), jnp.sin(a)
    return cos, jnp.where(second_half, 0.0, -sin), jnp.where(second_half, sin, 0.0)


def _prep(rows, q, k, v):
    t, w = q.shape
    tables = _rope_tables(rows.len_lat)

    def tab(i):
        return (jnp.maximum(i - rows.ctx_tiles, 0) % rows.tiles_per_lat, 0)

    return pl.pallas_call(
        functools.partial(_prep_kernel, ctx_tiles=rows.ctx_tiles),
        grid=(rows.tiles,),
        in_specs=[_row_spec(w)] * 3 + [pl.BlockSpec((TOKEN_TILE, DA_V_DIM), tab)] * 3,
        out_specs=[pl.BlockSpec((2, TOKEN_TILE, w), lambda i: (0, i, 0)), _row_spec(w),
                   pl.BlockSpec((w, TOKEN_TILE), lambda i: (0, i))],
        out_shape=[jax.ShapeDtypeStruct((2, t, w), BF16), jax.ShapeDtypeStruct((t, w), BF16),
                   jax.ShapeDtypeStruct((w, t), BF16)],
        compiler_params=_params("parallel"),
        name="attn_prep",
    )(q, k, v, *tables)


def _attn_kernel(*refs, tq, tk, has_cache, lam_init):
    if has_cache:
        qs_ref, kc_ref, vct_ref, k_ref, vt_ref, lamv_ref, gain_ref, o_ref = refs
    else:
        qs_ref, k_ref, vt_ref, lamv_ref, gain_ref, o_ref = refs
    q = qs_ref[...].reshape(2 * tq, DA_V_DIM)
    blocks = []
    if has_cache:
        blocks += [(kc_ref, vct_ref, c, True) for c in range(kc_ref.shape[1] // tk)]
    blocks += [(k_ref, vt_ref, c, False) for c in range(k_ref.shape[0] // tk)]
    m = jnp.full((1, 2 * tq), -jnp.inf, F32)
    l = jnp.zeros((1, 2 * tq), F32)
    acc = jnp.zeros((DA_V_DIM, 2 * tq), F32)
    for kref, vref, c, cached in blocks:
        ks = slice(c * tk, (c + 1) * tk)
        kb = kref[0, ks, :] if cached else kref[ks, :]
        vb = vref[0, :, ks] if cached else vref[:, ks]
        s = lax.dot_general(kb, q, NT_DIMS, preferred_element_type=F32)
        m_new = jnp.maximum(m, jnp.max(s, axis=0, keepdims=True))
        alpha = jnp.exp(m - m_new)
        p = jnp.exp(s - m_new)
        l = alpha * l + jnp.sum(p, axis=0, keepdims=True)
        acc = alpha * acc + _bdot(vb, p.astype(BF16))
        m = m_new
    lv = lamv_ref[...]
    lam = (jnp.exp(jnp.sum(lv[0:1] * lv[1:2], axis=1, keepdims=True))
           - jnp.exp(jnp.sum(lv[2:3] * lv[3:4], axis=1, keepdims=True)) + lam_init)
    on = acc / l
    ot = on[:, 0:tq] - lam * on[:, tq:2 * tq]
    ot = ot * lax.rsqrt(jnp.mean(ot * ot, axis=0, keepdims=True) + EPS) * gain_ref[...] * (1.0 - lam_init)
    o_ref[...] = ot.T.astype(BF16)


def _attention(qs, kr, vt, lam_vec, gain, lam_init, *, n_seq, length, row0, cache_k=None, cache_vt=None):
    w = kr.shape[1]
    heads = w // DA_V_DIM
    tq = TOKEN_TILE
    tk = min(length, ATTN_KEY_BLOCK)
    assert length % tq == 0 and length % tk == 0 and row0 % length == 0
    nq = length // tq
    has_cache = cache_k is not None
    q_blk0, seq0 = row0 // tq, row0 // length

    in_specs = [pl.BlockSpec((2, tq, DA_V_DIM), lambda b, h, i: (0, q_blk0 + b * nq + i, h))]
    args = [qs]
    if has_cache:
        past = cache_k.shape[1]
        assert past % tk == 0
        in_specs += [pl.BlockSpec((1, past, DA_V_DIM), lambda b, h, i: (b, 0, h)),
                     pl.BlockSpec((1, DA_V_DIM, past), lambda b, h, i: (b, h, 0))]
        args += [cache_k, cache_vt]
    in_specs += [pl.BlockSpec((length, DA_V_DIM), lambda b, h, i: (seq0 + b, h)),
                 pl.BlockSpec((DA_V_DIM, length), lambda b, h, i: (h, seq0 + b)),
                 pl.BlockSpec(lam_vec.shape, lambda b, h, i: (0, 0)),
                 pl.BlockSpec((DA_V_DIM, 1), lambda b, h, i: (0, 0))]
    args += [kr, vt, lam_vec, gain.reshape(DA_V_DIM, 1)]
    return pl.pallas_call(
        functools.partial(_attn_kernel, tq=tq, tk=tk, has_cache=has_cache, lam_init=lam_init),
        grid=(n_seq, heads, nq),
        in_specs=in_specs,
        out_specs=pl.BlockSpec((tq, DA_V_DIM), lambda b, h, i: (b * nq + i, h)),
        out_shape=jax.ShapeDtypeStruct((n_seq * length, w), BF16),
        compiler_params=_params("parallel", "parallel", "arbitrary"),
        name="attention_lat" if has_cache else "attention_ctx",
    )(*args)


def _outproj_kernel(yfT_ref, ybT_ref, xsT_ref, zT_ref, oc_ref, ol_ref, x_ref, mod_ref, dsk_ref, gain_ref,
                    wy_ref, wo_ref, out_ref, *, ctx_tiles):
    is_ctx = pl.program_id(0) < ctx_tiles
    y = yfT_ref[...] + ybT_ref[...] + (dsk_ref[0] + dsk_ref[1]) * xsT_ref[...]
    y = y * _silu(zT_ref[...])
    y = (y * lax.rsqrt(jnp.mean(y * y, axis=0, keepdims=True) + EPS) * gain_ref[...]).astype(BF16)
    o = jnp.where(is_ctx, oc_ref[...], ol_ref[...])
    mixed = lax.dot_general(y, wy_ref[...], TN_DIMS, preferred_element_type=F32) + _bdot(o, wo_ref[...])
    out_ref[...] = x_ref[...] + mod_ref[0][2:3] * mixed


def _outproj(rows, yfT, ybT, xsT, zT, o_ctx, o_lat, x, mod_l, d_skip, gain, wy, wo):
    t, d = x.shape
    d_ssm = xsT.shape[0]
    wa = o_ctx.shape[1]
    col_spec = pl.BlockSpec((d_ssm, TOKEN_TILE), lambda i: (0, i))
    return pl.pallas_call(
        functools.partial(_outproj_kernel, ctx_tiles=rows.ctx_tiles),
        grid=(rows.tiles,),
        in_specs=[col_spec] * 4 + [
            pl.BlockSpec((TOKEN_TILE, wa), lambda i: (jnp.minimum(i, rows.ctx_tiles - 1), 0)),
            pl.BlockSpec((TOKEN_TILE, wa), lambda i: (jnp.maximum(i - rows.ctx_tiles, 0), 0)),
            _row_spec(d),
            pl.BlockSpec((1, N_MOD, d), lambda i: (rows.mod_row(i), 0, 0)),
            _const_spec((2, d_ssm, 1)), _const_spec((d_ssm, 1)), _const_spec(wy.shape), _const_spec(wo.shape)],
        out_specs=_row_spec(d),
        out_shape=jax.ShapeDtypeStruct((t, d), F32),
        compiler_params=_params("parallel"),
        name="outproj",
    )(yfT, ybT, xsT, zT, o_ctx, o_lat, x, mod_l, d_skip.reshape(2, d_ssm, 1), gain.reshape(d_ssm, 1), wy, wo)


def _pool_kernel(cur_ref, prev_ref, next_ref, mod_ref, g_ref, w_ref, b_ref, sc_ref, out_ref, ext_ref,
                 *, rows, len_ctx, len_lat):
    i = pl.program_id(0)
    pos, n = rows.seq_pos(i)
    tile = cur_ref.shape[0]
    m = mod_ref[0]
    g = g_ref[...]

    def norm(x):
        return _modnorm(x, g, m[1:2], m[0:1])

    x = cur_ref[...]
    ext_ref[0:CONV_HALO, :] = jnp.where(pos == 0, 0.0, norm(prev_ref[...]))
    ext_ref[CONV_HALO:CONV_HALO + tile, :] = norm(x)
    ext_ref[CONV_HALO + tile:, :] = jnp.where(pos == n - 1, 0.0, norm(next_ref[...]))
    length = jnp.where(i < rows.ctx_tiles, len_ctx, len_lat)
    tpos = pos * tile + lax.broadcasted_iota(I32, (tile, 1), 0)
    pg = cur_ref.shape[1] // len(POOL_WINDOWS)
    for gi, win in enumerate(POOL_WINDOWS):
        cs = slice(gi * pg, (gi + 1) * pg)
        half = win // 2
        acc = ext_ref[CONV_HALO - half:CONV_HALO - half + tile, cs]
        for off in range(1 - half, half):
            acc = acc + ext_ref[CONV_HALO + off:CONV_HALO + off + tile, cs]
        cnt = jnp.minimum(tpos + half, length) - jnp.maximum(tpos - half, 0)
        pooled = acc / cnt.astype(F32) - ext_ref[CONV_HALO:CONV_HALO + tile, cs]
        mixed = (_bdot(pooled.astype(BF16), w_ref[gi]) + b_ref[gi:gi + 1, :]) * sc_ref[:, cs]
        out_ref[:, cs] = x[:, cs] + m[2:3, cs] * mixed


def _pool(rows, x, mod_l, g, pool_w, pool_b, pool_scale):
    t, d = x.shape
    hb = TOKEN_TILE // CONV_HALO
    last = t // CONV_HALO - 1
    return pl.pallas_call(
        functools.partial(_pool_kernel, rows=rows, len_ctx=rows.len_ctx, len_lat=rows.len_lat),
        grid=(rows.tiles,),
        in_specs=[_row_spec(d),
                  pl.BlockSpec((CONV_HALO, d), lambda i: (jnp.maximum(i * hb - 1, 0), 0)),
                  pl.BlockSpec((CONV_HALO, d), lambda i: (jnp.minimum((i + 1) * hb, last), 0)),
                  pl.BlockSpec((1, N_MOD, d), lambda i: (rows.mod_row(i), 0, 0)),
                  _const_spec((1, d)), _const_spec(pool_w.shape), _const_spec(pool_b.shape),
                  _const_spec((1, d))],
        out_specs=_row_spec(d),
        out_shape=jax.ShapeDtypeStruct((t, d), F32),
        scratch_shapes=[pltpu.VMEM((TOKEN_TILE + 2 * CONV_HALO, d), F32)],
        compiler_params=_params("parallel"),
        name="pool",
    )(x, x, x, mod_l, g.reshape(1, d), pool_w, pool_b, pool_scale.reshape(1, d))


def _router_kernel(x_ref, mod_ref, g_ref, whi_ref, wlo_ref, bias_ref, ri_o, rf_o, cnt_o, cnt_scr,
                   *, n_experts):
    i = pl.program_id(0)

    @pl.when(i == 0)
    def _():
        cnt_scr[...] = jnp.zeros(cnt_scr.shape, F32)

    m = mod_ref[0]
    h = _modnorm(x_ref[...], g_ref[...], m[4:5], m[3:4])
    tm = h.shape[0]
    h_hi = h.astype(BF16)
    h_lo = (h - h_hi.astype(F32)).astype(BF16)
    lg = (lax.dot_general(whi_ref[...], h_hi, NT_DIMS, preferred_element_type=F32)
          + lax.dot_general(wlo_ref[...], h_hi, NT_DIMS, preferred_element_type=F32)
          + lax.dot_general(whi_ref[...], h_lo, NT_DIMS, preferred_element_type=F32)) + bias_ref[...]
    row = lax.broadcasted_iota(I32, (n_experts, tm), 0)
    big = jnp.int32(n_experts + MOE_GROUPS)
    lc = lg[n_experts:n_experts + 8, :]
    crow = lax.broadcasted_iota(I32, (8, tm), 0)
    lc = jnp.where(crow < MOE_GROUPS, lc, -jnp.inf)
    mc = jnp.max(lc, axis=0, keepdims=True)
    group_p = 1.0 / jnp.sum(jnp.exp(lc - mc), axis=0, keepdims=True)
    g_idx = jnp.min(jnp.where(lc == mc, crow, big), axis=0, keepdims=True)
    lf = jnp.where(row // EXPERTS_PER_GROUP == g_idx, lg[0:n_experts, :], -jnp.inf)
    m1 = jnp.max(lf, axis=0, keepdims=True)
    e1 = jnp.min(jnp.where(lf == m1, row, big), axis=0, keepdims=True)
    lf2 = jnp.where(row == e1, -jnp.inf, lf)
    m2 = jnp.max(lf2, axis=0, keepdims=True)
    e2 = jnp.min(jnp.where(lf2 == m2, row, big), axis=0, keepdims=True)
    r2 = jnp.exp(m2 - m1)
    gate1 = group_p / (1.0 + r2)
    gate2 = group_p * r2 / (1.0 + r2)
    oh1 = (row == e1)
    oh2 = (row == e2)
    oh = jnp.where(oh1 | oh2, 1.0, 0.0)
    before = (lax.broadcasted_iota(I32, (tm, tm), 0) < lax.broadcasted_iota(I32, (tm, tm), 1))
    prior = _bdot(oh.astype(BF16), before.astype(BF16)) + cnt_scr[...]
    rank1 = jnp.sum(jnp.where(oh1, prior, 0.0), axis=0, keepdims=True)
    rank2 = jnp.sum(jnp.where(oh2, prior, 0.0), axis=0, keepdims=True)
    cnt_scr[...] = cnt_scr[...] + jnp.sum(oh, axis=1, keepdims=True)
    zi = jnp.zeros((4, tm), I32)
    ri_o[0] = jnp.concatenate([e1, e2, rank1.astype(I32), rank2.astype(I32), zi], axis=0)
    rf_o[0] = jnp.concatenate([gate1, gate2, jnp.zeros((6, tm), F32)], axis=0)
    cnt_o[...] = jnp.broadcast_to(cnt_scr[...], cnt_o.shape)


def _router(rows, x, mod_l, g, w_hi, w_lo, bias, n_experts):
    t, d = x.shape
    tm = TOKEN_TILE
    nt = t // tm
    return pl.pallas_call(
        functools.partial(_router_kernel, n_experts=n_experts),
        grid=(nt,),
        in_specs=[_row_spec(d), pl.BlockSpec((1, N_MOD, d), lambda i: (rows.mod_row(i), 0, 0)),
                  _const_spec((1, d)), _const_spec(w_hi.shape), _const_spec(w_lo.shape),
                  _const_spec(bias.shape)],
        out_specs=[pl.BlockSpec((1, 8, tm), lambda i: (i, 0, 0)),
                   pl.BlockSpec((1, 8, tm), lambda i: (i, 0, 0)),
                   _const_spec((n_experts, ROUTER_LANES))],
        out_shape=[jax.ShapeDtypeStruct((nt, 8, tm), I32),
                   jax.ShapeDtypeStruct((nt, 8, tm), F32),
                   jax.ShapeDtypeStruct((n_experts, ROUTER_LANES), F32)],
        scratch_shapes=[pltpu.VMEM((n_experts, 1), F32)],
        compiler_params=_params("arbitrary"),
        name="router",
    )(x, mod_l, g.reshape(1, d), w_hi, w_lo, bias)


def _dispatch_kernel(dest_ref, x_ref, mod_ref, g_ref, xs_in_ref, xs_ref, h_scr, sem):
    del xs_in_ref
    tm = x_ref.shape[0]
    m = mod_ref[0]
    h_scr[...] = _modnorm(x_ref[...], g_ref[...], m[4:5], m[3:4])

    def copy(t, slot):
        return pltpu.make_async_copy(h_scr.at[pl.ds(t, 1)], xs_ref.at[pl.ds(slot, 1)], sem)

    def start(t, carry):
        for k in range(MOE_TOP_K):
            copy(t, dest_ref[0, k, t]).start()
        return carry

    lax.fori_loop(0, tm, start, 0, unroll=DMA_ISSUE_UNROLL)
    for k in range(MOE_TOP_K):
        pltpu.make_async_copy(h_scr, xs_ref.at[pl.ds(0, tm)], sem).wait()


def _dispatch(rows, x, mod_l, g, dest, n_slots):
    t, d = x.shape
    nt, _, tm = dest.shape
    xs0 = jnp.zeros((n_slots, d), F32)
    return pl.pallas_call(
        _dispatch_kernel,
        grid=(nt,),
        in_specs=[pl.BlockSpec((1, MOE_TOP_K, tm), lambda i: (i, 0, 0), memory_space=pltpu.SMEM),
                  _row_spec(d), pl.BlockSpec((1, N_MOD, d), lambda i: (rows.mod_row(i), 0, 0)),
                  _const_spec((1, d)), pl.BlockSpec(memory_space=pl.ANY)],
        out_specs=pl.BlockSpec(memory_space=pl.ANY),
        out_shape=jax.ShapeDtypeStruct((n_slots, d), F32),
        scratch_shapes=[pltpu.VMEM((tm, d), F32), pltpu.SemaphoreType.DMA(())],
        input_output_aliases={4: 0},
        compiler_params=_params("arbitrary"),
        name="moe_dispatch",
    )(dest, x, mod_l, g.reshape(1, d), xs0)


def _expert_kernel(be_ref, nu_ref, x_ref, wg_ref, wu_ref, wd_ref, y_ref):
    j = pl.program_id(0)

    @pl.when(j < nu_ref[0])
    def _():
        x = x_ref[...].astype(BF16)
        a = _bdot(x, wg_ref[0, 0].astype(BF16))
        u = _bdot(x, wu_ref[0, 0].astype(BF16))
        y_ref[...] = _bdot((_silu(a) * u).astype(BF16), wd_ref[0, 0].astype(BF16))

    @pl.when(j >= nu_ref[0])
    def _():
        y_ref[...] = jnp.zeros(y_ref.shape, F32)


def _experts(xs, block_expert, n_used, w_gate, w_up, w_down, layer):
    p, d = xs.shape
    hid = w_gate.shape[3]
    nb = p // EXPERT_ROWS
    return pl.pallas_call(
        _expert_kernel,
        grid_spec=pltpu.PrefetchScalarGridSpec(
            num_scalar_prefetch=2,
            grid=(nb,),
            in_specs=[pl.BlockSpec((EXPERT_ROWS, d), lambda j, be, nu: (j, 0)),
                      pl.BlockSpec((1, 1, d, hid), lambda j, be, nu: (layer, be[j], 0, 0)),
                      pl.BlockSpec((1, 1, d, hid), lambda j, be, nu: (layer, be[j], 0, 0)),
                      pl.BlockSpec((1, 1, hid, d), lambda j, be, nu: (layer, be[j], 0, 0))],
            out_specs=pl.BlockSpec((EXPERT_ROWS, d), lambda j, be, nu: (j, 0))),
        out_shape=jax.ShapeDtypeStruct((p, d), F32),
        compiler_params=_params("arbitrary"),
        name="moe_experts",
    )(block_expert, n_used, xs, w_gate, w_up, w_down)


def _combine_kernel(dest_ref, x_ref, mod_ref, gates_ref, ys_ref, out_ref, buf1, buf2, sem):
    tm = x_ref.shape[0]
    bufs = (buf1, buf2)

    def copy(slot, t, k):
        return pltpu.make_async_copy(ys_ref.at[pl.ds(slot, 1)], bufs[k].at[pl.ds(t, 1)], sem)

    def start(t, carry):
        for k in range(MOE_TOP_K):
            copy(dest_ref[0, k, t], t, k).start()
        return carry

    lax.fori_loop(0, tm, start, 0, unroll=DMA_ISSUE_UNROLL)
    eye = (lax.broadcasted_iota(I32, (tm, tm), 0) == lax.broadcasted_iota(I32, (tm, tm), 1))
    gates = gates_ref[0]
    g1 = jnp.sum(jnp.where(eye, gates[0:1, :], 0.0), axis=1, keepdims=True)
    g2 = jnp.sum(jnp.where(eye, gates[1:2, :], 0.0), axis=1, keepdims=True)
    for k in range(MOE_TOP_K):
        pltpu.make_async_copy(ys_ref.at[pl.ds(0, tm)], bufs[k], sem).wait()
    ffn = g1 * buf1[...] + g2 * buf2[...]
    out_ref[...] = x_ref[...] + mod_ref[0][5:6] * ffn


def _combine(rows, x, mod_l, dest, gates, ys):
    t, d = x.shape
    nt, _, tm = dest.shape
    return pl.pallas_call(
        _combine_kernel,
        grid=(nt,),
        in_specs=[pl.BlockSpec((1, MOE_TOP_K, tm), lambda i: (i, 0, 0), memory_space=pltpu.SMEM),
                  _row_spec(d), pl.BlockSpec((1, N_MOD, d), lambda i: (rows.mod_row(i), 0, 0)),
                  pl.BlockSpec((1, 8, tm), lambda i: (i, 0, 0)),
                  pl.BlockSpec(memory_space=pl.ANY)],
        out_specs=_row_spec(d),
        out_shape=jax.ShapeDtypeStruct((t, d), F32),
        scratch_shapes=[pltpu.VMEM((tm, d), F32), pltpu.VMEM((tm, d), F32), pltpu.SemaphoreType.DMA(())],
        compiler_params=_params("arbitrary"),
        name="moe_combine",
    )(dest, x, mod_l, gates, ys)


def _moe(rows, x, mod_l, g, w_coarse, b_coarse, w_fine, b_fine, w_gate, w_up, w_down, layer):
    t, d = x.shape
    n_experts = w_fine.shape[1]
    w_t = jnp.zeros((ROUTER_LANES, d), F32).at[:n_experts].set(w_fine.T)
    w_t = w_t.at[n_experts:n_experts + MOE_GROUPS].set(w_coarse.T)
    w_hi = w_t.astype(BF16)
    w_lo = (w_t - w_hi.astype(F32)).astype(BF16)
    bias = jnp.zeros((ROUTER_LANES, 1), F32).at[:n_experts, 0].set(b_fine)
    bias = bias.at[n_experts:n_experts + MOE_GROUPS, 0].set(b_coarse)
    ri, gates, counts = _router(rows, x, mod_l, g, w_hi, w_lo, bias, n_experts)
    counts = counts[:, 0].astype(I32)
    padded = ((counts + EXPERT_ROWS - 1) // EXPERT_ROWS) * EXPERT_ROWS
    pad_end = jnp.cumsum(padded)
    pad_start = pad_end - padded
    nb = -(-(t * MOE_TOP_K) // EXPERT_ROWS) + n_experts
    block_row0 = jnp.arange(nb, dtype=I32) * EXPERT_ROWS
    block_expert = jnp.minimum(jnp.sum((pad_end[None, :] <= block_row0[:, None]).astype(I32), axis=1),
                               n_experts - 1)
    n_used = (pad_end[-1:] // EXPERT_ROWS).astype(I32)
    chosen = ri[:, 0:MOE_TOP_K, :, None] == jnp.arange(n_experts, dtype=I32)
    dest = jnp.sum(jnp.where(chosen, pad_start, 0), axis=-1) + ri[:, MOE_TOP_K:2 * MOE_TOP_K, :]
    xs = _dispatch(rows, x, mod_l, g, dest, nb * EXPERT_ROWS)
    ys = _experts(xs, block_expert, n_used, w_gate, w_up, w_down, layer)
    return _combine(rows, x, mod_l, dest, gates, ys)


def _final_kernel(x_ref, g_ref, o_ref):
    o_ref[...] = _rms(x_ref[...], g_ref[...])


def _final_norm(x, g, row0, n_rows):
    d = x.shape[1]
    blk0 = row0 // TOKEN_TILE
    return pl.pallas_call(
        _final_kernel,
        grid=(n_rows // TOKEN_TILE,),
        in_specs=[pl.BlockSpec((TOKEN_TILE, d), lambda i: (blk0 + i, 0)), _const_spec((1, d))],
        out_specs=_row_spec(d),
        out_shape=jax.ShapeDtypeStruct((n_rows, d), F32),
        compiler_params=_params("parallel"),
        name="final_norm",
    )(x, g.reshape(1, d))


def kernel(x_prompt, x_sample, cache_k, cache_v, state_ssm_fwd, state_ssm_bwd, c, c_ctx, w_mod, b_mod, norm_mix, norm_ffn, norm_final, w_in, conv_w, conv_b, dt_bias, a_log, d_skip, ssm_gain, lam_vec, subln_gain, w_out, pool_w, pool_b, pool_scale, w_coarse, b_coarse, w_fine, b_fine, w_gate, w_up, w_down):
    n_ctx, len_ctx, d = x_prompt.shape
    n_lat, len_lat, _ = x_sample.shape
    rows = _Rows(n_ctx, len_ctx, n_lat, len_lat)
    depth = w_mod.shape[0]
    d_ssm = ssm_gain.shape[1]
    heads = a_log.shape[2]
    da_heads = cache_k.shape[3]
    qk_cols = da_heads * 2 * DA_QK_DIM
    v_cols = da_heads * DA_V_DIM
    xbc_cols = conv_w.shape[2]
    past = cache_k.shape[2]

    x = jnp.concatenate([x_prompt.reshape(rows.t_ctx, d), x_sample.reshape(rows.t_lat, d)], axis=0)
    n_vec = -(-(1 + n_lat) // 8) * 8
    cvecs = jnp.zeros((n_vec, d), F32).at[0].set(c_ctx).at[1:1 + n_lat].set(c)
    mod = _modulation(cvecs, w_mod, b_mod).reshape(depth, n_vec, N_MOD, d)

    ks, vs, sf, sb = [], [], [], []
    for l in range(depth):
        j = l // 2
        if l % 2 == 0:
            lam_init = 0.8 - 0.6 * math.exp(-0.3 * l)
            splits = [d_ssm, d_ssm + xbc_cols, d_ssm + xbc_cols + 2 * heads]
            splits += [splits[-1] + qk_cols, splits[-1] + 2 * qk_cols]
            wz, wxbc, wdt, wq, wk, wv = [w.astype(BF16) for w in jnp.split(w_in[j], splits, axis=1)]
            zT, xbc, dt, dtT, q, k, v = _inproj(rows, x, mod[l], norm_mix[l], (wz.T, wxbc, wdt, wdt.T, wq, wk, wv),
                                                 (True, False, False, True, False, False, False))
            xsT, bm, cm = _conv(rows, xbc, conv_w[j], conv_b[j], d_ssm)
            hf0 = state_ssm_fwd[:, j].reshape(n_lat, heads * SSM_HEAD_DIM, SSM_STATE)
            hb0 = state_ssm_bwd[:, j].reshape(n_lat, heads * SSM_HEAD_DIM, SSM_STATE)
            yfT, ybT, hf, hb = _ssd(rows, xsT, bm, cm, dt, dtT, dt_bias[j], a_log[j], hf0, hb0)
            qs, kr, vt = _prep(rows, q, k, v)
            o_ctx = _attention(qs, kr, vt, lam_vec[j], subln_gain[j], lam_init,
                               n_seq=n_ctx, length=len_ctx, row0=0)
            ck = cache_k[:, j].reshape(n_lat, past, qk_cols).astype(BF16)
            cvt = jnp.swapaxes(cache_v[:, j].reshape(n_lat, past, v_cols), 1, 2).astype(BF16)
            o_lat = _attention(qs, kr, vt, lam_vec[j], subln_gain[j], lam_init,
                               n_seq=n_lat, length=len_lat, row0=rows.t_ctx, cache_k=ck, cache_vt=cvt)
            dsk = jnp.repeat(d_skip[j], SSM_HEAD_DIM, axis=1)
            wo = w_out[j].astype(BF16)
            x = _outproj(rows, yfT, ybT, xsT, zT, o_ctx, o_lat, x, mod[l], dsk, ssm_gain[j],
                         wo[:d_ssm], wo[d_ssm:])
            ks.append(k[:rows.t_ctx].reshape(n_ctx, len_ctx, da_heads, 2, DA_QK_DIM))
            vs.append(v[:rows.t_ctx].reshape(n_ctx, len_ctx, da_heads, DA_V_DIM))
            sf.append(hf[:n_ctx].reshape(n_ctx, heads, SSM_HEAD_DIM, SSM_STATE))
            sb.append(hb[:n_ctx].reshape(n_ctx, heads, SSM_HEAD_DIM, SSM_STATE))
        else:
            x = _pool(rows, x, mod[l], norm_mix[l], pool_w[j].astype(BF16), pool_b[j], pool_scale[j])
        x = _moe(rows, x, mod[l], norm_ffn[l], w_coarse[l], b_coarse[l], w_fine[l], b_fine[l],
                 w_gate, w_up, w_down, l)

    y_prompt = _final_norm(x, norm_final, 0, rows.t_ctx).reshape(n_ctx, len_ctx, d)
    y_sample = _final_norm(x, norm_final, rows.t_ctx, rows.t_lat).reshape(n_lat, len_lat, d)
    return (y_prompt, y_sample, jnp.stack(ks, axis=1), jnp.stack(vs, axis=1),
            jnp.stack(sf, axis=1), jnp.stack(sb, axis=1))
```

```python
import functools
import math

import jax
import jax.numpy as jnp
from jax import lax
from jax.experimental import pallas as pl
from jax.experimental.pallas import tpu as pltpu

F32 = jnp.float32
BF16 = jnp.bfloat16
I32 = jnp.int32
HIGHEST = lax.Precision.HIGHEST

EPS = 1e-6
N_MOD = 6
SSM_HEAD_DIM = 64
SSM_GROUPS = 4
SSM_STATE = 128
CHUNK = 128
CONV_WIDTH = 5
CONV_HALO = 8
DA_QK_DIM = 64
DA_V_DIM = 128
GRID_W = 64
ROPE_FREQS = 16
ROPE_THETA = 10000.0
POOL_WINDOWS = (2, 4, 8, 16)
MOE_GROUPS = 4
EXPERTS_PER_GROUP = 8
MOE_TOP_K = 2

TOKEN_TILE = 256
EXPERT_ROWS = 256
ATTN_KEY_BLOCK = 512
ATTN_QUERY_TILE = 1024
V_ROWS = DA_V_DIM + 16
LOG2_E = 1.4426950408889634
DMA_ISSUE_UNROLL = 8
ROUTER_LANES = 128
VMEM_LIMIT_BYTES = 56 * 1024 * 1024

NT_DIMS = (((1,), (1,)), ((), ()))
TN_DIMS = (((0,), (0,)), ((), ()))


def _params(*sem):
    return pltpu.CompilerParams(dimension_semantics=sem, vmem_limit_bytes=VMEM_LIMIT_BYTES)


def _silu(x):
    return x * jax.nn.sigmoid(x)


def _softplus(x):
    return jnp.maximum(x, 0.0) + jnp.log1p(jnp.exp(-jnp.abs(x)))


def _rms(x, g):
    return x * lax.rsqrt(jnp.mean(x * x, axis=-1, keepdims=True) + EPS) * g


def _modnorm(x, g, scale, shift):
    return _rms(x, g) * (1.0 + scale) + shift


def _bdot(a, b):
    return jnp.dot(a, b, preferred_element_type=F32)


class _Rows:
    def __init__(self, n_ctx, len_ctx, n_lat, len_lat):
        self.n_ctx, self.len_ctx, self.n_lat, self.len_lat = n_ctx, len_ctx, n_lat, len_lat
        self.t_ctx = n_ctx * len_ctx
        self.t_lat = n_lat * len_lat
        self.t = self.t_ctx + self.t_lat
        assert len_ctx % TOKEN_TILE == 0 and len_lat % TOKEN_TILE == 0
        self.ctx_tiles = self.t_ctx // TOKEN_TILE
        self.tiles = self.t // TOKEN_TILE
        self.tiles_per_ctx = len_ctx // TOKEN_TILE
        self.tiles_per_lat = len_lat // TOKEN_TILE

    def mod_row(self, i):
        return jnp.where(i < self.ctx_tiles, 0, 1 + (i - self.ctx_tiles) // self.tiles_per_lat)

    def seq_pos(self, i):
        is_ctx = i < self.ctx_tiles
        pos = jnp.where(is_ctx, i % self.tiles_per_ctx, (i - self.ctx_tiles) % self.tiles_per_lat)
        n = jnp.where(is_ctx, self.tiles_per_ctx, self.tiles_per_lat)
        return pos, n


def _row_spec(width, tile=TOKEN_TILE):
    return pl.BlockSpec((tile, width), lambda i: (i, 0))


def _const_spec(shape):
    nd = len(shape)
    return pl.BlockSpec(shape, lambda i: (0,) * nd)


def _mod_kernel(c_ref, w_ref, b_ref, o_ref):
    s = _silu(c_ref[...])
    o_ref[0] = jnp.dot(s, w_ref[0], precision=HIGHEST, preferred_element_type=F32) + b_ref[0]


def _modulation(cvecs, w_mod, b_mod):
    depth, d, n = w_mod.shape
    tn = n // 4
    return pl.pallas_call(
        _mod_kernel,
        grid=(depth, n // tn),
        in_specs=[pl.BlockSpec(cvecs.shape, lambda l, j: (0, 0)),
                  pl.BlockSpec((1, d, tn), lambda l, j: (l, 0, j)),
                  pl.BlockSpec((1, 1, tn), lambda l, j: (l, 0, j))],
        out_specs=pl.BlockSpec((1, cvecs.shape[0], tn), lambda l, j: (l, 0, j)),
        out_shape=jax.ShapeDtypeStruct((depth, cvecs.shape[0], n), F32),
        compiler_params=_params("parallel", "parallel"),
        name="modulation",
    )(cvecs, w_mod, b_mod.reshape(depth, 1, n))


def _inproj_kernel(x_ref, mod_ref, g_ref, cos_ref, sa_ref, sb_ref, wzT, wxbc, wdt, wdtT, wq, wk, wv,
                   zT_o, xbc_o, dt_o, dtT_o, qs_o, kr_o, vt_o, k_o, v_o, *, ctx_tiles):
    is_ctx = pl.program_id(0) < ctx_tiles
    m = mod_ref[0]
    h = _modnorm(x_ref[...], g_ref[...], m[1:2], m[0:1]).astype(BF16)
    zT_o[...] = lax.dot_general(wzT[...], h, NT_DIMS, preferred_element_type=F32)
    xbc_o[...] = _bdot(h, wxbc[...])
    dt_o[...] = _bdot(h, wdt[...])
    dtT_o[...] = lax.dot_general(wdtT[...], h, NT_DIMS, preferred_element_type=F32)
    q = _bdot(h, wq[...])
    k = _bdot(h, wk[...])
    v = _bdot(h, wv[...])

    @pl.when(is_ctx)
    def _():
        k_o[...] = k
        v_o[...] = v

    cos = jnp.where(is_ctx, 1.0, cos_ref[...])
    sa = jnp.where(is_ctx, 0.0, sa_ref[...])
    sb = jnp.where(is_ctx, 0.0, sb_ref[...])
    tile = q.shape[0]
    first_map = lax.broadcasted_iota(I32, (tile, DA_V_DIM), 1) < DA_QK_DIM
    scale = LOG2_E / math.sqrt(DA_QK_DIM)

    def rope(t):
        return (t * cos + pltpu.roll(t, DA_V_DIM - ROPE_FREQS, 1) * sa + pltpu.roll(t, ROPE_FREQS, 1) * sb)

    for hd in range(q.shape[1] // DA_V_DIM):
        sl = slice(hd * DA_V_DIM, (hd + 1) * DA_V_DIM)
        qr = rope(q[:, sl]) * scale
        qs_o[0, :, sl] = jnp.where(first_map, qr, 0.0).astype(BF16)
        qs_o[1, :, sl] = jnp.where(first_map, 0.0, qr).astype(BF16)
        kr_o[:, sl] = rope(k[:, sl]).astype(BF16)
        r0 = hd * V_ROWS
        vt_o[r0:r0 + DA_V_DIM, :] = v[:, sl].T.astype(BF16)
        vt_o[r0 + DA_V_DIM:r0 + V_ROWS, :] = jnp.ones((V_ROWS - DA_V_DIM, tile), BF16)


def _rope_tables(length):
    t = jnp.arange(length)
    pos = jnp.stack([t // GRID_W, t % GRID_W], axis=-1).astype(F32)
    inv = ROPE_THETA ** (-jnp.arange(ROPE_FREQS, dtype=F32) / ROPE_FREQS)
    ang = pos[..., None] * inv
    lane = jnp.arange(DA_V_DIM)
    axis = (lane % DA_QK_DIM) // (2 * ROPE_FREQS)
    freq = lane % ROPE_FREQS
    second_half = (lane % (2 * ROPE_FREQS)) >= ROPE_FREQS
    a = ang[:, axis, freq]
    cos, sin = jnp.cos(a), jnp.sin(a)
    return cos, jnp.where(second_half, 0.0, -sin), jnp.where(second_half, sin, 0.0)


def _inproj(rows, x, mod_l, g, weights):
    t, d = x.shape
    wzT, wxbc, wdt, wdtT, wq, wk, wv = weights
    w = wq.shape[1]
    vt_rows = wv.shape[1] // DA_V_DIM * V_ROWS
    tables = _rope_tables(rows.len_lat)

    def tab(i):
        return (jnp.maximum(i - rows.ctx_tiles, 0) % rows.tiles_per_lat, 0)

    def col_spec(n):
        return pl.BlockSpec((n, TOKEN_TILE), lambda i: (0, i))

    ctx_spec = pl.BlockSpec((TOKEN_TILE, w), lambda i: (jnp.minimum(i, rows.ctx_tiles - 1), 0))
    return pl.pallas_call(
        functools.partial(_inproj_kernel, ctx_tiles=rows.ctx_tiles),
        grid=(rows.tiles,),
        in_specs=[_row_spec(d),
                  pl.BlockSpec((1, N_MOD, d), lambda i: (rows.mod_row(i), 0, 0)),
                  _const_spec((1, d))] + [pl.BlockSpec((TOKEN_TILE, DA_V_DIM), tab)] * 3
                 + [_const_spec(wt.shape) for wt in weights],
        out_specs=[col_spec(wzT.shape[0]), _row_spec(wxbc.shape[1]), _row_spec(wdt.shape[1]),
                   col_spec(wdtT.shape[0]), pl.BlockSpec((2, TOKEN_TILE, w), lambda i: (0, i, 0)),
                   _row_spec(w), col_spec(vt_rows), ctx_spec, ctx_spec],
        out_shape=[jax.ShapeDtypeStruct((wzT.shape[0], t), F32), jax.ShapeDtypeStruct((t, wxbc.shape[1]), F32),
                   jax.ShapeDtypeStruct((t, wdt.shape[1]), F32), jax.ShapeDtypeStruct((wdtT.shape[0], t), F32),
                   jax.ShapeDtypeStruct((2, t, w), BF16), jax.ShapeDtypeStruct((t, w), BF16),
                   jax.ShapeDtypeStruct((vt_rows, t), BF16),
                   jax.ShapeDtypeStruct((rows.t_ctx, w), F32), jax.ShapeDtypeStruct((rows.t_ctx, w), F32)],
        compiler_params=_params("arbitrary"),
        name="inproj",
    )(x, mod_l, g.reshape(1, d), *tables, *weights)


def _conv_kernel(cur_ref, prev_ref, next_ref, w_ref, b_ref, xsT_o, b_o, c_o, ext_ref, *, rows, d_ssm, gn):
    i = pl.program_id(0)
    pos, n = rows.seq_pos(i)
    tile = cur_ref.shape[0]
    ext_ref[0:CONV_HALO, :] = jnp.where(pos == 0, 0.0, prev_ref[...])
    ext_ref[CONV_HALO:CONV_HALO + tile, :] = cur_ref[...]
    ext_ref[CONV_HALO + tile:, :] = jnp.where(pos == n - 1, 0.0, next_ref[...])
    width = cur_ref.shape[1]
    step = 512
    for c0 in range(0, width, step):
        acc = jnp.broadcast_to(b_ref[:, c0:c0 + step], (tile, step))
        for k in range(CONV_WIDTH):
            off = CONV_HALO - CONV_WIDTH // 2 + k
            acc = acc + w_ref[k:k + 1, c0:c0 + step] * ext_ref[off:off + tile, c0:c0 + step]
        y = _silu(acc)
        if c0 < d_ssm:
            xsT_o[c0:c0 + step, :] = y.T
        elif c0 < d_ssm + gn:
            b_o[:, c0 - d_ssm:c0 - d_ssm + step] = y.astype(BF16)
        else:
            c_o[:, c0 - d_ssm - gn:c0 - d_ssm - gn + step] = y.astype(BF16)


def _conv(rows, xbc, conv_w, conv_b, d_ssm):
    t, width = xbc.shape
    gn = (width - d_ssm) // 2
    hb = TOKEN_TILE // CONV_HALO
    last = t // CONV_HALO - 1
    return pl.pallas_call(
        functools.partial(_conv_kernel, rows=rows, d_ssm=d_ssm, gn=gn),
        grid=(rows.tiles,),
        in_specs=[_row_spec(width),
                  pl.BlockSpec((CONV_HALO, width), lambda i: (jnp.maximum(i * hb - 1, 0), 0)),
                  pl.BlockSpec((CONV_HALO, width), lambda i: (jnp.minimum((i + 1) * hb, last), 0)),
                  _const_spec(conv_w.shape), _const_spec((1, width))],
        out_specs=[pl.BlockSpec((d_ssm, TOKEN_TILE), lambda i: (0, i)), _row_spec(gn), _row_spec(gn)],
        out_shape=[jax.ShapeDtypeStruct((d_ssm, t), F32),
                   jax.ShapeDtypeStruct((t, gn), BF16),
                   jax.ShapeDtypeStruct((t, gn), BF16)],
        scratch_shapes=[pltpu.VMEM((TOKEN_TILE + 2 * CONV_HALO, width), F32)],
        compiler_params=_params("parallel"),
        name="conv",
    )(xbc, xbc, xbc, conv_w, conv_b.reshape(1, width))


def _contributes(q, reverse, target_rows):
    r = lax.broadcasted_iota(I32, (q, q), 0)
    c = lax.broadcasted_iota(I32, (q, q), 1)
    tgt, src = (r, c) if target_rows else (c, r)
    return (src >= tgt) if reverse else (src <= tgt)


def _ssd_direction(xT_ref, b_ref, c_ref, dt_c, dt_r, bias_row, bias_col, a_row, a_col,
                   h_refs, yT_ref, reverse):
    q = CHUNK
    heads = a_row.shape[1]
    dt_c = _softplus(dt_c + bias_row)
    dt_r = _softplus(dt_r + bias_col)
    keep_t = _contributes(q, reverse, False)
    acs_c = jnp.dot(_contributes(q, reverse, True).astype(F32), dt_c * a_row,
                    precision=HIGHEST, preferred_element_type=F32)
    acs_r = jnp.dot(dt_r * a_col, keep_t.astype(F32), precision=HIGHEST, preferred_element_type=F32)
    last = 0 if reverse else q - 1
    total_r = acs_r[:, last:last + 1]
    grow_r = jnp.exp(acs_r)
    to_end_r = jnp.exp(total_r - acs_r)
    carry_r = jnp.exp(total_r)
    per_group = heads // SSM_GROUPS
    p = SSM_HEAD_DIM
    for g in range(SSM_GROUPS):
        bg = b_ref[:, g * SSM_STATE:(g + 1) * SSM_STATE]
        cg = c_ref[:, g * SSM_STATE:(g + 1) * SSM_STATE]
        cb_t = lax.dot_general(bg, cg, NT_DIMS, preferred_element_type=F32)
        h_in = h_refs[g][...]
        y_off = lax.dot_general(h_in.astype(BF16), cg, NT_DIMS, preferred_element_type=F32)
        xw, h_decayed = [], []
        for r in range(per_group):
            h = g * per_group + r
            rs = slice(r * p, (r + 1) * p)
            seg = acs_r[h:h + 1, :] - acs_c[:, h:h + 1]
            decay = jnp.exp(jnp.where(keep_t, seg, -jnp.inf))
            l_t = (cb_t * decay).astype(BF16)
            xdt = xT_ref[h * p:(h + 1) * p, :] * dt_r[h:h + 1, :]
            yT_ref[h * p:(h + 1) * p, :] = _bdot(xdt.astype(BF16), l_t) + y_off[rs, :] * grow_r[h:h + 1, :]
            xw.append((xdt * to_end_r[h:h + 1, :]).astype(BF16))
            h_decayed.append(h_in[rs, :] * carry_r[h:h + 1, :])
        state = _bdot(jnp.concatenate(xw, axis=0), bg)
        h_refs[g][...] = jnp.concatenate(h_decayed, axis=0) + state


def _ssd_kernel(xf, bf, cf, dtf, dtTf, xb, bb, cb, dtb, dtTb, bias_ref, biasT_ref, alog_ref, alogT_ref,
                hf0, hb0, yf_o, yb_o, hf_o, hb_o, *scratch, geom):
    s = pl.program_id(0)
    is_ctx, _, c, nc, _ = geom(s)
    heads = alog_ref.shape[1]
    hf_scr, hb_scr = scratch[:SSM_GROUPS], scratch[SSM_GROUPS:]
    rows_g = hf_scr[0].shape[0]

    @pl.when(c == 0)
    def _():
        for g in range(SSM_GROUPS):
            hf_scr[g][...] = jnp.where(is_ctx, 0.0, hf0[0, g * rows_g:(g + 1) * rows_g, :])
            hb_scr[g][...] = jnp.where(is_ctx, 0.0, hb0[0, g * rows_g:(g + 1) * rows_g, :])

    a_row = -jnp.exp(alog_ref[...])
    a_col = -jnp.exp(alogT_ref[...])
    _ssd_direction(xf, bf, cf, dtf[:, 0:heads], dtTf[0:heads, :], bias_ref[0:1, :], biasT_ref[:, 0:1],
                   a_row[0:1, :], a_col[:, 0:1], hf_scr, yf_o, False)
    _ssd_direction(xb, bb, cb, dtb[:, heads:2 * heads], dtTb[heads:2 * heads, :], bias_ref[1:2, :],
                   biasT_ref[:, 1:2], a_row[1:2, :], a_col[:, 1:2], hb_scr, yb_o, True)

    @pl.when(c == nc - 1)
    def _():
        for g in range(SSM_GROUPS):
            hf_o[0, g * rows_g:(g + 1) * rows_g, :] = hf_scr[g][...]
            hb_o[0, g * rows_g:(g + 1) * rows_g, :] = hb_scr[g][...]


def _ssd(rows, xsT, bm, cm, dt, dtT, dt_bias, a_log, hf0, hb0):
    d_ssm, t = xsT.shape
    gn = bm.shape[1]
    heads2 = dt.shape[1]
    ncc, ncl = rows.len_ctx // CHUNK, rows.len_lat // CHUNK
    s_ctx, s_lat = rows.n_ctx * ncc, rows.n_lat * ncl
    n_seq = rows.n_ctx + rows.n_lat

    def geom(s):
        is_ctx = s < s_ctx
        sl = s - s_ctx
        seq = jnp.where(is_ctx, s // ncc, rows.n_ctx + sl // ncl)
        c = jnp.where(is_ctx, s % ncc, sl % ncl)
        nc = jnp.where(is_ctx, ncc, ncl)
        base = jnp.where(is_ctx, (s // ncc) * ncc, s_ctx + (sl // ncl) * ncl)
        return is_ctx, seq, c, nc, base

    def fwd(s):
        _, _, c, _, base = geom(s)
        return base + c

    def bwd(s):
        _, _, c, nc, base = geom(s)
        return base + nc - 1 - c

    def lat_seq(s):
        return jnp.maximum(geom(s)[1] - rows.n_ctx, 0)

    def side(blk):
        return [pl.BlockSpec((d_ssm, CHUNK), lambda s: (0, blk(s))),
                pl.BlockSpec((CHUNK, gn), lambda s: (blk(s), 0)),
                pl.BlockSpec((CHUNK, gn), lambda s: (blk(s), 0)),
                pl.BlockSpec((CHUNK, heads2), lambda s: (blk(s), 0)),
                pl.BlockSpec((heads2, CHUNK), lambda s: (0, blk(s)))]

    state_shape = (SSM_HEAD_DIM * (heads2 // 2), SSM_STATE)
    group_shape = (state_shape[0] // SSM_GROUPS, SSM_STATE)
    state_spec_in = pl.BlockSpec((1,) + state_shape, lambda s: (lat_seq(s), 0, 0))
    state_spec_out = pl.BlockSpec((1,) + state_shape, lambda s: (geom(s)[1], 0, 0))
    return pl.pallas_call(
        functools.partial(_ssd_kernel, geom=geom),
        grid=(s_ctx + s_lat,),
        in_specs=side(fwd) + side(bwd) + [_const_spec(dt_bias.shape), _const_spec(dt_bias.T.shape),
                                          _const_spec(a_log.shape), _const_spec(a_log.T.shape),
                                          state_spec_in, state_spec_in],
        out_specs=[pl.BlockSpec((d_ssm, CHUNK), lambda s: (0, fwd(s))),
                   pl.BlockSpec((d_ssm, CHUNK), lambda s: (0, bwd(s))),
                   state_spec_out, state_spec_out],
        out_shape=[jax.ShapeDtypeStruct((d_ssm, t), F32), jax.ShapeDtypeStruct((d_ssm, t), F32),
                   jax.ShapeDtypeStruct((n_seq,) + state_shape, F32),
                   jax.ShapeDtypeStruct((n_seq,) + state_shape, F32)],
        scratch_shapes=[pltpu.VMEM(group_shape, F32)] * (2 * SSM_GROUPS),
        compiler_params=_params("arbitrary"),
        name="ssd",
    )(xsT, bm, cm, dt, dtT, xsT, bm, cm, dt, dtT, dt_bias, dt_bias.T, a_log, a_log.T, hf0, hb0)


def _attn_kernel(*refs, tq, tk, has_cache, lam_init):
    if has_cache:
        qs_ref, kc_ref, vct_ref, k_ref, vt_ref, lamv_ref, gain_ref, o_ref = refs
    else:
        qs_ref, k_ref, vt_ref, lamv_ref, gain_ref, o_ref = refs
    q = qs_ref[...].reshape(2 * tq, DA_V_DIM)
    blocks = []
    if has_cache:
        blocks += [(kc_ref, vct_ref, c, True) for c in range(kc_ref.shape[1] // tk)]
    blocks += [(k_ref, vt_ref, c, False) for c in range(k_ref.shape[0] // tk)]
    m = jnp.full((1, 2 * tq), -jnp.inf, F32)
    acc = jnp.zeros((V_ROWS, 2 * tq), F32)
    def scores(blk):
        kref, _, c, cached = blk
        ks = slice(c * tk, (c + 1) * tk)
        kb = kref[0, ks, :] if cached else kref[ks, :]
        return lax.dot_general(kb, q, NT_DIMS, preferred_element_type=F32)

    s_next = scores(blocks[0])
    for n, (_, vref, c, cached) in enumerate(blocks):
        ks = slice(c * tk, (c + 1) * tk)
        vb = vref[0, :, ks] if cached else vref[:, ks]
        s = s_next
        if n + 1 < len(blocks):
            s_next = scores(blocks[n + 1])
        m_new = jnp.maximum(m, jnp.max(s, axis=0, keepdims=True))
        acc = jnp.exp2(m - m_new) * acc + _bdot(vb, jnp.exp2(s - m_new).astype(BF16))
        m = m_new
    lv = lamv_ref[...]
    lam = (jnp.exp(jnp.sum(lv[0:1] * lv[1:2], axis=1, keepdims=True))
           - jnp.exp(jnp.sum(lv[2:3] * lv[3:4], axis=1, keepdims=True)) + lam_init)
    on = acc[0:DA_V_DIM, :] / acc[DA_V_DIM:DA_V_DIM + 1, :]
    ot = on[:, 0:tq] - lam * on[:, tq:2 * tq]
    ot = ot * lax.rsqrt(jnp.mean(ot * ot, axis=0, keepdims=True) + EPS) * gain_ref[...] * (1.0 - lam_init)
    o_ref[...] = ot.T.astype(BF16)


def _attention(qs, kr, vt, lam_vec, gain, lam_init, *, n_seq, length, row0, cache_k=None, cache_vt=None):
    w = kr.shape[1]
    heads = w // DA_V_DIM
    tq = min(length, ATTN_QUERY_TILE)
    tk = min(length, ATTN_KEY_BLOCK)
    assert length % tq == 0 and length % tk == 0 and row0 % length == 0
    nq = length // tq
    has_cache = cache_k is not None
    q_blk0, seq0 = row0 // tq, row0 // length

    in_specs = [pl.BlockSpec((2, tq, DA_V_DIM), lambda b, h, i: (0, q_blk0 + b * nq + i, h))]
    args = [qs]
    if has_cache:
        past = cache_k.shape[1]
        assert past % tk == 0
        in_specs += [pl.BlockSpec((1, past, DA_V_DIM), lambda b, h, i: (b, 0, h)),
                     pl.BlockSpec((1, V_ROWS, past), lambda b, h, i: (b, h, 0))]
        args += [cache_k, cache_vt]
    in_specs += [pl.BlockSpec((length, DA_V_DIM), lambda b, h, i: (seq0 + b, h)),
                 pl.BlockSpec((V_ROWS, length), lambda b, h, i: (h, seq0 + b)),
                 pl.BlockSpec(lam_vec.shape, lambda b, h, i: (0, 0)),
                 pl.BlockSpec((DA_V_DIM, 1), lambda b, h, i: (0, 0))]
    args += [kr, vt, lam_vec, gain.reshape(DA_V_DIM, 1)]
    return pl.pallas_call(
        functools.partial(_attn_kernel, tq=tq, tk=tk, has_cache=has_cache, lam_init=lam_init),
        grid=(n_seq, heads, nq),
        in_specs=in_specs,
        out_specs=pl.BlockSpec((tq, DA_V_DIM), lambda b, h, i: (b * nq + i, h)),
        out_shape=jax.ShapeDtypeStruct((n_seq * length, w), BF16),
        compiler_params=_params("parallel", "parallel", "arbitrary"),
        name="attention_lat" if has_cache else "attention_ctx",
    )(*args)


def _outproj_kernel(yfT_ref, ybT_ref, xsT_ref, zT_ref, oc_ref, ol_ref, x_ref, mod_ref, dsk_ref, gain_ref,
                    wy_ref, wo_ref, out_ref, *, ctx_tiles):
    is_ctx = pl.program_id(0) < ctx_tiles
    y = yfT_ref[...] + ybT_ref[...] + (dsk_ref[0] + dsk_ref[1]) * xsT_ref[...]
    y = y * _silu(zT_ref[...])
    y = (y * lax.rsqrt(jnp.mean(y * y, axis=0, keepdims=True) + EPS) * gain_ref[...]).astype(BF16)
    o = jnp.where(is_ctx, oc_ref[...], ol_ref[...])
    mixed = lax.dot_general(y, wy_ref[...], TN_DIMS, preferred_element_type=F32) + _bdot(o, wo_ref[...])
    out_ref[...] = x_ref[...] + mod_ref[0][2:3] * mixed


def _outproj(rows, yfT, ybT, xsT, zT, o_ctx, o_lat, x, mod_l, d_skip, gain, wy, wo):
    t, d = x.shape
    d_ssm = xsT.shape[0]
    wa = o_ctx.shape[1]
    col_spec = pl.BlockSpec((d_ssm, TOKEN_TILE), lambda i: (0, i))
    return pl.pallas_call(
        functools.partial(_outproj_kernel, ctx_tiles=rows.ctx_tiles),
        grid=(rows.tiles,),
        in_specs=[col_spec] * 4 + [
            pl.BlockSpec((TOKEN_TILE, wa), lambda i: (jnp.minimum(i, rows.ctx_tiles - 1), 0)),
            pl.BlockSpec((TOKEN_TILE, wa), lambda i: (jnp.maximum(i - rows.ctx_tiles, 0), 0)),
            _row_spec(d),
            pl.BlockSpec((1, N_MOD, d), lambda i: (rows.mod_row(i), 0, 0)),
            _const_spec((2, d_ssm, 1)), _const_spec((d_ssm, 1)), _const_spec(wy.shape), _const_spec(wo.shape)],
        out_specs=_row_spec(d),
        out_shape=jax.ShapeDtypeStruct((t, d), F32),
        compiler_params=_params("parallel"),
        name="outproj",
    )(yfT, ybT, xsT, zT, o_ctx, o_lat, x, mod_l, d_skip.reshape(2, d_ssm, 1), gain.reshape(d_ssm, 1), wy, wo)


def _pool_kernel(cur_ref, prev_ref, next_ref, mod_ref, g_ref, w_ref, b_ref, sc_ref, out_ref, ext_ref,
                 *, rows, len_ctx, len_lat):
    i = pl.program_id(0)
    pos, n = rows.seq_pos(i)
    tile = cur_ref.shape[0]
    m = mod_ref[0]
    g = g_ref[...]

    def norm(x):
        return _modnorm(x, g, m[1:2], m[0:1])

    x = cur_ref[...]
    ext_ref[0:CONV_HALO, :] = jnp.where(pos == 0, 0.0, norm(prev_ref[...]))
    ext_ref[CONV_HALO:CONV_HALO + tile, :] = norm(x)
    ext_ref[CONV_HALO + tile:, :] = jnp.where(pos == n - 1, 0.0, norm(next_ref[...]))
    length = jnp.where(i < rows.ctx_tiles, len_ctx, len_lat)
    tpos = pos * tile + lax.broadcasted_iota(I32, (tile, 1), 0)
    pg = cur_ref.shape[1] // len(POOL_WINDOWS)
    for gi, win in enumerate(POOL_WINDOWS):
        cs = slice(gi * pg, (gi + 1) * pg)
        half = win // 2
        acc = ext_ref[CONV_HALO - half:CONV_HALO - half + tile, cs]
        for off in range(1 - half, half):
            acc = acc + ext_ref[CONV_HALO + off:CONV_HALO + off + tile, cs]
        cnt = jnp.minimum(tpos + half, length) - jnp.maximum(tpos - half, 0)
        pooled = acc / cnt.astype(F32) - ext_ref[CONV_HALO:CONV_HALO + tile, cs]
        mixed = (_bdot(pooled.astype(BF16), w_ref[gi]) + b_ref[gi:gi + 1, :]) * sc_ref[:, cs]
        out_ref[:, cs] = x[:, cs] + m[2:3, cs] * mixed


def _pool(rows, x, mod_l, g, pool_w, pool_b, pool_scale):
    t, d = x.shape
    hb = TOKEN_TILE // CONV_HALO
    last = t // CONV_HALO - 1
    return pl.pallas_call(
        functools.partial(_pool_kernel, rows=rows, len_ctx=rows.len_ctx, len_lat=rows.len_lat),
        grid=(rows.tiles,),
        in_specs=[_row_spec(d),
                  pl.BlockSpec((CONV_HALO, d), lambda i: (jnp.maximum(i * hb - 1, 0), 0)),
                  pl.BlockSpec((CONV_HALO, d), lambda i: (jnp.minimum((i + 1) * hb, last), 0)),
                  pl.BlockSpec((1, N_MOD, d), lambda i: (rows.mod_row(i), 0, 0)),
                  _const_spec((1, d)), _const_spec(pool_w.shape), _const_spec(pool_b.shape),
                  _const_spec((1, d))],
        out_specs=_row_spec(d),
        out_shape=jax.ShapeDtypeStruct((t, d), F32),
        scratch_shapes=[pltpu.VMEM((TOKEN_TILE + 2 * CONV_HALO, d), F32)],
        compiler_params=_params("parallel"),
        name="pool",
    )(x, x, x, mod_l, g.reshape(1, d), pool_w, pool_b, pool_scale.reshape(1, d))


def _router_kernel(x_ref, mod_ref, g_ref, whi_ref, wlo_ref, bias_ref, ri_o, rf_o, cnt_o, cnt_scr,
                   *, n_experts):
    i = pl.program_id(0)

    @pl.when(i == 0)
    def _():
        cnt_scr[...] = jnp.zeros(cnt_scr.shape, F32)

    m = mod_ref[0]
    h = _modnorm(x_ref[...], g_ref[...], m[4:5], m[3:4])
    tm = h.shape[0]
    h_hi = h.astype(BF16)
    h_lo = (h - h_hi.astype(F32)).astype(BF16)
    lg = (lax.dot_general(whi_ref[...], h_hi, NT_DIMS, preferred_element_type=F32)
          + lax.dot_general(wlo_ref[...], h_hi, NT_DIMS, preferred_element_type=F32)
          + lax.dot_general(whi_ref[...], h_lo, NT_DIMS, preferred_element_type=F32)) + bias_ref[...]
    row = lax.broadcasted_iota(I32, (n_experts, tm), 0)
    big = jnp.int32(n_experts + MOE_GROUPS)
    lc = lg[n_experts:n_experts + 8, :]
    crow = lax.broadcasted_iota(I32, (8, tm), 0)
    lc = jnp.where(crow < MOE_GROUPS, lc, -jnp.inf)
    mc = jnp.max(lc, axis=0, keepdims=True)
    group_p = 1.0 / jnp.sum(jnp.exp(lc - mc), axis=0, keepdims=True)
    g_idx = jnp.min(jnp.where(lc == mc, crow, big), axis=0, keepdims=True)
    lf = jnp.where(row // EXPERTS_PER_GROUP == g_idx, lg[0:n_experts, :], -jnp.inf)
    m1 = jnp.max(lf, axis=0, keepdims=True)
    e1 = jnp.min(jnp.where(lf == m1, row, big), axis=0, keepdims=True)
    lf2 = jnp.where(row == e1, -jnp.inf, lf)
    m2 = jnp.max(lf2, axis=0, keepdims=True)
    e2 = jnp.min(jnp.where(lf2 == m2, row, big), axis=0, keepdims=True)
    r2 = jnp.exp(m2 - m1)
    gate1 = group_p / (1.0 + r2)
    gate2 = group_p * r2 / (1.0 + r2)
    oh1 = (row == e1)
    oh2 = (row == e2)
    oh = jnp.where(oh1 | oh2, 1.0, 0.0)
    before = (lax.broadcasted_iota(I32, (tm, tm), 0) < lax.broadcasted_iota(I32, (tm, tm), 1))
    prior = _bdot(oh.astype(BF16), before.astype(BF16)) + cnt_scr[...]
    rank1 = jnp.sum(jnp.where(oh1, prior, 0.0), axis=0, keepdims=True)
    rank2 = jnp.sum(jnp.where(oh2, prior, 0.0), axis=0, keepdims=True)
    cnt_scr[...] = cnt_scr[...] + jnp.sum(oh, axis=1, keepdims=True)
    zi = jnp.zeros((4, tm), I32)
    ri_o[0] = jnp.concatenate([e1, e2, rank1.astype(I32), rank2.astype(I32), zi], axis=0)
    rf_o[0] = jnp.concatenate([gate1, gate2, jnp.zeros((6, tm), F32)], axis=0)
    cnt_o[...] = jnp.broadcast_to(cnt_scr[...], cnt_o.shape)


def _router(rows, x, mod_l, g, w_hi, w_lo, bias, n_experts):
    t, d = x.shape
    tm = TOKEN_TILE
    nt = t // tm
    return pl.pallas_call(
        functools.partial(_router_kernel, n_experts=n_experts),
        grid=(nt,),
        in_specs=[_row_spec(d), pl.BlockSpec((1, N_MOD, d), lambda i: (rows.mod_row(i), 0, 0)),
                  _const_spec((1, d)), _const_spec(w_hi.shape), _const_spec(w_lo.shape),
                  _const_spec(bias.shape)],
        out_specs=[pl.BlockSpec((1, 8, tm), lambda i: (i, 0, 0)),
                   pl.BlockSpec((1, 8, tm), lambda i: (i, 0, 0)),
                   _const_spec((n_experts, ROUTER_LANES))],
        out_shape=[jax.ShapeDtypeStruct((nt, 8, tm), I32),
                   jax.ShapeDtypeStruct((nt, 8, tm), F32),
                   jax.ShapeDtypeStruct((n_experts, ROUTER_LANES), F32)],
        scratch_shapes=[pltpu.VMEM((n_experts, 1), F32)],
        compiler_params=_params("arbitrary"),
        name="router",
    )(x, mod_l, g.reshape(1, d), w_hi, w_lo, bias)


def _dispatch_kernel(dest_ref, x_ref, mod_ref, g_ref, xs_in_ref, xs_ref, h_scr, sem):
    del xs_in_ref
    tm = x_ref.shape[0]
    m = mod_ref[0]
    h_scr[...] = _modnorm(x_ref[...], g_ref[...], m[4:5], m[3:4])

    def copy(t, slot):
        return pltpu.make_async_copy(h_scr.at[pl.ds(t, 1)], xs_ref.at[pl.ds(slot, 1)], sem)

    def start(t, carry):
        for k in range(MOE_TOP_K):
            copy(t, dest_ref[0, k, t]).start()
        return carry

    lax.fori_loop(0, tm, start, 0, unroll=DMA_ISSUE_UNROLL)
    for k in range(MOE_TOP_K):
        pltpu.make_async_copy(h_scr, xs_ref.at[pl.ds(0, tm)], sem).wait()


def _dispatch(rows, x, mod_l, g, dest, n_slots):
    t, d = x.shape
    nt, _, tm = dest.shape
    xs0 = jnp.zeros((n_slots, d), F32)
    return pl.pallas_call(
        _dispatch_kernel,
        grid=(nt,),
        in_specs=[pl.BlockSpec((1, MOE_TOP_K, tm), lambda i: (i, 0, 0), memory_space=pltpu.SMEM),
                  _row_spec(d), pl.BlockSpec((1, N_MOD, d), lambda i: (rows.mod_row(i), 0, 0)),
                  _const_spec((1, d)), pl.BlockSpec(memory_space=pl.ANY)],
        out_specs=pl.BlockSpec(memory_space=pl.ANY),
        out_shape=jax.ShapeDtypeStruct((n_slots, d), F32),
        scratch_shapes=[pltpu.VMEM((tm, d), F32), pltpu.SemaphoreType.DMA(())],
        input_output_aliases={4: 0},
        compiler_params=_params("arbitrary"),
        name="moe_dispatch",
    )(dest, x, mod_l, g.reshape(1, d), xs0)


def _expert_kernel(be_ref, nu_ref, x_ref, wg_ref, wu_ref, wd_ref, y_ref):
    j = pl.program_id(0)

    @pl.when(j < nu_ref[0])
    def _():
        x = x_ref[...].astype(BF16)
        a = _bdot(x, wg_ref[0, 0].astype(BF16))
        u = _bdot(x, wu_ref[0, 0].astype(BF16))
        y_ref[...] = _bdot((_silu(a) * u).astype(BF16), wd_ref[0, 0].astype(BF16))

    @pl.when(j >= nu_ref[0])
    def _():
        y_ref[...] = jnp.zeros(y_ref.shape, F32)


def _experts(xs, block_expert, n_used, w_gate, w_up, w_down, layer):
    p, d = xs.shape
    hid = w_gate.shape[3]
    nb = p // EXPERT_ROWS
    return pl.pallas_call(
        _expert_kernel,
        grid_spec=pltpu.PrefetchScalarGridSpec(
            num_scalar_prefetch=2,
            grid=(nb,),
            in_specs=[pl.BlockSpec((EXPERT_ROWS, d), lambda j, be, nu: (j, 0)),
                      pl.BlockSpec((1, 1, d, hid), lambda j, be, nu: (layer, be[j], 0, 0)),
                      pl.BlockSpec((1, 1, d, hid), lambda j, be, nu: (layer, be[j], 0, 0)),
                      pl.BlockSpec((1, 1, hid, d), lambda j, be, nu: (layer, be[j], 0, 0))],
            out_specs=pl.BlockSpec((EXPERT_ROWS, d), lambda j, be, nu: (j, 0))),
        out_shape=jax.ShapeDtypeStruct((p, d), F32),
        compiler_params=_params("arbitrary"),
        name="moe_experts",
    )(block_expert, n_used, xs, w_gate, w_up, w_down)


def _combine_kernel(dest_ref, x_ref, mod_ref, gates_ref, ys_ref, out_ref, buf1, buf2, sem):
    tm = x_ref.shape[0]
    bufs = (buf1, buf2)

    def copy(slot, t, k):
        return pltpu.make_async_copy(ys_ref.at[pl.ds(slot, 1)], bufs[k].at[pl.ds(t, 1)], sem)

    def start(t, carry):
        for k in range(MOE_TOP_K):
            copy(dest_ref[0, k, t], t, k).start()
        return carry

    lax.fori_loop(0, tm, start, 0, unroll=DMA_ISSUE_UNROLL)
    eye = (lax.broadcasted_iota(I32, (tm, tm), 0) == lax.broadcasted_iota(I32, (tm, tm), 1))
    gates = gates_ref[0]
    g1 = jnp.sum(jnp.where(eye, gates[0:1, :], 0.0), axis=1, keepdims=True)
    g2 = jnp.sum(jnp.where(eye, gates[1:2, :], 0.0), axis=1, keepdims=True)
    for k in range(MOE_TOP_K):
        pltpu.make_async_copy(ys_ref.at[pl.ds(0, tm)], bufs[k], sem).wait()
    ffn = g1 * buf1[...] + g2 * buf2[...]
    out_ref[...] = x_ref[...] + mod_ref[0][5:6] * ffn


def _combine(rows, x, mod_l, dest, gates, ys):
    t, d = x.shape
    nt, _, tm = dest.shape
    return pl.pallas_call(
        _combine_kernel,
        grid=(nt,),
        in_specs=[pl.BlockSpec((1, MOE_TOP_K, tm), lambda i: (i, 0, 0), memory_space=pltpu.SMEM),
                  _row_spec(d), pl.BlockSpec((1, N_MOD, d), lambda i: (rows.mod_row(i), 0, 0)),
                  pl.BlockSpec((1, 8, tm), lambda i: (i, 0, 0)),
                  pl.BlockSpec(memory_space=pl.ANY)],
        out_specs=_row_spec(d),
        out_shape=jax.ShapeDtypeStruct((t, d), F32),
        scratch_shapes=[pltpu.VMEM((tm, d), F32), pltpu.VMEM((tm, d), F32), pltpu.SemaphoreType.DMA(())],
        compiler_params=_params("arbitrary"),
        name="moe_combine",
    )(dest, x, mod_l, gates, ys)


def _moe(rows, x, mod_l, g, w_coarse, b_coarse, w_fine, b_fine, w_gate, w_up, w_down, layer):
    t, d = x.shape
    n_experts = w_fine.shape[1]
    w_t = jnp.zeros((ROUTER_LANES, d), F32).at[:n_experts].set(w_fine.T)
    w_t = w_t.at[n_experts:n_experts + MOE_GROUPS].set(w_coarse.T)
    w_hi = w_t.astype(BF16)
    w_lo = (w_t - w_hi.astype(F32)).astype(BF16)
    bias = jnp.zeros((ROUTER_LANES, 1), F32).at[:n_experts, 0].set(b_fine)
    bias = bias.at[n_experts:n_experts + MOE_GROUPS, 0].set(b_coarse)
    ri, gates, counts = _router(rows, x, mod_l, g, w_hi, w_lo, bias, n_experts)
    counts = counts[:, 0].astype(I32)
    padded = ((counts + EXPERT_ROWS - 1) // EXPERT_ROWS) * EXPERT_ROWS
    pad_end = jnp.cumsum(padded)
    pad_start = pad_end - padded
    nb = -(-(t * MOE_TOP_K) // EXPERT_ROWS) + n_experts
    block_row0 = jnp.arange(nb, dtype=I32) * EXPERT_ROWS
    block_expert = jnp.minimum(jnp.sum((pad_end[None, :] <= block_row0[:, None]).astype(I32), axis=1),
                               n_experts - 1)
    n_used = (pad_end[-1:] // EXPERT_ROWS).astype(I32)
    chosen = ri[:, 0:MOE_TOP_K, :, None] == jnp.arange(n_experts, dtype=I32)
    dest = jnp.sum(jnp.where(chosen, pad_start, 0), axis=-1) + ri[:, MOE_TOP_K:2 * MOE_TOP_K, :]
    xs = _dispatch(rows, x, mod_l, g, dest, nb * EXPERT_ROWS)
    ys = _experts(xs, block_expert, n_used, w_gate, w_up, w_down, layer)
    return _combine(rows, x, mod_l, dest, gates, ys)


def _final_kernel(x_ref, g_ref, o_ref):
    o_ref[...] = _rms(x_ref[...], g_ref[...])


def _final_norm(x, g, row0, n_rows):
    d = x.shape[1]
    blk0 = row0 // TOKEN_TILE
    return pl.pallas_call(
        _final_kernel,
        grid=(n_rows // TOKEN_TILE,),
        in_specs=[pl.BlockSpec((TOKEN_TILE, d), lambda i: (blk0 + i, 0)), _const_spec((1, d))],
        out_specs=_row_spec(d),
        out_shape=jax.ShapeDtypeStruct((n_rows, d), F32),
        compiler_params=_params("parallel"),
        name="final_norm",
    )(x, g.reshape(1, d))


def kernel(x_prompt, x_sample, cache_k, cache_v, state_ssm_fwd, state_ssm_bwd, c, c_ctx, w_mod, b_mod, norm_mix, norm_ffn, norm_final, w_in, conv_w, conv_b, dt_bias, a_log, d_skip, ssm_gain, lam_vec, subln_gain, w_out, pool_w, pool_b, pool_scale, w_coarse, b_coarse, w_fine, b_fine, w_gate, w_up, w_down):
    n_ctx, len_ctx, d = x_prompt.shape
    n_lat, len_lat, _ = x_sample.shape
    rows = _Rows(n_ctx, len_ctx, n_lat, len_lat)
    depth = w_mod.shape[0]
    d_ssm = ssm_gain.shape[1]
    heads = a_log.shape[2]
    da_heads = cache_k.shape[3]
    qk_cols = da_heads * 2 * DA_QK_DIM
    xbc_cols = conv_w.shape[2]
    past = cache_k.shape[2]

    x = jnp.concatenate([x_prompt.reshape(rows.t_ctx, d), x_sample.reshape(rows.t_lat, d)], axis=0)
    n_vec = -(-(1 + n_lat) // 8) * 8
    cvecs = jnp.zeros((n_vec, d), F32).at[0].set(c_ctx).at[1:1 + n_lat].set(c)
    mod = _modulation(cvecs, w_mod, b_mod).reshape(depth, n_vec, N_MOD, d)

    ks, vs, sf, sb = [], [], [], []
    for l in range(depth):
        j = l // 2
        if l % 2 == 0:
            lam_init = 0.8 - 0.6 * math.exp(-0.3 * l)
            splits = [d_ssm, d_ssm + xbc_cols, d_ssm + xbc_cols + 2 * heads]
            splits += [splits[-1] + qk_cols, splits[-1] + 2 * qk_cols]
            wz, wxbc, wdt, wq, wk, wv = [w.astype(BF16) for w in jnp.split(w_in[j], splits, axis=1)]
            zT, xbc, dt, dtT, qs, kr, vt, k_ctx, v_ctx = _inproj(rows, x, mod[l], norm_mix[l],
                                                                 (wz.T, wxbc, wdt, wdt.T, wq, wk, wv))
            xsT, bm, cm = _conv(rows, xbc, conv_w[j], conv_b[j], d_ssm)
            hf0 = state_ssm_fwd[:, j].reshape(n_lat, heads * SSM_HEAD_DIM, SSM_STATE)
            hb0 = state_ssm_bwd[:, j].reshape(n_lat, heads * SSM_HEAD_DIM, SSM_STATE)
            yfT, ybT, hf, hb = _ssd(rows, xsT, bm, cm, dt, dtT, dt_bias[j], a_log[j], hf0, hb0)
            o_ctx = _attention(qs, kr, vt, lam_vec[j], subln_gain[j], lam_init,
                               n_seq=n_ctx, length=len_ctx, row0=0)
            ck = cache_k[:, j].reshape(n_lat, past, qk_cols).astype(BF16)
            cvt = jnp.transpose(cache_v[:, j], (0, 2, 3, 1))
            cvt = jnp.concatenate([cvt, jnp.ones((n_lat, da_heads, V_ROWS - DA_V_DIM, past), F32)], axis=2)
            cvt = cvt.reshape(n_lat, da_heads * V_ROWS, past).astype(BF16)
            o_lat = _attention(qs, kr, vt, lam_vec[j], subln_gain[j], lam_init,
                               n_seq=n_lat, length=len_lat, row0=rows.t_ctx, cache_k=ck, cache_vt=cvt)
            dsk = jnp.repeat(d_skip[j], SSM_HEAD_DIM, axis=1)
            wo = w_out[j].astype(BF16)
            x = _outproj(rows, yfT, ybT, xsT, zT, o_ctx, o_lat, x, mod[l], dsk, ssm_gain[j],
                         wo[:d_ssm], wo[d_ssm:])
            ks.append(k_ctx.reshape(n_ctx, len_ctx, da_heads, 2, DA_QK_DIM))
            vs.append(v_ctx.reshape(n_ctx, len_ctx, da_heads, DA_V_DIM))
            sf.append(hf[:n_ctx].reshape(n_ctx, heads, SSM_HEAD_DIM, SSM_STATE))
            sb.append(hb[:n_ctx].reshape(n_ctx, heads, SSM_HEAD_DIM, SSM_STATE))
        else:
            x = _pool(rows, x, mod[l], norm_mix[l], pool_w[j].astype(BF16), pool_b[j], pool_scale[j])
        x = _moe(rows, x, mod[l], norm_ffn[l], w_coarse[l], b_coarse[l], w_fine[l], b_fine[l],
                 w_gate, w_up, w_down, l)

    y_prompt = _final_norm(x, norm_final, 0, rows.t_ctx).reshape(n_ctx, len_ctx, d)
    y_sample = _final_norm(x, norm_final, rows.t_ctx, rows.t_lat).reshape(n_lat, len_lat, d)
    return (y_prompt, y_sample, jnp.stack(ks, axis=1), jnp.stack(vs, axis=1),
            jnp.stack(sf, axis=1), jnp.stack(sb, axis=1))
```

```python
import functools
import math

import jax
import jax.numpy as jnp
from jax import lax
from jax.experimental import pallas as pl
from jax.experimental.pallas import tpu as pltpu

F32 = jnp.float32
BF16 = jnp.bfloat16
I32 = jnp.int32
U32 = jnp.uint32
HIGHEST = lax.Precision.HIGHEST

EPS = 1e-6
N_MOD = 6
SSM_HEAD_DIM = 64
SSM_GROUPS = 4
SSM_STATE = 128
CHUNK = 128
CONV_WIDTH = 5
CONV_HALO = 8
DA_QK_DIM = 64
DA_V_DIM = 128
GRID_W = 64
ROPE_FREQS = 16
ROPE_THETA = 10000.0
POOL_WINDOWS = (2, 4, 8, 16)
MOE_GROUPS = 4
EXPERTS_PER_GROUP = 8
MOE_TOP_K = 2

TOKEN_TILE = 256
EXPERT_ROWS = 256
ATTN_KEY_BLOCK = 512
ATTN_QUERY_TILE = 1024
V_ROWS = DA_V_DIM + 16
LOG2_E = 1.4426950408889634
DMA_ISSUE_UNROLL = 8
ROUTER_LANES = 128
VMEM_LIMIT_BYTES = 56 * 1024 * 1024

NT_DIMS = (((1,), (1,)), ((), ()))
TN_DIMS = (((0,), (0,)), ((), ()))


def _params(*sem):
    return pltpu.CompilerParams(dimension_semantics=sem, vmem_limit_bytes=VMEM_LIMIT_BYTES)


def _silu(x):
    return x * jax.nn.sigmoid(x)


def _softplus(x):
    return jnp.maximum(x, 0.0) + jnp.log1p(jnp.exp(-jnp.abs(x)))


def _rms(x, g):
    return x * lax.rsqrt(jnp.mean(x * x, axis=-1, keepdims=True) + EPS) * g


def _modnorm(x, g, scale, shift):
    return _rms(x, g) * (1.0 + scale) + shift


def _bdot(a, b):
    return jnp.dot(a, b, preferred_element_type=F32)


class _Rows:
    def __init__(self, n_ctx, len_ctx, n_lat, len_lat):
        self.n_ctx, self.len_ctx, self.n_lat, self.len_lat = n_ctx, len_ctx, n_lat, len_lat
        self.t_ctx = n_ctx * len_ctx
        self.t_lat = n_lat * len_lat
        self.t = self.t_ctx + self.t_lat
        assert len_ctx % TOKEN_TILE == 0 and len_lat % TOKEN_TILE == 0
        self.ctx_tiles = self.t_ctx // TOKEN_TILE
        self.tiles = self.t // TOKEN_TILE
        self.tiles_per_ctx = len_ctx // TOKEN_TILE
        self.tiles_per_lat = len_lat // TOKEN_TILE

    def mod_row(self, i):
        return jnp.where(i < self.ctx_tiles, 0, 1 + (i - self.ctx_tiles) // self.tiles_per_lat)

    def seq_pos(self, i):
        is_ctx = i < self.ctx_tiles
        pos = jnp.where(is_ctx, i % self.tiles_per_ctx, (i - self.ctx_tiles) % self.tiles_per_lat)
        n = jnp.where(is_ctx, self.tiles_per_ctx, self.tiles_per_lat)
        return pos, n


def _row_spec(width, tile=TOKEN_TILE):
    return pl.BlockSpec((tile, width), lambda i: (i, 0))


def _const_spec(shape):
    nd = len(shape)
    return pl.BlockSpec(shape, lambda i: (0,) * nd)


def _mod_kernel(c_ref, w_ref, b_ref, o_ref):
    s = _silu(c_ref[...])
    o_ref[0] = _bdot(s.astype(BF16), w_ref[0].astype(BF16)) + b_ref[0]


def _modulation(cvecs, w_mod, b_mod):
    depth, d, n = w_mod.shape
    tn = n // 4
    return pl.pallas_call(
        _mod_kernel,
        grid=(depth, n // tn),
        in_specs=[pl.BlockSpec(cvecs.shape, lambda l, j: (0, 0)),
                  pl.BlockSpec((1, d, tn), lambda l, j: (l, 0, j)),
                  pl.BlockSpec((1, 1, tn), lambda l, j: (l, 0, j))],
        out_specs=pl.BlockSpec((1, cvecs.shape[0], tn), lambda l, j: (l, 0, j)),
        out_shape=jax.ShapeDtypeStruct((depth, cvecs.shape[0], n), F32),
        compiler_params=_params("parallel", "parallel"),
        name="modulation",
    )(cvecs, w_mod, b_mod.reshape(depth, 1, n))


def _inproj_kernel(x_ref, mod_ref, g_ref, cos_ref, sa_ref, sb_ref, wzT, wxbc, wdt, wdtT, wq, wk, wv,
                   zT_o, xbc_o, dt_o, dtT_o, qs_o, kr_o, vt_o, k_o, v_o, *, ctx_tiles):
    is_ctx = pl.program_id(0) < ctx_tiles
    m = mod_ref[0]
    h = _modnorm(x_ref[...], g_ref[...], m[1:2], m[0:1]).astype(BF16)
    zT_o[...] = lax.dot_general(wzT[...], h, NT_DIMS, preferred_element_type=F32)
    xbc_o[...] = _bdot(h, wxbc[...])
    dt_o[...] = _bdot(h, wdt[...])
    dtT_o[...] = lax.dot_general(wdtT[...], h, NT_DIMS, preferred_element_type=F32)
    q = _bdot(h, wq[...])
    k = _bdot(h, wk[...])
    v = _bdot(h, wv[...])

    @pl.when(is_ctx)
    def _():
        k_o[...] = k
        v_o[...] = v

    cos = jnp.where(is_ctx, 1.0, cos_ref[...])
    sa = jnp.where(is_ctx, 0.0, sa_ref[...])
    sb = jnp.where(is_ctx, 0.0, sb_ref[...])
    tile = q.shape[0]
    first_map = lax.broadcasted_iota(I32, (tile, DA_V_DIM), 1) < DA_QK_DIM
    scale = LOG2_E / math.sqrt(DA_QK_DIM)

    def rope(t):
        return (t * cos + pltpu.roll(t, DA_V_DIM - ROPE_FREQS, 1) * sa + pltpu.roll(t, ROPE_FREQS, 1) * sb)

    for hd in range(q.shape[1] // DA_V_DIM):
        sl = slice(hd * DA_V_DIM, (hd + 1) * DA_V_DIM)
        qr = rope(q[:, sl]) * scale
        qs_o[0, :, sl] = jnp.where(first_map, qr, 0.0).astype(BF16)
        qs_o[1, :, sl] = jnp.where(first_map, 0.0, qr).astype(BF16)
        kr_o[:, sl] = rope(k[:, sl]).astype(BF16)
        r0 = hd * V_ROWS
        vt_o[r0:r0 + DA_V_DIM, :] = v[:, sl].T.astype(BF16)
        vt_o[r0 + DA_V_DIM:r0 + V_ROWS, :] = jnp.ones((V_ROWS - DA_V_DIM, tile), BF16)


def _rope_tables(length):
    t = jnp.arange(length)
    pos = jnp.stack([t // GRID_W, t % GRID_W], axis=-1).astype(F32)
    inv = ROPE_THETA ** (-jnp.arange(ROPE_FREQS, dtype=F32) / ROPE_FREQS)
    ang = pos[..., None] * inv
    lane = jnp.arange(DA_V_DIM)
    axis = (lane % DA_QK_DIM) // (2 * ROPE_FREQS)
    freq = lane % ROPE_FREQS
    second_half = (lane % (2 * ROPE_FREQS)) >= ROPE_FREQS
    a = ang[:, axis, freq]
    cos, sin = jnp.cos(a), jnp.sin(a)
    return cos, jnp.where(second_half, 0.0, -sin), jnp.where(second_half, sin, 0.0)


def _inproj(rows, x, mod_l, g, weights):
    t, d = x.shape
    wzT, wxbc, wdt, wdtT, wq, wk, wv = weights
    w = wq.shape[1]
    vt_rows = wv.shape[1] // DA_V_DIM * V_ROWS
    tables = _rope_tables(rows.len_lat)

    def tab(i):
        return (jnp.maximum(i - rows.ctx_tiles, 0) % rows.tiles_per_lat, 0)

    def col_spec(n):
        return pl.BlockSpec((n, TOKEN_TILE), lambda i: (0, i))

    ctx_spec = pl.BlockSpec((TOKEN_TILE, w), lambda i: (jnp.minimum(i, rows.ctx_tiles - 1), 0))
    return pl.pallas_call(
        functools.partial(_inproj_kernel, ctx_tiles=rows.ctx_tiles),
        grid=(rows.tiles,),
        in_specs=[_row_spec(d),
                  pl.BlockSpec((1, N_MOD, d), lambda i: (rows.mod_row(i), 0, 0)),
                  _const_spec((1, d))] + [pl.BlockSpec((TOKEN_TILE, DA_V_DIM), tab)] * 3
                 + [_const_spec(wt.shape) for wt in weights],
        out_specs=[col_spec(wzT.shape[0]), _row_spec(wxbc.shape[1]), _row_spec(wdt.shape[1]),
                   col_spec(wdtT.shape[0]), pl.BlockSpec((2, TOKEN_TILE, w), lambda i: (0, i, 0)),
                   _row_spec(w), col_spec(vt_rows), ctx_spec, ctx_spec],
        out_shape=[jax.ShapeDtypeStruct((wzT.shape[0], t), F32), jax.ShapeDtypeStruct((t, wxbc.shape[1]), F32),
                   jax.ShapeDtypeStruct((t, wdt.shape[1]), F32), jax.ShapeDtypeStruct((wdtT.shape[0], t), F32),
                   jax.ShapeDtypeStruct((2, t, w), BF16), jax.ShapeDtypeStruct((t, w), BF16),
                   jax.ShapeDtypeStruct((vt_rows, t), BF16),
                   jax.ShapeDtypeStruct((rows.t_ctx, w), F32), jax.ShapeDtypeStruct((rows.t_ctx, w), F32)],
        compiler_params=_params("arbitrary"),
        name="inproj",
    )(x, mod_l, g.reshape(1, d), *tables, *weights)


def _conv_kernel(cur_ref, prev_ref, next_ref, w_ref, b_ref, xsT_o, b_o, c_o, ext_ref, *, rows, d_ssm, gn):
    i = pl.program_id(0)
    pos, n = rows.seq_pos(i)
    tile = cur_ref.shape[0]
    ext_ref[0:CONV_HALO, :] = jnp.where(pos == 0, 0.0, prev_ref[...])
    ext_ref[CONV_HALO:CONV_HALO + tile, :] = cur_ref[...]
    ext_ref[CONV_HALO + tile:, :] = jnp.where(pos == n - 1, 0.0, next_ref[...])
    width = cur_ref.shape[1]
    step = 512
    for c0 in range(0, width, step):
        acc = jnp.broadcast_to(b_ref[:, c0:c0 + step], (tile, step))
        for k in range(CONV_WIDTH):
            off = CONV_HALO - CONV_WIDTH // 2 + k
            acc = acc + w_ref[k:k + 1, c0:c0 + step] * ext_ref[off:off + tile, c0:c0 + step]
        y = _silu(acc)
        if c0 < d_ssm:
            xsT_o[c0:c0 + step, :] = y.T
        elif c0 < d_ssm + gn:
            b_o[:, c0 - d_ssm:c0 - d_ssm + step] = y.astype(BF16)
        else:
            c_o[:, c0 - d_ssm - gn:c0 - d_ssm - gn + step] = y.astype(BF16)


def _conv(rows, xbc, conv_w, conv_b, d_ssm):
    t, width = xbc.shape
    gn = (width - d_ssm) // 2
    hb = TOKEN_TILE // CONV_HALO
    last = t // CONV_HALO - 1
    return pl.pallas_call(
        functools.partial(_conv_kernel, rows=rows, d_ssm=d_ssm, gn=gn),
        grid=(rows.tiles,),
        in_specs=[_row_spec(width),
                  pl.BlockSpec((CONV_HALO, width), lambda i: (jnp.maximum(i * hb - 1, 0), 0)),
                  pl.BlockSpec((CONV_HALO, width), lambda i: (jnp.minimum((i + 1) * hb, last), 0)),
                  _const_spec(conv_w.shape), _const_spec((1, width))],
        out_specs=[pl.BlockSpec((d_ssm, TOKEN_TILE), lambda i: (0, i)), _row_spec(gn), _row_spec(gn)],
        out_shape=[jax.ShapeDtypeStruct((d_ssm, t), F32),
                   jax.ShapeDtypeStruct((t, gn), BF16),
                   jax.ShapeDtypeStruct((t, gn), BF16)],
        scratch_shapes=[pltpu.VMEM((TOKEN_TILE + 2 * CONV_HALO, width), F32)],
        compiler_params=_params("parallel"),
        name="conv",
    )(xbc, xbc, xbc, conv_w, conv_b.reshape(1, width))


def _contributes(q, reverse, target_rows):
    r = lax.broadcasted_iota(I32, (q, q), 0)
    c = lax.broadcasted_iota(I32, (q, q), 1)
    tgt, src = (r, c) if target_rows else (c, r)
    return (src >= tgt) if reverse else (src <= tgt)


def _ssd_direction(xT_ref, b_ref, c_ref, dt_c, dt_r, bias_row, bias_col, a_row, a_col,
                   h_refs, yT_ref, reverse):
    q = CHUNK
    heads = a_row.shape[1]
    dt_c = _softplus(dt_c + bias_row)
    dt_r = _softplus(dt_r + bias_col)
    keep_t = _contributes(q, reverse, False)
    acs_c = jnp.dot(_contributes(q, reverse, True).astype(F32), dt_c * a_row,
                    precision=HIGHEST, preferred_element_type=F32)
    acs_r = jnp.dot(dt_r * a_col, keep_t.astype(F32), precision=HIGHEST, preferred_element_type=F32)
    last = 0 if reverse else q - 1
    total_r = acs_r[:, last:last + 1]
    grow_r = jnp.exp(acs_r)
    to_end_r = jnp.exp(total_r - acs_r)
    carry_r = jnp.exp(total_r)
    per_group = heads // SSM_GROUPS
    p = SSM_HEAD_DIM
    for g in range(SSM_GROUPS):
        bg = b_ref[:, g * SSM_STATE:(g + 1) * SSM_STATE]
        cg = c_ref[:, g * SSM_STATE:(g + 1) * SSM_STATE]
        cb_t = lax.dot_general(bg, cg, NT_DIMS, preferred_element_type=F32)
        h_in = h_refs[g][...]
        y_off = lax.dot_general(h_in.astype(BF16), cg, NT_DIMS, preferred_element_type=F32)
        xw, h_decayed = [], []
        for r in range(per_group):
            h = g * per_group + r
            rs = slice(r * p, (r + 1) * p)
            seg = acs_r[h:h + 1, :] - acs_c[:, h:h + 1]
            decay = jnp.exp(jnp.where(keep_t, seg, -jnp.inf))
            l_t = (cb_t * decay).astype(BF16)
            xdt = xT_ref[h * p:(h + 1) * p, :] * dt_r[h:h + 1, :]
            yT_ref[h * p:(h + 1) * p, :] = _bdot(xdt.astype(BF16), l_t) + y_off[rs, :] * grow_r[h:h + 1, :]
            xw.append((xdt * to_end_r[h:h + 1, :]).astype(BF16))
            h_decayed.append(h_in[rs, :] * carry_r[h:h + 1, :])
        state = _bdot(jnp.concatenate(xw, axis=0), bg)
        h_refs[g][...] = jnp.concatenate(h_decayed, axis=0) + state


def _ssd_kernel(xf, bf, cf, dtf, dtTf, xb, bb, cb, dtb, dtTb, bias_ref, biasT_ref, alog_ref, alogT_ref,
                hf0, hb0, yf_o, yb_o, hf_o, hb_o, *scratch, geom):
    s = pl.program_id(0)
    is_ctx, _, c, nc, _ = geom(s)
    heads = alog_ref.shape[1]
    hf_scr, hb_scr = scratch[:SSM_GROUPS], scratch[SSM_GROUPS:]
    rows_g = hf_scr[0].shape[0]

    @pl.when(c == 0)
    def _():
        for g in range(SSM_GROUPS):
            hf_scr[g][...] = jnp.where(is_ctx, 0.0, hf0[0, g * rows_g:(g + 1) * rows_g, :])
            hb_scr[g][...] = jnp.where(is_ctx, 0.0, hb0[0, g * rows_g:(g + 1) * rows_g, :])

    a_row = -jnp.exp(alog_ref[...])
    a_col = -jnp.exp(alogT_ref[...])
    _ssd_direction(xf, bf, cf, dtf[:, 0:heads], dtTf[0:heads, :], bias_ref[0:1, :], biasT_ref[:, 0:1],
                   a_row[0:1, :], a_col[:, 0:1], hf_scr, yf_o, False)
    _ssd_direction(xb, bb, cb, dtb[:, heads:2 * heads], dtTb[heads:2 * heads, :], bias_ref[1:2, :],
                   biasT_ref[:, 1:2], a_row[1:2, :], a_col[:, 1:2], hb_scr, yb_o, True)

    @pl.when(c == nc - 1)
    def _():
        for g in range(SSM_GROUPS):
            hf_o[0, g * rows_g:(g + 1) * rows_g, :] = hf_scr[g][...]
            hb_o[0, g * rows_g:(g + 1) * rows_g, :] = hb_scr[g][...]


def _ssd(rows, xsT, bm, cm, dt, dtT, dt_bias, a_log, hf0, hb0):
    d_ssm, t = xsT.shape
    gn = bm.shape[1]
    heads2 = dt.shape[1]
    ncc, ncl = rows.len_ctx // CHUNK, rows.len_lat // CHUNK
    s_ctx, s_lat = rows.n_ctx * ncc, rows.n_lat * ncl
    n_seq = rows.n_ctx + rows.n_lat

    def geom(s):
        is_ctx = s < s_ctx
        sl = s - s_ctx
        seq = jnp.where(is_ctx, s // ncc, rows.n_ctx + sl // ncl)
        c = jnp.where(is_ctx, s % ncc, sl % ncl)
        nc = jnp.where(is_ctx, ncc, ncl)
        base = jnp.where(is_ctx, (s // ncc) * ncc, s_ctx + (sl // ncl) * ncl)
        return is_ctx, seq, c, nc, base

    def fwd(s):
        _, _, c, _, base = geom(s)
        return base + c

    def bwd(s):
        _, _, c, nc, base = geom(s)
        return base + nc - 1 - c

    def lat_seq(s):
        return jnp.maximum(geom(s)[1] - rows.n_ctx, 0)

    def side(blk):
        return [pl.BlockSpec((d_ssm, CHUNK), lambda s: (0, blk(s))),
                pl.BlockSpec((CHUNK, gn), lambda s: (blk(s), 0)),
                pl.BlockSpec((CHUNK, gn), lambda s: (blk(s), 0)),
                pl.BlockSpec((CHUNK, heads2), lambda s: (blk(s), 0)),
                pl.BlockSpec((heads2, CHUNK), lambda s: (0, blk(s)))]

    state_shape = (SSM_HEAD_DIM * (heads2 // 2), SSM_STATE)
    group_shape = (state_shape[0] // SSM_GROUPS, SSM_STATE)
    state_spec_in = pl.BlockSpec((1,) + state_shape, lambda s: (lat_seq(s), 0, 0))
    state_spec_out = pl.BlockSpec((1,) + state_shape, lambda s: (geom(s)[1], 0, 0))
    return pl.pallas_call(
        functools.partial(_ssd_kernel, geom=geom),
        grid=(s_ctx + s_lat,),
        in_specs=side(fwd) + side(bwd) + [_const_spec(dt_bias.shape), _const_spec(dt_bias.T.shape),
                                          _const_spec(a_log.shape), _const_spec(a_log.T.shape),
                                          state_spec_in, state_spec_in],
        out_specs=[pl.BlockSpec((d_ssm, CHUNK), lambda s: (0, fwd(s))),
                   pl.BlockSpec((d_ssm, CHUNK), lambda s: (0, bwd(s))),
                   state_spec_out, state_spec_out],
        out_shape=[jax.ShapeDtypeStruct((d_ssm, t), F32), jax.ShapeDtypeStruct((d_ssm, t), F32),
                   jax.ShapeDtypeStruct((n_seq,) + state_shape, F32),
                   jax.ShapeDtypeStruct((n_seq,) + state_shape, F32)],
        scratch_shapes=[pltpu.VMEM(group_shape, F32)] * (2 * SSM_GROUPS),
        compiler_params=_params("arbitrary"),
        name="ssd",
    )(xsT, bm, cm, dt, dtT, xsT, bm, cm, dt, dtT, dt_bias, dt_bias.T, a_log, a_log.T, hf0, hb0)


def _attn_kernel(*refs, tq, tk, has_cache, lam_init):
    if has_cache:
        qs_ref, kc_ref, vct_ref, k_ref, vt_ref, lamv_ref, gain_ref, o_ref = refs
    else:
        qs_ref, k_ref, vt_ref, lamv_ref, gain_ref, o_ref = refs
    q = qs_ref[...].reshape(2 * tq, DA_V_DIM)
    blocks = []
    if has_cache:
        blocks += [(kc_ref, vct_ref, c, True) for c in range(kc_ref.shape[1] // tk)]
    blocks += [(k_ref, vt_ref, c, False) for c in range(k_ref.shape[0] // tk)]
    m = jnp.full((1, 2 * tq), -jnp.inf, F32)
    acc = jnp.zeros((V_ROWS, 2 * tq), F32)
    def scores(blk):
        kref, _, c, cached = blk
        ks = slice(c * tk, (c + 1) * tk)
        kb = kref[0, ks, :] if cached else kref[ks, :]
        return lax.dot_general(kb, q, NT_DIMS, preferred_element_type=F32)

    s_next = scores(blocks[0])
    for n, (_, vref, c, cached) in enumerate(blocks):
        ks = slice(c * tk, (c + 1) * tk)
        vb = vref[0, :, ks] if cached else vref[:, ks]
        s = s_next
        if n + 1 < len(blocks):
            s_next = scores(blocks[n + 1])
        m_new = jnp.maximum(m, jnp.max(s, axis=0, keepdims=True))
        acc = jnp.exp2(m - m_new) * acc + _bdot(vb, jnp.exp2(s - m_new).astype(BF16))
        m = m_new
    lv = lamv_ref[...]
    lam = (jnp.exp(jnp.sum(lv[0:1] * lv[1:2], axis=1, keepdims=True))
           - jnp.exp(jnp.sum(lv[2:3] * lv[3:4], axis=1, keepdims=True)) + lam_init)
    on = acc[0:DA_V_DIM, :] / acc[DA_V_DIM:DA_V_DIM + 1, :]
    ot = on[:, 0:tq] - lam * on[:, tq:2 * tq]
    ot = ot * lax.rsqrt(jnp.mean(ot * ot, axis=0, keepdims=True) + EPS) * gain_ref[...] * (1.0 - lam_init)
    o_ref[...] = ot.T.astype(BF16)


def _attention(qs, kr, vt, lam_vec, gain, lam_init, *, n_seq, length, row0, cache_k=None, cache_vt=None):
    w = kr.shape[1]
    heads = w // DA_V_DIM
    tq = min(length, ATTN_QUERY_TILE)
    tk = min(length, ATTN_KEY_BLOCK)
    assert length % tq == 0 and length % tk == 0 and row0 % length == 0
    nq = length // tq
    has_cache = cache_k is not None
    q_blk0, seq0 = row0 // tq, row0 // length

    in_specs = [pl.BlockSpec((2, tq, DA_V_DIM), lambda b, h, i: (0, q_blk0 + b * nq + i, h))]
    args = [qs]
    if has_cache:
        past = cache_k.shape[1]
        assert past % tk == 0
        in_specs += [pl.BlockSpec((1, past, DA_V_DIM), lambda b, h, i: (b, 0, h)),
                     pl.BlockSpec((1, V_ROWS, past), lambda b, h, i: (b, h, 0))]
        args += [cache_k, cache_vt]
    in_specs += [pl.BlockSpec((length, DA_V_DIM), lambda b, h, i: (seq0 + b, h)),
                 pl.BlockSpec((V_ROWS, length), lambda b, h, i: (h, seq0 + b)),
                 pl.BlockSpec(lam_vec.shape, lambda b, h, i: (0, 0)),
                 pl.BlockSpec((DA_V_DIM, 1), lambda b, h, i: (0, 0))]
    args += [kr, vt, lam_vec, gain.reshape(DA_V_DIM, 1)]
    return pl.pallas_call(
        functools.partial(_attn_kernel, tq=tq, tk=tk, has_cache=has_cache, lam_init=lam_init),
        grid=(n_seq, heads, nq),
        in_specs=in_specs,
        out_specs=pl.BlockSpec((tq, DA_V_DIM), lambda b, h, i: (b * nq + i, h)),
        out_shape=jax.ShapeDtypeStruct((n_seq * length, w), BF16),
        compiler_params=_params("parallel", "parallel", "arbitrary"),
        name="attention_lat" if has_cache else "attention_ctx",
    )(*args)


def _outproj_kernel(yfT_ref, ybT_ref, xsT_ref, zT_ref, oc_ref, ol_ref, x_ref, mod_ref, dsk_ref, gain_ref,
                    wy_ref, wo_ref, out_ref, *, ctx_tiles):
    is_ctx = pl.program_id(0) < ctx_tiles
    y = yfT_ref[...] + ybT_ref[...] + (dsk_ref[0] + dsk_ref[1]) * xsT_ref[...]
    y = y * _silu(zT_ref[...])
    y = (y * lax.rsqrt(jnp.mean(y * y, axis=0, keepdims=True) + EPS) * gain_ref[...]).astype(BF16)
    o = jnp.where(is_ctx, oc_ref[...], ol_ref[...])
    mixed = lax.dot_general(y, wy_ref[...], TN_DIMS, preferred_element_type=F32) + _bdot(o, wo_ref[...])
    out_ref[...] = x_ref[...] + mod_ref[0][2:3] * mixed


def _outproj(rows, yfT, ybT, xsT, zT, o_ctx, o_lat, x, mod_l, d_skip, gain, wy, wo):
    t, d = x.shape
    d_ssm = xsT.shape[0]
    wa = o_ctx.shape[1]
    col_spec = pl.BlockSpec((d_ssm, TOKEN_TILE), lambda i: (0, i))
    return pl.pallas_call(
        functools.partial(_outproj_kernel, ctx_tiles=rows.ctx_tiles),
        grid=(rows.tiles,),
        in_specs=[col_spec] * 4 + [
            pl.BlockSpec((TOKEN_TILE, wa), lambda i: (jnp.minimum(i, rows.ctx_tiles - 1), 0)),
            pl.BlockSpec((TOKEN_TILE, wa), lambda i: (jnp.maximum(i - rows.ctx_tiles, 0), 0)),
            _row_spec(d),
            pl.BlockSpec((1, N_MOD, d), lambda i: (rows.mod_row(i), 0, 0)),
            _const_spec((2, d_ssm, 1)), _const_spec((d_ssm, 1)), _const_spec(wy.shape), _const_spec(wo.shape)],
        out_specs=_row_spec(d),
        out_shape=jax.ShapeDtypeStruct((t, d), F32),
        compiler_params=_params("parallel"),
        name="outproj",
    )(yfT, ybT, xsT, zT, o_ctx, o_lat, x, mod_l, d_skip.reshape(2, d_ssm, 1), gain.reshape(d_ssm, 1), wy, wo)


def _pool_kernel(cur_ref, prev_ref, next_ref, mod_ref, g_ref, w_ref, b_ref, sc_ref, out_ref, ext_ref,
                 *, rows, len_ctx, len_lat):
    i = pl.program_id(0)
    pos, n = rows.seq_pos(i)
    tile = cur_ref.shape[0]
    m = mod_ref[0]
    g = g_ref[...]

    def norm(x):
        return _modnorm(x, g, m[1:2], m[0:1])

    x = cur_ref[...]
    ext_ref[0:CONV_HALO, :] = jnp.where(pos == 0, 0.0, norm(prev_ref[...]))
    ext_ref[CONV_HALO:CONV_HALO + tile, :] = norm(x)
    ext_ref[CONV_HALO + tile:, :] = jnp.where(pos == n - 1, 0.0, norm(next_ref[...]))
    length = jnp.where(i < rows.ctx_tiles, len_ctx, len_lat)
    tpos = pos * tile + lax.broadcasted_iota(I32, (tile, 1), 0)
    pg = cur_ref.shape[1] // len(POOL_WINDOWS)
    for gi, win in enumerate(POOL_WINDOWS):
        cs = slice(gi * pg, (gi + 1) * pg)
        half = win // 2
        acc = ext_ref[CONV_HALO - half:CONV_HALO - half + tile, cs]
        for off in range(1 - half, half):
            acc = acc + ext_ref[CONV_HALO + off:CONV_HALO + off + tile, cs]
        cnt = jnp.minimum(tpos + half, length) - jnp.maximum(tpos - half, 0)
        pooled = acc / cnt.astype(F32) - ext_ref[CONV_HALO:CONV_HALO + tile, cs]
        mixed = (_bdot(pooled.astype(BF16), w_ref[gi]) + b_ref[gi:gi + 1, :]) * sc_ref[:, cs]
        out_ref[:, cs] = x[:, cs] + m[2:3, cs] * mixed


def _pool(rows, x, mod_l, g, pool_w, pool_b, pool_scale):
    t, d = x.shape
    hb = TOKEN_TILE // CONV_HALO
    last = t // CONV_HALO - 1
    return pl.pallas_call(
        functools.partial(_pool_kernel, rows=rows, len_ctx=rows.len_ctx, len_lat=rows.len_lat),
        grid=(rows.tiles,),
        in_specs=[_row_spec(d),
                  pl.BlockSpec((CONV_HALO, d), lambda i: (jnp.maximum(i * hb - 1, 0), 0)),
                  pl.BlockSpec((CONV_HALO, d), lambda i: (jnp.minimum((i + 1) * hb, last), 0)),
                  pl.BlockSpec((1, N_MOD, d), lambda i: (rows.mod_row(i), 0, 0)),
                  _const_spec((1, d)), _const_spec(pool_w.shape), _const_spec(pool_b.shape),
                  _const_spec((1, d))],
        out_specs=_row_spec(d),
        out_shape=jax.ShapeDtypeStruct((t, d), F32),
        scratch_shapes=[pltpu.VMEM((TOKEN_TILE + 2 * CONV_HALO, d), F32)],
        compiler_params=_params("parallel"),
        name="pool",
    )(x, x, x, mod_l, g.reshape(1, d), pool_w, pool_b, pool_scale.reshape(1, d))


def _router_kernel(x_ref, mod_ref, g_ref, w_ref, bias_ref, ri_o, rf_o, cnt_o, cnt_scr,
                   *, n_experts):
    i = pl.program_id(0)

    @pl.when(i == 0)
    def _():
        cnt_scr[...] = jnp.zeros(cnt_scr.shape, F32)

    m = mod_ref[0]
    h = _modnorm(x_ref[...], g_ref[...], m[4:5], m[3:4])
    tm = h.shape[0]
    lg = lax.dot_general(w_ref[...], h.astype(BF16), NT_DIMS, preferred_element_type=F32) + bias_ref[...]
    row = lax.broadcasted_iota(I32, (n_experts, tm), 0)
    big = jnp.int32(n_experts + MOE_GROUPS)
    lc = lg[n_experts:n_experts + 8, :]
    crow = lax.broadcasted_iota(I32, (8, tm), 0)
    lc = jnp.where(crow < MOE_GROUPS, lc, -jnp.inf)
    mc = jnp.max(lc, axis=0, keepdims=True)
    group_p = 1.0 / jnp.sum(jnp.exp(lc - mc), axis=0, keepdims=True)
    g_idx = jnp.min(jnp.where(lc == mc, crow, big), axis=0, keepdims=True)
    lf = jnp.where(row // EXPERTS_PER_GROUP == g_idx, lg[0:n_experts, :], -jnp.inf)
    m1 = jnp.max(lf, axis=0, keepdims=True)
    e1 = jnp.min(jnp.where(lf == m1, row, big), axis=0, keepdims=True)
    lf2 = jnp.where(row == e1, -jnp.inf, lf)
    m2 = jnp.max(lf2, axis=0, keepdims=True)
    e2 = jnp.min(jnp.where(lf2 == m2, row, big), axis=0, keepdims=True)
    r2 = jnp.exp(m2 - m1)
    gate1 = group_p / (1.0 + r2)
    gate2 = group_p * r2 / (1.0 + r2)
    oh1 = (row == e1)
    oh2 = (row == e2)
    oh = jnp.where(oh1 | oh2, 1.0, 0.0)
    before = (lax.broadcasted_iota(I32, (tm, tm), 0) < lax.broadcasted_iota(I32, (tm, tm), 1))
    prior = _bdot(oh.astype(BF16), before.astype(BF16)) + cnt_scr[...]
    rank1 = jnp.sum(jnp.where(oh1, prior, 0.0), axis=0, keepdims=True)
    rank2 = jnp.sum(jnp.where(oh2, prior, 0.0), axis=0, keepdims=True)
    cnt_scr[...] = cnt_scr[...] + jnp.sum(oh, axis=1, keepdims=True)
    zi = jnp.zeros((4, tm), I32)
    ri_o[0] = jnp.concatenate([e1, e2, rank1.astype(I32), rank2.astype(I32), zi], axis=0)
    rf_o[0] = jnp.concatenate([gate1, gate2, jnp.zeros((6, tm), F32)], axis=0)
    cnt_o[...] = jnp.broadcast_to(cnt_scr[...], cnt_o.shape)


def _router(rows, x, mod_l, g, w_t, bias, n_experts):
    t, d = x.shape
    tm = TOKEN_TILE
    nt = t // tm
    return pl.pallas_call(
        functools.partial(_router_kernel, n_experts=n_experts),
        grid=(nt,),
        in_specs=[_row_spec(d), pl.BlockSpec((1, N_MOD, d), lambda i: (rows.mod_row(i), 0, 0)),
                  _const_spec((1, d)), _const_spec(w_t.shape), _const_spec(bias.shape)],
        out_specs=[pl.BlockSpec((1, 8, tm), lambda i: (i, 0, 0)),
                   pl.BlockSpec((1, 8, tm), lambda i: (i, 0, 0)),
                   _const_spec((n_experts, ROUTER_LANES))],
        out_shape=[jax.ShapeDtypeStruct((nt, 8, tm), I32),
                   jax.ShapeDtypeStruct((nt, 8, tm), F32),
                   jax.ShapeDtypeStruct((n_experts, ROUTER_LANES), F32)],
        scratch_shapes=[pltpu.VMEM((n_experts, 1), F32)],
        compiler_params=_params("arbitrary"),
        name="router",
    )(x, mod_l, g.reshape(1, d), w_t, bias)


def _pack_bf16_pairs(x):
    n = x.shape[1] // 2
    hi = lax.bitcast_convert_type(x[:, :n].astype(BF16).astype(F32), U32)
    lo = lax.bitcast_convert_type(x[:, n:].astype(BF16).astype(F32), U32)
    return (hi & jnp.uint32(0xFFFF0000)) | (lo >> 16)


def _unpack_bf16_pairs(w):
    hi = lax.bitcast_convert_type(w & jnp.uint32(0xFFFF0000), F32)
    lo = lax.bitcast_convert_type(w << 16, F32)
    return hi.astype(BF16), lo.astype(BF16)


def _dispatch_kernel(dest_ref, x_ref, mod_ref, g_ref, xs_in_ref, xs_ref, h_scr, sem):
    del xs_in_ref
    tm = x_ref.shape[0]
    m = mod_ref[0]
    h_scr[...] = _pack_bf16_pairs(_modnorm(x_ref[...], g_ref[...], m[4:5], m[3:4]))

    def copy(t, slot):
        return pltpu.make_async_copy(h_scr.at[pl.ds(t, 1)], xs_ref.at[pl.ds(slot, 1)], sem)

    def start(t, carry):
        for k in range(MOE_TOP_K):
            copy(t, dest_ref[0, k, t]).start(priority=k % 2)
        return carry

    lax.fori_loop(0, tm, start, 0, unroll=DMA_ISSUE_UNROLL)
    for k in range(MOE_TOP_K):
        pltpu.make_async_copy(h_scr, xs_ref.at[pl.ds(0, tm)], sem).wait()


def _dispatch(rows, x, mod_l, g, dest, n_slots):
    t, d = x.shape
    nt, _, tm = dest.shape
    xs0 = jnp.zeros((n_slots, d // 2), U32)
    return pl.pallas_call(
        _dispatch_kernel,
        grid=(nt,),
        in_specs=[pl.BlockSpec((1, MOE_TOP_K, tm), lambda i: (i, 0, 0), memory_space=pltpu.SMEM),
                  _row_spec(d), pl.BlockSpec((1, N_MOD, d), lambda i: (rows.mod_row(i), 0, 0)),
                  _const_spec((1, d)), pl.BlockSpec(memory_space=pl.ANY)],
        out_specs=pl.BlockSpec(memory_space=pl.ANY),
        out_shape=jax.ShapeDtypeStruct((n_slots, d // 2), U32),
        scratch_shapes=[pltpu.VMEM((tm, d // 2), U32), pltpu.SemaphoreType.DMA(())],
        input_output_aliases={4: 0},
        compiler_params=_params("arbitrary"),
        name="moe_dispatch",
    )(dest, x, mod_l, g.reshape(1, d), xs0)


def _expert_kernel(be_ref, nu_ref, x_ref, wg_ref, wu_ref, wd_ref, y_ref):
    j = pl.program_id(0)

    @pl.when(j < nu_ref[0])
    def _():
        x_lo, x_hi = _unpack_bf16_pairs(x_ref[...])
        half = x_lo.shape[1]

        def proj(w_ref):
            return _bdot(x_lo, w_ref[0, 0, :half, :].astype(BF16)) + _bdot(x_hi, w_ref[0, 0, half:, :].astype(BF16))

        a = proj(wg_ref)
        u = proj(wu_ref)
        y_ref[...] = _bdot((_silu(a) * u).astype(BF16), wd_ref[0, 0].astype(BF16))

    @pl.when(j >= nu_ref[0])
    def _():
        y_ref[...] = jnp.zeros(y_ref.shape, F32)


def _experts(xs, block_expert, n_used, w_gate, w_up, w_down, layer):
    p = xs.shape[0]
    d, hid = w_gate.shape[2:]
    nb = p // EXPERT_ROWS
    return pl.pallas_call(
        _expert_kernel,
        grid_spec=pltpu.PrefetchScalarGridSpec(
            num_scalar_prefetch=2,
            grid=(nb,),
            in_specs=[pl.BlockSpec((EXPERT_ROWS, xs.shape[1]), lambda j, be, nu: (j, 0)),
                      pl.BlockSpec((1, 1, d, hid), lambda j, be, nu: (layer, be[j], 0, 0)),
                      pl.BlockSpec((1, 1, d, hid), lambda j, be, nu: (layer, be[j], 0, 0)),
                      pl.BlockSpec((1, 1, hid, d), lambda j, be, nu: (layer, be[j], 0, 0))],
            out_specs=pl.BlockSpec((EXPERT_ROWS, d), lambda j, be, nu: (j, 0))),
        out_shape=jax.ShapeDtypeStruct((p, d), F32),
        compiler_params=_params("arbitrary"),
        name="moe_experts",
    )(block_expert, n_used, xs, w_gate, w_up, w_down)


def _combine_kernel(dest_ref, x_ref, mod_ref, gates_ref, ys_ref, out_ref, buf1, buf2, sem):
    tm = x_ref.shape[0]
    bufs = (buf1, buf2)

    def copy(slot, t, k):
        return pltpu.make_async_copy(ys_ref.at[pl.ds(slot, 1)], bufs[k].at[pl.ds(t, 1)], sem)

    def start(t, carry):
        for k in range(MOE_TOP_K):
            copy(dest_ref[0, k, t], t, k).start(priority=k % 2)
        return carry

    lax.fori_loop(0, tm, start, 0, unroll=DMA_ISSUE_UNROLL)
    eye = (lax.broadcasted_iota(I32, (tm, tm), 0) == lax.broadcasted_iota(I32, (tm, tm), 1))
    gates = gates_ref[0]
    g1 = jnp.sum(jnp.where(eye, gates[0:1, :], 0.0), axis=1, keepdims=True)
    g2 = jnp.sum(jnp.where(eye, gates[1:2, :], 0.0), axis=1, keepdims=True)
    for k in range(MOE_TOP_K):
        pltpu.make_async_copy(ys_ref.at[pl.ds(0, tm)], bufs[k], sem).wait()
    ffn = g1 * buf1[...] + g2 * buf2[...]
    out_ref[...] = x_ref[...] + mod_ref[0][5:6] * ffn


def _combine(rows, x, mod_l, dest, gates, ys):
    t, d = x.shape
    nt, _, tm = dest.shape
    return pl.pallas_call(
        _combine_kernel,
        grid=(nt,),
        in_specs=[pl.BlockSpec((1, MOE_TOP_K, tm), lambda i: (i, 0, 0), memory_space=pltpu.SMEM),
                  _row_spec(d), pl.BlockSpec((1, N_MOD, d), lambda i: (rows.mod_row(i), 0, 0)),
                  pl.BlockSpec((1, 8, tm), lambda i: (i, 0, 0)),
                  pl.BlockSpec(memory_space=pl.ANY)],
        out_specs=_row_spec(d),
        out_shape=jax.ShapeDtypeStruct((t, d), F32),
        scratch_shapes=[pltpu.VMEM((tm, d), F32), pltpu.VMEM((tm, d), F32), pltpu.SemaphoreType.DMA(())],
        compiler_params=_params("arbitrary"),
        name="moe_combine",
    )(dest, x, mod_l, gates, ys)


def _moe(rows, x, mod_l, g, w_coarse, b_coarse, w_fine, b_fine, w_gate, w_up, w_down, layer):
    t, d = x.shape
    n_experts = w_fine.shape[1]
    w_t = jnp.zeros((ROUTER_LANES, d), F32).at[:n_experts].set(w_fine.T)
    w_t = w_t.at[n_experts:n_experts + MOE_GROUPS].set(w_coarse.T)
    bias = jnp.zeros((ROUTER_LANES, 1), F32).at[:n_experts, 0].set(b_fine)
    bias = bias.at[n_experts:n_experts + MOE_GROUPS, 0].set(b_coarse)
    ri, gates, counts = _router(rows, x, mod_l, g, w_t.astype(BF16), bias, n_experts)
    counts = counts[:, 0].astype(I32)
    padded = ((counts + EXPERT_ROWS - 1) // EXPERT_ROWS) * EXPERT_ROWS
    pad_end = jnp.cumsum(padded)
    pad_start = pad_end - padded
    nb = -(-(t * MOE_TOP_K) // EXPERT_ROWS) + n_experts
    block_row0 = jnp.arange(nb, dtype=I32) * EXPERT_ROWS
    block_expert = jnp.minimum(jnp.sum((pad_end[None, :] <= block_row0[:, None]).astype(I32), axis=1),
                               n_experts - 1)
    n_used = (pad_end[-1:] // EXPERT_ROWS).astype(I32)
    chosen = ri[:, 0:MOE_TOP_K, :, None] == jnp.arange(n_experts, dtype=I32)
    dest = jnp.sum(jnp.where(chosen, pad_start, 0), axis=-1) + ri[:, MOE_TOP_K:2 * MOE_TOP_K, :]
    xs = _dispatch(rows, x, mod_l, g, dest, nb * EXPERT_ROWS)
    ys = _experts(xs, block_expert, n_used, w_gate, w_up, w_down, layer)
    return _combine(rows, x, mod_l, dest, gates, ys)


def _final_kernel(x_ref, g_ref, o_ref):
    o_ref[...] = _rms(x_ref[...], g_ref[...])


def _final_norm(x, g, row0, n_rows):
    d = x.shape[1]
    blk0 = row0 // TOKEN_TILE
    return pl.pallas_call(
        _final_kernel,
        grid=(n_rows // TOKEN_TILE,),
        in_specs=[pl.BlockSpec((TOKEN_TILE, d), lambda i: (blk0 + i, 0)), _const_spec((1, d))],
        out_specs=_row_spec(d),
        out_shape=jax.ShapeDtypeStruct((n_rows, d), F32),
        compiler_params=_params("parallel"),
        name="final_norm",
    )(x, g.reshape(1, d))


def kernel(x_prompt, x_sample, cache_k, cache_v, state_ssm_fwd, state_ssm_bwd, c, c_ctx, w_mod, b_mod, norm_mix, norm_ffn, norm_final, w_in, conv_w, conv_b, dt_bias, a_log, d_skip, ssm_gain, lam_vec, subln_gain, w_out, pool_w, pool_b, pool_scale, w_coarse, b_coarse, w_fine, b_fine, w_gate, w_up, w_down):
    n_ctx, len_ctx, d = x_prompt.shape
    n_lat, len_lat, _ = x_sample.shape
    rows = _Rows(n_ctx, len_ctx, n_lat, len_lat)
    depth = w_mod.shape[0]
    d_ssm = ssm_gain.shape[1]
    heads = a_log.shape[2]
    da_heads = cache_k.shape[3]
    qk_cols = da_heads * 2 * DA_QK_DIM
    xbc_cols = conv_w.shape[2]
    past = cache_k.shape[2]

    x = jnp.concatenate([x_prompt.reshape(rows.t_ctx, d), x_sample.reshape(rows.t_lat, d)], axis=0)
    n_vec = -(-(1 + n_lat) // 8) * 8
    cvecs = jnp.zeros((n_vec, d), F32).at[0].set(c_ctx).at[1:1 + n_lat].set(c)
    mod = _modulation(cvecs, w_mod, b_mod).reshape(depth, n_vec, N_MOD, d)

    ks, vs, sf, sb = [], [], [], []
    for l in range(depth):
        j = l // 2
        if l % 2 == 0:
            lam_init = 0.8 - 0.6 * math.exp(-0.3 * l)
            splits = [d_ssm, d_ssm + xbc_cols, d_ssm + xbc_cols + 2 * heads]
            splits += [splits[-1] + qk_cols, splits[-1] + 2 * qk_cols]
            wz, wxbc, wdt, wq, wk, wv = [w.astype(BF16) for w in jnp.split(w_in[j], splits, axis=1)]
            zT, xbc, dt, dtT, qs, kr, vt, k_ctx, v_ctx = _inproj(rows, x, mod[l], norm_mix[l],
                                                                 (wz.T, wxbc, wdt, wdt.T, wq, wk, wv))
            xsT, bm, cm = _conv(rows, xbc, conv_w[j], conv_b[j], d_ssm)
            hf0 = state_ssm_fwd[:, j].reshape(n_lat, heads * SSM_HEAD_DIM, SSM_STATE)
            hb0 = state_ssm_bwd[:, j].reshape(n_lat, heads * SSM_HEAD_DIM, SSM_STATE)
            yfT, ybT, hf, hb = _ssd(rows, xsT, bm, cm, dt, dtT, dt_bias[j], a_log[j], hf0, hb0)
            o_ctx = _attention(qs, kr, vt, lam_vec[j], subln_gain[j], lam_init,
                               n_seq=n_ctx, length=len_ctx, row0=0)
            ck = cache_k[:, j].reshape(n_lat, past, qk_cols).astype(BF16)
            cvt = jnp.transpose(cache_v[:, j], (0, 2, 3, 1))
            cvt = jnp.concatenate([cvt, jnp.ones((n_lat, da_heads, V_ROWS - DA_V_DIM, past), F32)], axis=2)
            cvt = cvt.reshape(n_lat, da_heads * V_ROWS, past).astype(BF16)
            o_lat = _attention(qs, kr, vt, lam_vec[j], subln_gain[j], lam_init,
                               n_seq=n_lat, length=len_lat, row0=rows.t_ctx, cache_k=ck, cache_vt=cvt)
            dsk = jnp.repeat(d_skip[j], SSM_HEAD_DIM, axis=1)
            wo = w_out[j].astype(BF16)
            x = _outproj(rows, yfT, ybT, xsT, zT, o_ctx, o_lat, x, mod[l], dsk, ssm_gain[j],
                         wo[:d_ssm], wo[d_ssm:])
            ks.append(k_ctx.reshape(n_ctx, len_ctx, da_heads, 2, DA_QK_DIM))
            vs.append(v_ctx.reshape(n_ctx, len_ctx, da_heads, DA_V_DIM))
            sf.append(hf[:n_ctx].reshape(n_ctx, heads, SSM_HEAD_DIM, SSM_STATE))
            sb.append(hb[:n_ctx].reshape(n_ctx, heads, SSM_HEAD_DIM, SSM_STATE))
        else:
            x = _pool(rows, x, mod[l], norm_mix[l], pool_w[j].astype(BF16), pool_b[j], pool_scale[j])
        x = _moe(rows, x, mod[l], norm_ffn[l], w_coarse[l], b_coarse[l], w_fine[l], b_fine[l],
                 w_gate, w_up, w_down, l)

    y_prompt = _final_norm(x, norm_final, 0, rows.t_ctx).reshape(n_ctx, len_ctx, d)
    y_sample = _final_norm(x, norm_final, rows.t_ctx, rows.t_lat).reshape(n_lat, len_lat, d)
    return (y_prompt, y_sample, jnp.stack(ks, axis=1), jnp.stack(vs, axis=1),
            jnp.stack(sf, axis=1), jnp.stack(sb, axis=1))
```

```python
import functools
import math

import jax
import jax.numpy as jnp
from jax import lax
from jax.experimental import pallas as pl
from jax.experimental.pallas import tpu as pltpu

F32 = jnp.float32
BF16 = jnp.bfloat16
I32 = jnp.int32
U32 = jnp.uint32
HIGHEST = lax.Precision.HIGHEST

EPS = 1e-6
N_MOD = 6
SSM_HEAD_DIM = 64
SSM_GROUPS = 4
SSM_STATE = 128
CHUNK = 128
CONV_WIDTH = 5
CONV_HALO = 8
DA_QK_DIM = 64
DA_V_DIM = 128
GRID_W = 64
ROPE_FREQS = 16
ROPE_THETA = 10000.0
POOL_WINDOWS = (2, 4, 8, 16)
MOE_GROUPS = 4
EXPERTS_PER_GROUP = 8
MOE_TOP_K = 2

TOKEN_TILE = 256
MOE_TILE = 512
EXPERT_ROWS = 512
ATTN_KEY_BLOCK = 512
ATTN_QUERY_TILE = 1024
V_ROWS = DA_V_DIM + 16
LOG2_E = 1.4426950408889634
DMA_ISSUE_UNROLL = 8
ROUTER_LANES = 128
VMEM_LIMIT_BYTES = 56 * 1024 * 1024

NT_DIMS = (((1,), (1,)), ((), ()))
TN_DIMS = (((0,), (0,)), ((), ()))


def _params(*sem):
    return pltpu.CompilerParams(dimension_semantics=sem, vmem_limit_bytes=VMEM_LIMIT_BYTES)


def _silu(x):
    return x * jax.nn.sigmoid(x)


def _softplus(x):
    return jnp.maximum(x, 0.0) + jnp.log1p(jnp.exp(-jnp.abs(x)))


def _rms(x, g):
    return x * lax.rsqrt(jnp.mean(x * x, axis=-1, keepdims=True) + EPS) * g


def _modnorm(x, g, scale, shift):
    return _rms(x, g) * (1.0 + scale) + shift


def _bdot(a, b):
    return jnp.dot(a, b, preferred_element_type=F32)


class _Rows:
    def __init__(self, n_ctx, len_ctx, n_lat, len_lat):
        self.n_ctx, self.len_ctx, self.n_lat, self.len_lat = n_ctx, len_ctx, n_lat, len_lat
        self.t_ctx = n_ctx * len_ctx
        self.t_lat = n_lat * len_lat
        self.t = self.t_ctx + self.t_lat
        assert len_ctx % TOKEN_TILE == 0 and len_lat % TOKEN_TILE == 0
        self.ctx_tiles = self.t_ctx // TOKEN_TILE
        self.tiles = self.t // TOKEN_TILE
        self.tiles_per_ctx = len_ctx // TOKEN_TILE
        self.tiles_per_lat = len_lat // TOKEN_TILE

    def mod_row(self, i, tile=TOKEN_TILE):
        assert self.t_ctx % tile == 0 and self.len_lat % tile == 0
        ctx_tiles = self.t_ctx // tile
        return jnp.where(i < ctx_tiles, 0, 1 + (i - ctx_tiles) // (self.len_lat // tile))

    def seq_pos(self, i):
        is_ctx = i < self.ctx_tiles
        pos = jnp.where(is_ctx, i % self.tiles_per_ctx, (i - self.ctx_tiles) % self.tiles_per_lat)
        n = jnp.where(is_ctx, self.tiles_per_ctx, self.tiles_per_lat)
        return pos, n


def _row_spec(width, tile=TOKEN_TILE):
    return pl.BlockSpec((tile, width), lambda i: (i, 0))


def _const_spec(shape):
    nd = len(shape)
    return pl.BlockSpec(shape, lambda i: (0,) * nd)


def _mod_kernel(c_ref, w_ref, b_ref, o_ref):
    s = _silu(c_ref[...])
    o_ref[0] = _bdot(s.astype(BF16), w_ref[0].astype(BF16)) + b_ref[0]


def _modulation(cvecs, w_mod, b_mod):
    depth, d, n = w_mod.shape
    tn = n // 4
    return pl.pallas_call(
        _mod_kernel,
        grid=(depth, n // tn),
        in_specs=[pl.BlockSpec(cvecs.shape, lambda l, j: (0, 0)),
                  pl.BlockSpec((1, d, tn), lambda l, j: (l, 0, j)),
                  pl.BlockSpec((1, 1, tn), lambda l, j: (l, 0, j))],
        out_specs=pl.BlockSpec((1, cvecs.shape[0], tn), lambda l, j: (l, 0, j)),
        out_shape=jax.ShapeDtypeStruct((depth, cvecs.shape[0], n), F32),
        compiler_params=_params("parallel", "parallel"),
        name="modulation",
    )(cvecs, w_mod, b_mod.reshape(depth, 1, n))


def _inproj_kernel(x_ref, mod_ref, g_ref, cos_ref, sa_ref, sb_ref, wzT, wxbc, wdt, wdtT, wq, wk, wv,
                   zT_o, xbc_o, dt_o, dtT_o, qs_o, kr_o, vt_o, k_o, v_o, *, ctx_tiles):
    is_ctx = pl.program_id(0) < ctx_tiles
    m = mod_ref[0]
    h = _modnorm(x_ref[...], g_ref[...], m[1:2], m[0:1]).astype(BF16)
    zT_o[...] = lax.dot_general(wzT[...], h, NT_DIMS, preferred_element_type=F32)
    xbc_o[...] = _bdot(h, wxbc[...])
    dt_o[...] = _bdot(h, wdt[...])
    dtT_o[...] = lax.dot_general(wdtT[...], h, NT_DIMS, preferred_element_type=F32)
    q = _bdot(h, wq[...])
    k = _bdot(h, wk[...])
    v = _bdot(h, wv[...])

    @pl.when(is_ctx)
    def _():
        k_o[...] = k
        v_o[...] = v

    cos = jnp.where(is_ctx, 1.0, cos_ref[...])
    sa = jnp.where(is_ctx, 0.0, sa_ref[...])
    sb = jnp.where(is_ctx, 0.0, sb_ref[...])
    tile = q.shape[0]
    first_map = lax.broadcasted_iota(I32, (tile, DA_V_DIM), 1) < DA_QK_DIM
    scale = LOG2_E / math.sqrt(DA_QK_DIM)

    def rope(t):
        return (t * cos + pltpu.roll(t, DA_V_DIM - ROPE_FREQS, 1) * sa + pltpu.roll(t, ROPE_FREQS, 1) * sb)

    for hd in range(q.shape[1] // DA_V_DIM):
        sl = slice(hd * DA_V_DIM, (hd + 1) * DA_V_DIM)
        qr = rope(q[:, sl]) * scale
        qs_o[0, :, sl] = jnp.where(first_map, qr, 0.0).astype(BF16)
        qs_o[1, :, sl] = jnp.where(first_map, 0.0, qr).astype(BF16)
        kr_o[:, sl] = rope(k[:, sl]).astype(BF16)
        r0 = hd * V_ROWS
        vt_o[r0:r0 + DA_V_DIM, :] = v[:, sl].T.astype(BF16)
        vt_o[r0 + DA_V_DIM:r0 + V_ROWS, :] = jnp.ones((V_ROWS - DA_V_DIM, tile), BF16)


def _rope_tables(length):
    t = jnp.arange(length)
    pos = jnp.stack([t // GRID_W, t % GRID_W], axis=-1).astype(F32)
    inv = ROPE_THETA ** (-jnp.arange(ROPE_FREQS, dtype=F32) / ROPE_FREQS)
    ang = pos[..., None] * inv
    lane = jnp.arange(DA_V_DIM)
    axis = (lane % DA_QK_DIM) // (2 * ROPE_FREQS)
    freq = lane % ROPE_FREQS
    second_half = (lane % (2 * ROPE_FREQS)) >= ROPE_FREQS
    a = ang[:, axis, freq]
    cos, sin = jnp.cos(a), jnp.sin(a)
    return cos, jnp.where(second_half, 0.0, -sin), jnp.where(second_half, sin, 0.0)


def _inproj(rows, x, mod_l, g, weights):
    t, d = x.shape
    wzT, wxbc, wdt, wdtT, wq, wk, wv = weights
    w = wq.shape[1]
    vt_rows = wv.shape[1] // DA_V_DIM * V_ROWS
    tables = _rope_tables(rows.len_lat)

    def tab(i):
        return (jnp.maximum(i - rows.ctx_tiles, 0) % rows.tiles_per_lat, 0)

    def col_spec(n):
        return pl.BlockSpec((n, TOKEN_TILE), lambda i: (0, i))

    ctx_spec = pl.BlockSpec((TOKEN_TILE, w), lambda i: (jnp.minimum(i, rows.ctx_tiles - 1), 0))
    return pl.pallas_call(
        functools.partial(_inproj_kernel, ctx_tiles=rows.ctx_tiles),
        grid=(rows.tiles,),
        in_specs=[_row_spec(d),
                  pl.BlockSpec((1, N_MOD, d), lambda i: (rows.mod_row(i), 0, 0)),
                  _const_spec((1, d))] + [pl.BlockSpec((TOKEN_TILE, DA_V_DIM), tab)] * 3
                 + [_const_spec(wt.shape) for wt in weights],
        out_specs=[col_spec(wzT.shape[0]), _row_spec(wxbc.shape[1]), _row_spec(wdt.shape[1]),
                   col_spec(wdtT.shape[0]), pl.BlockSpec((2, TOKEN_TILE, w), lambda i: (0, i, 0)),
                   _row_spec(w), col_spec(vt_rows), ctx_spec, ctx_spec],
        out_shape=[jax.ShapeDtypeStruct((wzT.shape[0], t), F32), jax.ShapeDtypeStruct((t, wxbc.shape[1]), F32),
                   jax.ShapeDtypeStruct((t, wdt.shape[1]), F32), jax.ShapeDtypeStruct((wdtT.shape[0], t), F32),
                   jax.ShapeDtypeStruct((2, t, w), BF16), jax.ShapeDtypeStruct((t, w), BF16),
                   jax.ShapeDtypeStruct((vt_rows, t), BF16),
                   jax.ShapeDtypeStruct((rows.t_ctx, w), F32), jax.ShapeDtypeStruct((rows.t_ctx, w), F32)],
        compiler_params=_params("arbitrary"),
        name="inproj",
    )(x, mod_l, g.reshape(1, d), *tables, *weights)


def _conv_kernel(cur_ref, prev_ref, next_ref, w_ref, b_ref, xsT_o, b_o, c_o, ext_ref, *, rows, d_ssm, gn):
    i = pl.program_id(0)
    pos, n = rows.seq_pos(i)
    tile = cur_ref.shape[0]
    ext_ref[0:CONV_HALO, :] = jnp.where(pos == 0, 0.0, prev_ref[...])
    ext_ref[CONV_HALO:CONV_HALO + tile, :] = cur_ref[...]
    ext_ref[CONV_HALO + tile:, :] = jnp.where(pos == n - 1, 0.0, next_ref[...])
    width = cur_ref.shape[1]
    step = 512
    for c0 in range(0, width, step):
        acc = jnp.broadcast_to(b_ref[:, c0:c0 + step], (tile, step))
        for k in range(CONV_WIDTH):
            off = CONV_HALO - CONV_WIDTH // 2 + k
            acc = acc + w_ref[k:k + 1, c0:c0 + step] * ext_ref[off:off + tile, c0:c0 + step]
        y = _silu(acc)
        if c0 < d_ssm:
            xsT_o[c0:c0 + step, :] = y.T
        elif c0 < d_ssm + gn:
            b_o[:, c0 - d_ssm:c0 - d_ssm + step] = y.astype(BF16)
        else:
            c_o[:, c0 - d_ssm - gn:c0 - d_ssm - gn + step] = y.astype(BF16)


def _conv(rows, xbc, conv_w, conv_b, d_ssm):
    t, width = xbc.shape
    gn = (width - d_ssm) // 2
    hb = TOKEN_TILE // CONV_HALO
    last = t // CONV_HALO - 1
    return pl.pallas_call(
        functools.partial(_conv_kernel, rows=rows, d_ssm=d_ssm, gn=gn),
        grid=(rows.tiles,),
        in_specs=[_row_spec(width),
                  pl.BlockSpec((CONV_HALO, width), lambda i: (jnp.maximum(i * hb - 1, 0), 0)),
                  pl.BlockSpec((CONV_HALO, width), lambda i: (jnp.minimum((i + 1) * hb, last), 0)),
                  _const_spec(conv_w.shape), _const_spec((1, width))],
        out_specs=[pl.BlockSpec((d_ssm, TOKEN_TILE), lambda i: (0, i)), _row_spec(gn), _row_spec(gn)],
        out_shape=[jax.ShapeDtypeStruct((d_ssm, t), F32),
                   jax.ShapeDtypeStruct((t, gn), BF16),
                   jax.ShapeDtypeStruct((t, gn), BF16)],
        scratch_shapes=[pltpu.VMEM((TOKEN_TILE + 2 * CONV_HALO, width), F32)],
        compiler_params=_params("parallel"),
        name="conv",
    )(xbc, xbc, xbc, conv_w, conv_b.reshape(1, width))


def _contributes(q, reverse, target_rows):
    r = lax.broadcasted_iota(I32, (q, q), 0)
    c = lax.broadcasted_iota(I32, (q, q), 1)
    tgt, src = (r, c) if target_rows else (c, r)
    return (src >= tgt) if reverse else (src <= tgt)


def _ssd_direction(xT_ref, b_ref, c_ref, dt_c, dt_r, bias_row, bias_col, a_row, a_col,
                   h_refs, yT_ref, reverse):
    q = CHUNK
    heads = a_row.shape[1]
    dt_c = _softplus(dt_c + bias_row)
    dt_r = _softplus(dt_r + bias_col)
    keep_t = _contributes(q, reverse, False)
    acs_c = jnp.dot(_contributes(q, reverse, True).astype(F32), dt_c * a_row,
                    precision=HIGHEST, preferred_element_type=F32)
    acs_r = jnp.dot(dt_r * a_col, keep_t.astype(F32), precision=HIGHEST, preferred_element_type=F32)
    last = 0 if reverse else q - 1
    total_r = acs_r[:, last:last + 1]
    grow_r = jnp.exp(acs_r)
    to_end_r = jnp.exp(total_r - acs_r)
    carry_r = jnp.exp(total_r)
    per_group = heads // SSM_GROUPS
    p = SSM_HEAD_DIM
    for g in range(SSM_GROUPS):
        bg = b_ref[:, g * SSM_STATE:(g + 1) * SSM_STATE]
        cg = c_ref[:, g * SSM_STATE:(g + 1) * SSM_STATE]
        cb_t = lax.dot_general(bg, cg, NT_DIMS, preferred_element_type=F32)
        h_in = h_refs[g][...]
        y_off = lax.dot_general(h_in.astype(BF16), cg, NT_DIMS, preferred_element_type=F32)
        xw, h_decayed = [], []
        for r in range(per_group):
            h = g * per_group + r
            rs = slice(r * p, (r + 1) * p)
            seg = acs_r[h:h + 1, :] - acs_c[:, h:h + 1]
            decay = jnp.exp(jnp.where(keep_t, seg, -jnp.inf))
            l_t = (cb_t * decay).astype(BF16)
            xdt = xT_ref[h * p:(h + 1) * p, :] * dt_r[h:h + 1, :]
            yT_ref[h * p:(h + 1) * p, :] = _bdot(xdt.astype(BF16), l_t) + y_off[rs, :] * grow_r[h:h + 1, :]
            xw.append((xdt * to_end_r[h:h + 1, :]).astype(BF16))
            h_decayed.append(h_in[rs, :] * carry_r[h:h + 1, :])
        state = _bdot(jnp.concatenate(xw, axis=0), bg)
        h_refs[g][...] = jnp.concatenate(h_decayed, axis=0) + state


def _ssd_kernel(xf, bf, cf, dtf, dtTf, xb, bb, cb, dtb, dtTb, bias_ref, biasT_ref, alog_ref, alogT_ref,
                hf0, hb0, yf_o, yb_o, hf_o, hb_o, *scratch, geom):
    s = pl.program_id(0)
    is_ctx, _, c, nc, _ = geom(s)
    heads = alog_ref.shape[1]
    hf_scr, hb_scr = scratch[:SSM_GROUPS], scratch[SSM_GROUPS:]
    rows_g = hf_scr[0].shape[0]

    @pl.when(c == 0)
    def _():
        for g in range(SSM_GROUPS):
            hf_scr[g][...] = jnp.where(is_ctx, 0.0, hf0[0, g * rows_g:(g + 1) * rows_g, :])
            hb_scr[g][...] = jnp.where(is_ctx, 0.0, hb0[0, g * rows_g:(g + 1) * rows_g, :])

    a_row = -jnp.exp(alog_ref[...])
    a_col = -jnp.exp(alogT_ref[...])
    _ssd_direction(xf, bf, cf, dtf[:, 0:heads], dtTf[0:heads, :], bias_ref[0:1, :], biasT_ref[:, 0:1],
                   a_row[0:1, :], a_col[:, 0:1], hf_scr, yf_o, False)
    _ssd_direction(xb, bb, cb, dtb[:, heads:2 * heads], dtTb[heads:2 * heads, :], bias_ref[1:2, :],
                   biasT_ref[:, 1:2], a_row[1:2, :], a_col[:, 1:2], hb_scr, yb_o, True)

    @pl.when(c == nc - 1)
    def _():
        for g in range(SSM_GROUPS):
            hf_o[0, g * rows_g:(g + 1) * rows_g, :] = hf_scr[g][...]
            hb_o[0, g * rows_g:(g + 1) * rows_g, :] = hb_scr[g][...]


def _ssd(rows, xsT, bm, cm, dt, dtT, dt_bias, a_log, hf0, hb0):
    d_ssm, t = xsT.shape
    gn = bm.shape[1]
    heads2 = dt.shape[1]
    ncc, ncl = rows.len_ctx // CHUNK, rows.len_lat // CHUNK
    s_ctx, s_lat = rows.n_ctx * ncc, rows.n_lat * ncl
    n_seq = rows.n_ctx + rows.n_lat

    def geom(s):
        is_ctx = s < s_ctx
        sl = s - s_ctx
        seq = jnp.where(is_ctx, s // ncc, rows.n_ctx + sl // ncl)
        c = jnp.where(is_ctx, s % ncc, sl % ncl)
        nc = jnp.where(is_ctx, ncc, ncl)
        base = jnp.where(is_ctx, (s // ncc) * ncc, s_ctx + (sl // ncl) * ncl)
        return is_ctx, seq, c, nc, base

    def fwd(s):
        _, _, c, _, base = geom(s)
        return base + c

    def bwd(s):
        _, _, c, nc, base = geom(s)
        return base + nc - 1 - c

    def lat_seq(s):
        return jnp.maximum(geom(s)[1] - rows.n_ctx, 0)

    def side(blk):
        return [pl.BlockSpec((d_ssm, CHUNK), lambda s: (0, blk(s))),
                pl.BlockSpec((CHUNK, gn), lambda s: (blk(s), 0)),
                pl.BlockSpec((CHUNK, gn), lambda s: (blk(s), 0)),
                pl.BlockSpec((CHUNK, heads2), lambda s: (blk(s), 0)),
                pl.BlockSpec((heads2, CHUNK), lambda s: (0, blk(s)))]

    state_shape = (SSM_HEAD_DIM * (heads2 // 2), SSM_STATE)
    group_shape = (state_shape[0] // SSM_GROUPS, SSM_STATE)
    state_spec_in = pl.BlockSpec((1,) + state_shape, lambda s: (lat_seq(s), 0, 0))
    state_spec_out = pl.BlockSpec((1,) + state_shape, lambda s: (geom(s)[1], 0, 0))
    return pl.pallas_call(
        functools.partial(_ssd_kernel, geom=geom),
        grid=(s_ctx + s_lat,),
        in_specs=side(fwd) + side(bwd) + [_const_spec(dt_bias.shape), _const_spec(dt_bias.T.shape),
                                          _const_spec(a_log.shape), _const_spec(a_log.T.shape),
                                          state_spec_in, state_spec_in],
        out_specs=[pl.BlockSpec((d_ssm, CHUNK), lambda s: (0, fwd(s))),
                   pl.BlockSpec((d_ssm, CHUNK), lambda s: (0, bwd(s))),
                   state_spec_out, state_spec_out],
        out_shape=[jax.ShapeDtypeStruct((d_ssm, t), F32), jax.ShapeDtypeStruct((d_ssm, t), F32),
                   jax.ShapeDtypeStruct((n_seq,) + state_shape, F32),
                   jax.ShapeDtypeStruct((n_seq,) + state_shape, F32)],
        scratch_shapes=[pltpu.VMEM(group_shape, F32)] * (2 * SSM_GROUPS),
        compiler_params=_params("arbitrary"),
        name="ssd",
    )(xsT, bm, cm, dt, dtT, xsT, bm, cm, dt, dtT, dt_bias, dt_bias.T, a_log, a_log.T, hf0, hb0)


def _attn_kernel(*refs, tq, tk, has_cache, lam_init):
    if has_cache:
        qs_ref, kc_ref, vct_ref, k_ref, vt_ref, lamv_ref, gain_ref, o_ref = refs
    else:
        qs_ref, k_ref, vt_ref, lamv_ref, gain_ref, o_ref = refs
    q = qs_ref[...].reshape(2 * tq, DA_V_DIM)
    blocks = []
    if has_cache:
        blocks += [(kc_ref, vct_ref, c, True) for c in range(kc_ref.shape[1] // tk)]
    blocks += [(k_ref, vt_ref, c, False) for c in range(k_ref.shape[0] // tk)]
    m = jnp.full((1, 2 * tq), -jnp.inf, F32)
    acc = jnp.zeros((V_ROWS, 2 * tq), F32)
    def scores(blk):
        kref, _, c, cached = blk
        ks = slice(c * tk, (c + 1) * tk)
        kb = kref[0, ks, :] if cached else kref[ks, :]
        return lax.dot_general(kb, q, NT_DIMS, preferred_element_type=F32)

    s_next = scores(blocks[0])
    for n, (_, vref, c, cached) in enumerate(blocks):
        ks = slice(c * tk, (c + 1) * tk)
        vb = vref[0, :, ks] if cached else vref[:, ks]
        s = s_next
        if n + 1 < len(blocks):
            s_next = scores(blocks[n + 1])
        m_new = jnp.maximum(m, jnp.max(s, axis=0, keepdims=True))
        acc = jnp.exp2(m - m_new) * acc + _bdot(vb, jnp.exp2(s - m_new).astype(BF16))
        m = m_new
    lv = lamv_ref[...]
    lam = (jnp.exp(jnp.sum(lv[0:1] * lv[1:2], axis=1, keepdims=True))
           - jnp.exp(jnp.sum(lv[2:3] * lv[3:4], axis=1, keepdims=True)) + lam_init)
    on = acc[0:DA_V_DIM, :] / acc[DA_V_DIM:DA_V_DIM + 1, :]
    ot = on[:, 0:tq] - lam * on[:, tq:2 * tq]
    ot = ot * lax.rsqrt(jnp.mean(ot * ot, axis=0, keepdims=True) + EPS) * gain_ref[...] * (1.0 - lam_init)
    o_ref[...] = ot.T.astype(BF16)


def _attention(qs, kr, vt, lam_vec, gain, lam_init, *, n_seq, length, row0, cache_k=None, cache_vt=None):
    w = kr.shape[1]
    heads = w // DA_V_DIM
    tq = min(length, ATTN_QUERY_TILE)
    tk = min(length, ATTN_KEY_BLOCK)
    assert length % tq == 0 and length % tk == 0 and row0 % length == 0
    nq = length // tq
    has_cache = cache_k is not None
    q_blk0, seq0 = row0 // tq, row0 // length

    in_specs = [pl.BlockSpec((2, tq, DA_V_DIM), lambda b, h, i: (0, q_blk0 + b * nq + i, h))]
    args = [qs]
    if has_cache:
        past = cache_k.shape[1]
        assert past % tk == 0
        in_specs += [pl.BlockSpec((1, past, DA_V_DIM), lambda b, h, i: (b, 0, h)),
                     pl.BlockSpec((1, V_ROWS, past), lambda b, h, i: (b, h, 0))]
        args += [cache_k, cache_vt]
    in_specs += [pl.BlockSpec((length, DA_V_DIM), lambda b, h, i: (seq0 + b, h)),
                 pl.BlockSpec((V_ROWS, length), lambda b, h, i: (h, seq0 + b)),
                 pl.BlockSpec(lam_vec.shape, lambda b, h, i: (0, 0)),
                 pl.BlockSpec((DA_V_DIM, 1), lambda b, h, i: (0, 0))]
    args += [kr, vt, lam_vec, gain.reshape(DA_V_DIM, 1)]
    return pl.pallas_call(
        functools.partial(_attn_kernel, tq=tq, tk=tk, has_cache=has_cache, lam_init=lam_init),
        grid=(n_seq, heads, nq),
        in_specs=in_specs,
        out_specs=pl.BlockSpec((tq, DA_V_DIM), lambda b, h, i: (b * nq + i, h)),
        out_shape=jax.ShapeDtypeStruct((n_seq * length, w), BF16),
        compiler_params=_params("parallel", "parallel", "arbitrary"),
        name="attention_lat" if has_cache else "attention_ctx",
    )(*args)


def _outproj_kernel(yfT_ref, ybT_ref, xsT_ref, zT_ref, oc_ref, ol_ref, x_ref, mod_ref, dsk_ref, gain_ref,
                    wy_ref, wo_ref, out_ref, *, ctx_tiles):
    is_ctx = pl.program_id(0) < ctx_tiles
    y = yfT_ref[...] + ybT_ref[...] + (dsk_ref[0] + dsk_ref[1]) * xsT_ref[...]
    y = y * _silu(zT_ref[...])
    y = (y * lax.rsqrt(jnp.mean(y * y, axis=0, keepdims=True) + EPS) * gain_ref[...]).astype(BF16)
    o = jnp.where(is_ctx, oc_ref[...], ol_ref[...])
    mixed = lax.dot_general(y, wy_ref[...], TN_DIMS, preferred_element_type=F32) + _bdot(o, wo_ref[...])
    out_ref[...] = x_ref[...] + mod_ref[0][2:3] * mixed


def _outproj(rows, yfT, ybT, xsT, zT, o_ctx, o_lat, x, mod_l, d_skip, gain, wy, wo):
    t, d = x.shape
    d_ssm = xsT.shape[0]
    wa = o_ctx.shape[1]
    col_spec = pl.BlockSpec((d_ssm, TOKEN_TILE), lambda i: (0, i))
    return pl.pallas_call(
        functools.partial(_outproj_kernel, ctx_tiles=rows.ctx_tiles),
        grid=(rows.tiles,),
        in_specs=[col_spec] * 4 + [
            pl.BlockSpec((TOKEN_TILE, wa), lambda i: (jnp.minimum(i, rows.ctx_tiles - 1), 0)),
            pl.BlockSpec((TOKEN_TILE, wa), lambda i: (jnp.maximum(i - rows.ctx_tiles, 0), 0)),
            _row_spec(d),
            pl.BlockSpec((1, N_MOD, d), lambda i: (rows.mod_row(i), 0, 0)),
            _const_spec((2, d_ssm, 1)), _const_spec((d_ssm, 1)), _const_spec(wy.shape), _const_spec(wo.shape)],
        out_specs=_row_spec(d),
        out_shape=jax.ShapeDtypeStruct((t, d), F32),
        compiler_params=_params("parallel"),
        name="outproj",
    )(yfT, ybT, xsT, zT, o_ctx, o_lat, x, mod_l, d_skip.reshape(2, d_ssm, 1), gain.reshape(d_ssm, 1), wy, wo)


def _pool_kernel(cur_ref, prev_ref, next_ref, mod_ref, g_ref, w_ref, b_ref, sc_ref, out_ref, ext_ref,
                 *, rows, len_ctx, len_lat):
    i = pl.program_id(0)
    pos, n = rows.seq_pos(i)
    tile = cur_ref.shape[0]
    m = mod_ref[0]
    g = g_ref[...]

    def norm(x):
        return _modnorm(x, g, m[1:2], m[0:1])

    x = cur_ref[...]
    ext_ref[0:CONV_HALO, :] = jnp.where(pos == 0, 0.0, norm(prev_ref[...]))
    ext_ref[CONV_HALO:CONV_HALO + tile, :] = norm(x)
    ext_ref[CONV_HALO + tile:, :] = jnp.where(pos == n - 1, 0.0, norm(next_ref[...]))
    length = jnp.where(i < rows.ctx_tiles, len_ctx, len_lat)
    tpos = pos * tile + lax.broadcasted_iota(I32, (tile, 1), 0)
    pg = cur_ref.shape[1] // len(POOL_WINDOWS)
    for gi, win in enumerate(POOL_WINDOWS):
        cs = slice(gi * pg, (gi + 1) * pg)
        half = win // 2
        acc = ext_ref[CONV_HALO - half:CONV_HALO - half + tile, cs]
        for off in range(1 - half, half):
            acc = acc + ext_ref[CONV_HALO + off:CONV_HALO + off + tile, cs]
        cnt = jnp.minimum(tpos + half, length) - jnp.maximum(tpos - half, 0)
        pooled = acc / cnt.astype(F32) - ext_ref[CONV_HALO:CONV_HALO + tile, cs]
        mixed = (_bdot(pooled.astype(BF16), w_ref[gi]) + b_ref[gi:gi + 1, :]) * sc_ref[:, cs]
        out_ref[:, cs] = x[:, cs] + m[2:3, cs] * mixed


def _pool(rows, x, mod_l, g, pool_w, pool_b, pool_scale):
    t, d = x.shape
    hb = TOKEN_TILE // CONV_HALO
    last = t // CONV_HALO - 1
    return pl.pallas_call(
        functools.partial(_pool_kernel, rows=rows, len_ctx=rows.len_ctx, len_lat=rows.len_lat),
        grid=(rows.tiles,),
        in_specs=[_row_spec(d),
                  pl.BlockSpec((CONV_HALO, d), lambda i: (jnp.maximum(i * hb - 1, 0), 0)),
                  pl.BlockSpec((CONV_HALO, d), lambda i: (jnp.minimum((i + 1) * hb, last), 0)),
                  pl.BlockSpec((1, N_MOD, d), lambda i: (rows.mod_row(i), 0, 0)),
                  _const_spec((1, d)), _const_spec(pool_w.shape), _const_spec(pool_b.shape),
                  _const_spec((1, d))],
        out_specs=_row_spec(d),
        out_shape=jax.ShapeDtypeStruct((t, d), F32),
        scratch_shapes=[pltpu.VMEM((TOKEN_TILE + 2 * CONV_HALO, d), F32)],
        compiler_params=_params("parallel"),
        name="pool",
    )(x, x, x, mod_l, g.reshape(1, d), pool_w, pool_b, pool_scale.reshape(1, d))


def _router_kernel(x_ref, mod_ref, g_ref, w_ref, bias_ref, ri_o, rf_o, cnt_o, cnt_scr,
                   *, n_experts):
    i = pl.program_id(0)

    @pl.when(i == 0)
    def _():
        cnt_scr[...] = jnp.zeros(cnt_scr.shape, F32)

    m = mod_ref[0]
    h = _modnorm(x_ref[...], g_ref[...], m[4:5], m[3:4])
    tm = h.shape[0]
    lg = lax.dot_general(w_ref[...], h.astype(BF16), NT_DIMS, preferred_element_type=F32) + bias_ref[...]
    row = lax.broadcasted_iota(I32, (n_experts, tm), 0)
    big = jnp.int32(n_experts + MOE_GROUPS)
    lc = lg[n_experts:n_experts + 8, :]
    crow = lax.broadcasted_iota(I32, (8, tm), 0)
    lc = jnp.where(crow < MOE_GROUPS, lc, -jnp.inf)
    mc = jnp.max(lc, axis=0, keepdims=True)
    group_p = 1.0 / jnp.sum(jnp.exp(lc - mc), axis=0, keepdims=True)
    g_idx = jnp.min(jnp.where(lc == mc, crow, big), axis=0, keepdims=True)
    lf = jnp.where(row // EXPERTS_PER_GROUP == g_idx, lg[0:n_experts, :], -jnp.inf)
    m1 = jnp.max(lf, axis=0, keepdims=True)
    e1 = jnp.min(jnp.where(lf == m1, row, big), axis=0, keepdims=True)
    lf2 = jnp.where(row == e1, -jnp.inf, lf)
    m2 = jnp.max(lf2, axis=0, keepdims=True)
    e2 = jnp.min(jnp.where(lf2 == m2, row, big), axis=0, keepdims=True)
    r2 = jnp.exp(m2 - m1)
    gate1 = group_p / (1.0 + r2)
    gate2 = group_p * r2 / (1.0 + r2)
    oh1 = (row == e1)
    oh2 = (row == e2)
    oh = jnp.where(oh1 | oh2, 1.0, 0.0)
    before = (lax.broadcasted_iota(I32, (tm, tm), 0) < lax.broadcasted_iota(I32, (tm, tm), 1))
    prior = _bdot(oh.astype(BF16), before.astype(BF16)) + cnt_scr[...]
    rank1 = jnp.sum(jnp.where(oh1, prior, 0.0), axis=0, keepdims=True)
    rank2 = jnp.sum(jnp.where(oh2, prior, 0.0), axis=0, keepdims=True)
    cnt_scr[...] = cnt_scr[...] + jnp.sum(oh, axis=1, keepdims=True)
    zi = jnp.zeros((4, tm), I32)
    ri_o[0] = jnp.concatenate([e1, e2, rank1.astype(I32), rank2.astype(I32), zi], axis=0)
    rf_o[0] = jnp.concatenate([gate1, gate2, jnp.zeros((6, tm), F32)], axis=0)
    cnt_o[...] = jnp.broadcast_to(cnt_scr[...], cnt_o.shape)


def _router(rows, x, mod_l, g, w_t, bias, n_experts):
    t, d = x.shape
    tm = MOE_TILE
    nt = t // tm
    return pl.pallas_call(
        functools.partial(_router_kernel, n_experts=n_experts),
        grid=(nt,),
        in_specs=[_row_spec(d, tm), pl.BlockSpec((1, N_MOD, d), lambda i: (rows.mod_row(i, tm), 0, 0)),
                  _const_spec((1, d)), _const_spec(w_t.shape), _const_spec(bias.shape)],
        out_specs=[pl.BlockSpec((1, 8, tm), lambda i: (i, 0, 0)),
                   pl.BlockSpec((1, 8, tm), lambda i: (i, 0, 0)),
                   _const_spec((n_experts, ROUTER_LANES))],
        out_shape=[jax.ShapeDtypeStruct((nt, 8, tm), I32),
                   jax.ShapeDtypeStruct((nt, 8, tm), F32),
                   jax.ShapeDtypeStruct((n_experts, ROUTER_LANES), F32)],
        scratch_shapes=[pltpu.VMEM((n_experts, 1), F32)],
        compiler_params=_params("arbitrary"),
        name="router",
    )(x, mod_l, g.reshape(1, d), w_t, bias)


def _pack_bf16_pairs(x):
    n = x.shape[1] // 2
    hi = lax.bitcast_convert_type(x[:, :n].astype(BF16).astype(F32), U32)
    lo = lax.bitcast_convert_type(x[:, n:].astype(BF16).astype(F32), U32)
    return (hi & jnp.uint32(0xFFFF0000)) | (lo >> 16)


def _unpack_bf16_pairs(w):
    hi = lax.bitcast_convert_type(w & jnp.uint32(0xFFFF0000), F32)
    lo = lax.bitcast_convert_type(w << 16, F32)
    return hi.astype(BF16), lo.astype(BF16)


def _dispatch_kernel(dest_ref, x_ref, mod_ref, g_ref, xs_in_ref, xs_ref, h_scr, sem):
    del xs_in_ref
    tm = x_ref.shape[0]
    m = mod_ref[0]
    h_scr[...] = _pack_bf16_pairs(_modnorm(x_ref[...], g_ref[...], m[4:5], m[3:4]))

    def copy(t, slot):
        return pltpu.make_async_copy(h_scr.at[pl.ds(t, 1)], xs_ref.at[pl.ds(slot, 1)], sem)

    def start(t, carry):
        for k in range(MOE_TOP_K):
            copy(t, dest_ref[0, k, t]).start()
        return carry

    lax.fori_loop(0, tm, start, 0, unroll=DMA_ISSUE_UNROLL)
    for k in range(MOE_TOP_K):
        pltpu.make_async_copy(h_scr, xs_ref.at[pl.ds(0, tm)], sem).wait()


def _dispatch(rows, x, mod_l, g, dest, n_slots):
    t, d = x.shape
    nt, _, tm = dest.shape
    xs0 = jnp.zeros((n_slots, d // 2), U32)
    return pl.pallas_call(
        _dispatch_kernel,
        grid=(nt,),
        in_specs=[pl.BlockSpec((1, MOE_TOP_K, tm), lambda i: (i, 0, 0), memory_space=pltpu.SMEM),
                  _row_spec(d, tm), pl.BlockSpec((1, N_MOD, d), lambda i: (rows.mod_row(i, tm), 0, 0)),
                  _const_spec((1, d)), pl.BlockSpec(memory_space=pl.ANY)],
        out_specs=pl.BlockSpec(memory_space=pl.ANY),
        out_shape=jax.ShapeDtypeStruct((n_slots, d // 2), U32),
        scratch_shapes=[pltpu.VMEM((tm, d // 2), U32), pltpu.SemaphoreType.DMA(())],
        input_output_aliases={4: 0},
        compiler_params=_params("arbitrary"),
        name="moe_dispatch",
    )(dest, x, mod_l, g.reshape(1, d), xs0)


def _expert_kernel(be_ref, nu_ref, x_ref, wg_ref, wu_ref, wd_ref, y_ref):
    j = pl.program_id(0)

    @pl.when(j < nu_ref[0])
    def _():
        x_lo, x_hi = _unpack_bf16_pairs(x_ref[...])
        half = x_lo.shape[1]

        def proj(w_ref):
            return _bdot(x_lo, w_ref[0, 0, :half, :].astype(BF16)) + _bdot(x_hi, w_ref[0, 0, half:, :].astype(BF16))

        a = proj(wg_ref)
        u = proj(wu_ref)
        y_ref[...] = _bdot((_silu(a) * u).astype(BF16), wd_ref[0, 0].astype(BF16))

    @pl.when(j >= nu_ref[0])
    def _():
        y_ref[...] = jnp.zeros(y_ref.shape, F32)


def _experts(xs, block_expert, n_used, w_gate, w_up, w_down, layer):
    p = xs.shape[0]
    d, hid = w_gate.shape[2:]
    nb = p // EXPERT_ROWS
    return pl.pallas_call(
        _expert_kernel,
        grid_spec=pltpu.PrefetchScalarGridSpec(
            num_scalar_prefetch=2,
            grid=(nb,),
            in_specs=[pl.BlockSpec((EXPERT_ROWS, xs.shape[1]), lambda j, be, nu: (j, 0)),
                      pl.BlockSpec((1, 1, d, hid), lambda j, be, nu: (layer, be[j], 0, 0)),
                      pl.BlockSpec((1, 1, d, hid), lambda j, be, nu: (layer, be[j], 0, 0)),
                      pl.BlockSpec((1, 1, hid, d), lambda j, be, nu: (layer, be[j], 0, 0))],
            out_specs=pl.BlockSpec((EXPERT_ROWS, d), lambda j, be, nu: (j, 0))),
        out_shape=jax.ShapeDtypeStruct((p, d), F32),
        compiler_params=_params("arbitrary"),
        name="moe_experts",
    )(block_expert, n_used, xs, w_gate, w_up, w_down)


def _combine_kernel(dest_ref, x_ref, mod_ref, gates_ref, ys_ref, out_ref, buf1, buf2, sem):
    tm = x_ref.shape[0]
    bufs = (buf1, buf2)

    def copy(slot, t, k):
        return pltpu.make_async_copy(ys_ref.at[pl.ds(slot, 1)], bufs[k].at[pl.ds(t, 1)], sem)

    def start(t, carry):
        for k in range(MOE_TOP_K):
            copy(dest_ref[0, k, t], t, k).start()
        return carry

    lax.fori_loop(0, tm, start, 0, unroll=DMA_ISSUE_UNROLL)
    eye = (lax.broadcasted_iota(I32, (tm, tm), 0) == lax.broadcasted_iota(I32, (tm, tm), 1))
    gates = gates_ref[0]
    g1 = jnp.sum(jnp.where(eye, gates[0:1, :], 0.0), axis=1, keepdims=True)
    g2 = jnp.sum(jnp.where(eye, gates[1:2, :], 0.0), axis=1, keepdims=True)
    for k in range(MOE_TOP_K):
        pltpu.make_async_copy(ys_ref.at[pl.ds(0, tm)], bufs[k], sem).wait()
    ffn = g1 * buf1[...] + g2 * buf2[...]
    out_ref[...] = x_ref[...] + mod_ref[0][5:6] * ffn


def _combine(rows, x, mod_l, dest, gates, ys):
    t, d = x.shape
    nt, _, tm = dest.shape
    return pl.pallas_call(
        _combine_kernel,
        grid=(nt,),
        in_specs=[pl.BlockSpec((1, MOE_TOP_K, tm), lambda i: (i, 0, 0), memory_space=pltpu.SMEM),
                  _row_spec(d, tm), pl.BlockSpec((1, N_MOD, d), lambda i: (rows.mod_row(i, tm), 0, 0)),
                  pl.BlockSpec((1, 8, tm), lambda i: (i, 0, 0)),
                  pl.BlockSpec(memory_space=pl.ANY)],
        out_specs=_row_spec(d, tm),
        out_shape=jax.ShapeDtypeStruct((t, d), F32),
        scratch_shapes=[pltpu.VMEM((tm, d), F32), pltpu.VMEM((tm, d), F32), pltpu.SemaphoreType.DMA(())],
        compiler_params=_params("arbitrary"),
        name="moe_combine",
    )(dest, x, mod_l, gates, ys)


def _moe(rows, x, mod_l, g, w_coarse, b_coarse, w_fine, b_fine, w_gate, w_up, w_down, layer):
    t, d = x.shape
    n_experts = w_fine.shape[1]
    n_pad = ROUTER_LANES - n_experts - MOE_GROUPS
    w_t = jnp.concatenate([w_fine.T, w_coarse.T, jnp.zeros((n_pad, d), F32)], axis=0)
    bias = jnp.concatenate([b_fine, b_coarse, jnp.zeros((n_pad,), F32)]).reshape(ROUTER_LANES, 1)
    ri, gates, counts = _router(rows, x, mod_l, g, w_t.astype(BF16), bias, n_experts)
    counts = counts[:, 0].astype(I32)
    padded = ((counts + EXPERT_ROWS - 1) // EXPERT_ROWS) * EXPERT_ROWS
    pad_end = jnp.cumsum(padded)
    pad_start = pad_end - padded
    nb = -(-(t * MOE_TOP_K) // EXPERT_ROWS) + n_experts
    block_row0 = jnp.arange(nb, dtype=I32) * EXPERT_ROWS
    block_expert = jnp.minimum(jnp.sum((pad_end[None, :] <= block_row0[:, None]).astype(I32), axis=1),
                               n_experts - 1)
    n_used = (pad_end[-1:] // EXPERT_ROWS).astype(I32)
    chosen = ri[:, 0:MOE_TOP_K, :, None] == jnp.arange(n_experts, dtype=I32)
    dest = jnp.sum(jnp.where(chosen, pad_start, 0), axis=-1) + ri[:, MOE_TOP_K:2 * MOE_TOP_K, :]
    xs = _dispatch(rows, x, mod_l, g, dest, nb * EXPERT_ROWS)
    ys = _experts(xs, block_expert, n_used, w_gate, w_up, w_down, layer)
    return _combine(rows, x, mod_l, dest, gates, ys)


def _final_kernel(x_ref, g_ref, o_ref):
    o_ref[...] = _rms(x_ref[...], g_ref[...])


def _final_norm(x, g, row0, n_rows):
    d = x.shape[1]
    blk0 = row0 // TOKEN_TILE
    return pl.pallas_call(
        _final_kernel,
        grid=(n_rows // TOKEN_TILE,),
        in_specs=[pl.BlockSpec((TOKEN_TILE, d), lambda i: (blk0 + i, 0)), _const_spec((1, d))],
        out_specs=_row_spec(d),
        out_shape=jax.ShapeDtypeStruct((n_rows, d), F32),
        compiler_params=_params("parallel"),
        name="final_norm",
    )(x, g.reshape(1, d))


def kernel(x_prompt, x_sample, cache_k, cache_v, state_ssm_fwd, state_ssm_bwd, c, c_ctx, w_mod, b_mod, norm_mix, norm_ffn, norm_final, w_in, conv_w, conv_b, dt_bias, a_log, d_skip, ssm_gain, lam_vec, subln_gain, w_out, pool_w, pool_b, pool_scale, w_coarse, b_coarse, w_fine, b_fine, w_gate, w_up, w_down):
    n_ctx, len_ctx, d = x_prompt.shape
    n_lat, len_lat, _ = x_sample.shape
    rows = _Rows(n_ctx, len_ctx, n_lat, len_lat)
    depth = w_mod.shape[0]
    d_ssm = ssm_gain.shape[1]
    heads = a_log.shape[2]
    da_heads = cache_k.shape[3]
    qk_cols = da_heads * 2 * DA_QK_DIM
    xbc_cols = conv_w.shape[2]
    past = cache_k.shape[2]

    x = jnp.concatenate([x_prompt.reshape(rows.t_ctx, d), x_sample.reshape(rows.t_lat, d)], axis=0)
    n_vec = -(-(1 + n_lat) // 8) * 8
    cvecs = jnp.zeros((n_vec, d), F32).at[0].set(c_ctx).at[1:1 + n_lat].set(c)
    mod = _modulation(cvecs, w_mod, b_mod).reshape(depth, n_vec, N_MOD, d)

    ks, vs, sf, sb = [], [], [], []
    for l in range(depth):
        j = l // 2
        if l % 2 == 0:
            lam_init = 0.8 - 0.6 * math.exp(-0.3 * l)
            splits = [d_ssm, d_ssm + xbc_cols, d_ssm + xbc_cols + 2 * heads]
            splits += [splits[-1] + qk_cols, splits[-1] + 2 * qk_cols]
            wz, wxbc, wdt, wq, wk, wv = [w.astype(BF16) for w in jnp.split(w_in[j], splits, axis=1)]
            zT, xbc, dt, dtT, qs, kr, vt, k_ctx, v_ctx = _inproj(rows, x, mod[l], norm_mix[l],
                                                                 (wz.T, wxbc, wdt, wdt.T, wq, wk, wv))
            xsT, bm, cm = _conv(rows, xbc, conv_w[j], conv_b[j], d_ssm)
            hf0 = state_ssm_fwd[:, j].reshape(n_lat, heads * SSM_HEAD_DIM, SSM_STATE)
            hb0 = state_ssm_bwd[:, j].reshape(n_lat, heads * SSM_HEAD_DIM, SSM_STATE)
            yfT, ybT, hf, hb = _ssd(rows, xsT, bm, cm, dt, dtT, dt_bias[j], a_log[j], hf0, hb0)
            o_ctx = _attention(qs, kr, vt, lam_vec[j], subln_gain[j], lam_init,
                               n_seq=n_ctx, length=len_ctx, row0=0)
            ck = cache_k[:, j].reshape(n_lat, past, qk_cols).astype(BF16)
            cvt = jnp.transpose(cache_v[:, j], (0, 2, 3, 1))
            cvt = jnp.concatenate([cvt, jnp.ones((n_lat, da_heads, V_ROWS - DA_V_DIM, past), F32)], axis=2)
            cvt = cvt.reshape(n_lat, da_heads * V_ROWS, past).astype(BF16)
            o_lat = _attention(qs, kr, vt, lam_vec[j], subln_gain[j], lam_init,
                               n_seq=n_lat, length=len_lat, row0=rows.t_ctx, cache_k=ck, cache_vt=cvt)
            dsk = jnp.repeat(d_skip[j], SSM_HEAD_DIM, axis=1)
            wo = w_out[j].astype(BF16)
            x = _outproj(rows, yfT, ybT, xsT, zT, o_ctx, o_lat, x, mod[l], dsk, ssm_gain[j],
                         wo[:d_ssm], wo[d_ssm:])
            ks.append(k_ctx.reshape(n_ctx, len_ctx, da_heads, 2, DA_QK_DIM))
            vs.append(v_ctx.reshape(n_ctx, len_ctx, da_heads, DA_V_DIM))
            sf.append(hf[:n_ctx].reshape(n_ctx, heads, SSM_HEAD_DIM, SSM_STATE))
            sb.append(hb[:n_ctx].reshape(n_ctx, heads, SSM_HEAD_DIM, SSM_STATE))
        else:
            x = _pool(rows, x, mod[l], norm_mix[l], pool_w[j].astype(BF16), pool_b[j], pool_scale[j])
        x = _moe(rows, x, mod[l], norm_ffn[l], w_coarse[l], b_coarse[l], w_fine[l], b_fine[l],
                 w_gate, w_up, w_down, l)

    y_prompt = _final_norm(x, norm_final, 0, rows.t_ctx).reshape(n_ctx, len_ctx, d)
    y_sample = _final_norm(x, norm_final, rows.t_ctx, rows.t_lat).reshape(n_lat, len_lat, d)
    return (y_prompt, y_sample, jnp.stack(ks, axis=1), jnp.stack(vs, axis=1),
            jnp.stack(sf, axis=1), jnp.stack(sb, axis=1))
```

```python
import functools
import math

import jax
import jax.numpy as jnp
from jax import lax
from jax.experimental import pallas as pl
from jax.experimental.pallas import tpu as pltpu

F32 = jnp.float32
BF16 = jnp.bfloat16
I32 = jnp.int32
U32 = jnp.uint32
HIGHEST = lax.Precision.HIGHEST

EPS = 1e-6
N_MOD = 6
SSM_HEAD_DIM = 64
SSM_GROUPS = 4
SSM_STATE = 128
CHUNK = 128
CONV_WIDTH = 5
CONV_HALO = 8
DA_QK_DIM = 64
DA_V_DIM = 128
GRID_W = 64
ROPE_FREQS = 16
ROPE_THETA = 10000.0
POOL_WINDOWS = (2, 4, 8, 16)
MOE_GROUPS = 4
EXPERTS_PER_GROUP = 8
MOE_TOP_K = 2

TOKEN_TILE = 256
MOE_TILE = 512
EXPERT_ROWS = 512
ATTN_KEY_BLOCK = 512
ATTN_QUERY_TILE = 1024
V_ROWS = DA_V_DIM + 16
LOG2_E = 1.4426950408889634
DMA_ISSUE_UNROLL = 8
ROUTER_LANES = 128
VMEM_LIMIT_BYTES = 56 * 1024 * 1024

NT_DIMS = (((1,), (1,)), ((), ()))
TN_DIMS = (((0,), (0,)), ((), ()))


def _params(*sem):
    return pltpu.CompilerParams(dimension_semantics=sem, vmem_limit_bytes=VMEM_LIMIT_BYTES)


def _silu(x):
    return x * jax.nn.sigmoid(x)


def _softplus(x):
    return jnp.maximum(x, 0.0) + jnp.log1p(jnp.exp(-jnp.abs(x)))


def _rms(x, g):
    return x * lax.rsqrt(jnp.mean(x * x, axis=-1, keepdims=True) + EPS) * g


def _modnorm(x, g, scale, shift):
    return _rms(x, g) * (1.0 + scale) + shift


def _bdot(a, b):
    return jnp.dot(a, b, preferred_element_type=F32)


class _Rows:
    def __init__(self, n_ctx, len_ctx, n_lat, len_lat):
        self.n_ctx, self.len_ctx, self.n_lat, self.len_lat = n_ctx, len_ctx, n_lat, len_lat
        self.t_ctx = n_ctx * len_ctx
        self.t_lat = n_lat * len_lat
        self.t = self.t_ctx + self.t_lat
        assert len_ctx % TOKEN_TILE == 0 and len_lat % TOKEN_TILE == 0
        self.ctx_tiles = self.t_ctx // TOKEN_TILE
        self.tiles = self.t // TOKEN_TILE
        self.tiles_per_ctx = len_ctx // TOKEN_TILE
        self.tiles_per_lat = len_lat // TOKEN_TILE

    def mod_row(self, i, tile=TOKEN_TILE):
        assert self.t_ctx % tile == 0 and self.len_lat % tile == 0
        ctx_tiles = self.t_ctx // tile
        return jnp.where(i < ctx_tiles, 0, 1 + (i - ctx_tiles) // (self.len_lat // tile))

    def seq_pos(self, i):
        is_ctx = i < self.ctx_tiles
        pos = jnp.where(is_ctx, i % self.tiles_per_ctx, (i - self.ctx_tiles) % self.tiles_per_lat)
        n = jnp.where(is_ctx, self.tiles_per_ctx, self.tiles_per_lat)
        return pos, n


def _row_spec(width, tile=TOKEN_TILE):
    return pl.BlockSpec((tile, width), lambda i: (i, 0))


def _const_spec(shape):
    nd = len(shape)
    return pl.BlockSpec(shape, lambda i: (0,) * nd)


def _mod_kernel(c_ref, w_ref, b_ref, o_ref):
    s = _silu(c_ref[...])
    o_ref[0] = _bdot(s.astype(BF16), w_ref[0].astype(BF16)) + b_ref[0]


def _modulation(cvecs, w_mod, b_mod):
    depth, d, n = w_mod.shape
    tn = n // 4
    return pl.pallas_call(
        _mod_kernel,
        grid=(depth, n // tn),
        in_specs=[pl.BlockSpec(cvecs.shape, lambda l, j: (0, 0)),
                  pl.BlockSpec((1, d, tn), lambda l, j: (l, 0, j)),
                  pl.BlockSpec((1, 1, tn), lambda l, j: (l, 0, j))],
        out_specs=pl.BlockSpec((1, cvecs.shape[0], tn), lambda l, j: (l, 0, j)),
        out_shape=jax.ShapeDtypeStruct((depth, cvecs.shape[0], n), F32),
        compiler_params=_params("parallel", "parallel"),
        name="modulation",
    )(cvecs, w_mod, b_mod.reshape(depth, 1, n))


def _inproj_kernel(x_ref, mod_ref, g_ref, cos_ref, sa_ref, sb_ref, wzT, wxbc, wdt, wdtT, wq, wk, wv,
                   zT_o, xbc_o, dt_o, dtT_o, qs_o, kr_o, vt_o, k_o, v_o, *, ctx_tiles):
    is_ctx = pl.program_id(0) < ctx_tiles
    m = mod_ref[0]
    h = _modnorm(x_ref[...], g_ref[...], m[1:2], m[0:1]).astype(BF16)
    q = _bdot(h, wq[...])
    k = _bdot(h, wk[...])
    v = _bdot(h, wv[...])
    zT_o[...] = lax.dot_general(wzT[...], h, NT_DIMS, preferred_element_type=F32)
    xbc_o[...] = _bdot(h, wxbc[...])
    dt_o[...] = _bdot(h, wdt[...])
    dtT_o[...] = lax.dot_general(wdtT[...], h, NT_DIMS, preferred_element_type=F32)

    cos = jnp.where(is_ctx, 1.0, cos_ref[...])
    sa = jnp.where(is_ctx, 0.0, sa_ref[...])
    sb = jnp.where(is_ctx, 0.0, sb_ref[...])
    tile = q.shape[0]
    first_map = lax.broadcasted_iota(I32, (tile, DA_V_DIM), 1) < DA_QK_DIM
    scale = LOG2_E / math.sqrt(DA_QK_DIM)

    def rope(t):
        return (t * cos + pltpu.roll(t, DA_V_DIM - ROPE_FREQS, 1) * sa + pltpu.roll(t, ROPE_FREQS, 1) * sb)

    for hd in range(q.shape[1] // DA_V_DIM):
        sl = slice(hd * DA_V_DIM, (hd + 1) * DA_V_DIM)
        qr = rope(q[:, sl]) * scale
        qs_o[0, :, sl] = jnp.where(first_map, qr, 0.0).astype(BF16)
        qs_o[1, :, sl] = jnp.where(first_map, 0.0, qr).astype(BF16)
        kr_o[:, sl] = rope(k[:, sl]).astype(BF16)
        r0 = hd * V_ROWS
        vt_o[r0:r0 + DA_V_DIM, :] = v[:, sl].T.astype(BF16)
        vt_o[r0 + DA_V_DIM:r0 + V_ROWS, :] = jnp.ones((V_ROWS - DA_V_DIM, tile), BF16)

    @pl.when(is_ctx)
    def _():
        k_o[...] = k
        v_o[...] = v


def _rope_tables(length):
    t = jnp.arange(length)
    pos = jnp.stack([t // GRID_W, t % GRID_W], axis=-1).astype(F32)
    inv = ROPE_THETA ** (-jnp.arange(ROPE_FREQS, dtype=F32) / ROPE_FREQS)
    ang = pos[..., None] * inv
    lane = jnp.arange(DA_V_DIM)
    axis = (lane % DA_QK_DIM) // (2 * ROPE_FREQS)
    freq = lane % ROPE_FREQS
    second_half = (lane % (2 * ROPE_FREQS)) >= ROPE_FREQS
    a = ang[:, axis, freq]
    cos, sin = jnp.cos(a), jnp.sin(a)
    return cos, jnp.where(second_half, 0.0, -sin), jnp.where(second_half, sin, 0.0)


def _inproj(rows, x, mod_l, g, weights):
    t, d = x.shape
    wzT, wxbc, wdt, wdtT, wq, wk, wv = weights
    w = wq.shape[1]
    vt_rows = wv.shape[1] // DA_V_DIM * V_ROWS
    tables = _rope_tables(rows.len_lat)

    def tab(i):
        return (jnp.maximum(i - rows.ctx_tiles, 0) % rows.tiles_per_lat, 0)

    def col_spec(n):
        return pl.BlockSpec((n, TOKEN_TILE), lambda i: (0, i))

    ctx_spec = pl.BlockSpec((TOKEN_TILE, w), lambda i: (jnp.minimum(i, rows.ctx_tiles - 1), 0))
    return pl.pallas_call(
        functools.partial(_inproj_kernel, ctx_tiles=rows.ctx_tiles),
        grid=(rows.tiles,),
        in_specs=[_row_spec(d),
                  pl.BlockSpec((1, N_MOD, d), lambda i: (rows.mod_row(i), 0, 0)),
                  _const_spec((1, d))] + [pl.BlockSpec((TOKEN_TILE, DA_V_DIM), tab)] * 3
                 + [_const_spec(wt.shape) for wt in weights],
        out_specs=[col_spec(wzT.shape[0]), _row_spec(wxbc.shape[1]), _row_spec(wdt.shape[1]),
                   col_spec(wdtT.shape[0]), pl.BlockSpec((2, TOKEN_TILE, w), lambda i: (0, i, 0)),
                   _row_spec(w), col_spec(vt_rows), ctx_spec, ctx_spec],
        out_shape=[jax.ShapeDtypeStruct((wzT.shape[0], t), F32), jax.ShapeDtypeStruct((t, wxbc.shape[1]), F32),
                   jax.ShapeDtypeStruct((t, wdt.shape[1]), F32), jax.ShapeDtypeStruct((wdtT.shape[0], t), F32),
                   jax.ShapeDtypeStruct((2, t, w), BF16), jax.ShapeDtypeStruct((t, w), BF16),
                   jax.ShapeDtypeStruct((vt_rows, t), BF16),
                   jax.ShapeDtypeStruct((rows.t_ctx, w), F32), jax.ShapeDtypeStruct((rows.t_ctx, w), F32)],
        compiler_params=_params("arbitrary"),
        name="inproj",
    )(x, mod_l, g.reshape(1, d), *tables, *weights)


def _conv_kernel(cur_ref, prev_ref, next_ref, w_ref, b_ref, xsT_o, b_o, c_o, ext_ref, *, rows, d_ssm, gn):
    i = pl.program_id(0)
    pos, n = rows.seq_pos(i)
    tile = cur_ref.shape[0]
    ext_ref[0:CONV_HALO, :] = jnp.where(pos == 0, 0.0, prev_ref[...])
    ext_ref[CONV_HALO:CONV_HALO + tile, :] = cur_ref[...]
    ext_ref[CONV_HALO + tile:, :] = jnp.where(pos == n - 1, 0.0, next_ref[...])
    width = cur_ref.shape[1]
    step = 512
    for c0 in range(0, width, step):
        acc = jnp.broadcast_to(b_ref[:, c0:c0 + step], (tile, step))
        for k in range(CONV_WIDTH):
            off = CONV_HALO - CONV_WIDTH // 2 + k
            acc = acc + w_ref[k:k + 1, c0:c0 + step] * ext_ref[off:off + tile, c0:c0 + step]
        y = _silu(acc)
        if c0 < d_ssm:
            xsT_o[c0:c0 + step, :] = y.T
        elif c0 < d_ssm + gn:
            b_o[:, c0 - d_ssm:c0 - d_ssm + step] = y.astype(BF16)
        else:
            c_o[:, c0 - d_ssm - gn:c0 - d_ssm - gn + step] = y.astype(BF16)


def _conv(rows, xbc, conv_w, conv_b, d_ssm):
    t, width = xbc.shape
    gn = (width - d_ssm) // 2
    hb = TOKEN_TILE // CONV_HALO
    last = t // CONV_HALO - 1
    return pl.pallas_call(
        functools.partial(_conv_kernel, rows=rows, d_ssm=d_ssm, gn=gn),
        grid=(rows.tiles,),
        in_specs=[_row_spec(width),
                  pl.BlockSpec((CONV_HALO, width), lambda i: (jnp.maximum(i * hb - 1, 0), 0)),
                  pl.BlockSpec((CONV_HALO, width), lambda i: (jnp.minimum((i + 1) * hb, last), 0)),
                  _const_spec(conv_w.shape), _const_spec((1, width))],
        out_specs=[pl.BlockSpec((d_ssm, TOKEN_TILE), lambda i: (0, i)), _row_spec(gn), _row_spec(gn)],
        out_shape=[jax.ShapeDtypeStruct((d_ssm, t), F32),
                   jax.ShapeDtypeStruct((t, gn), BF16),
                   jax.ShapeDtypeStruct((t, gn), BF16)],
        scratch_shapes=[pltpu.VMEM((TOKEN_TILE + 2 * CONV_HALO, width), F32)],
        compiler_params=_params("parallel"),
        name="conv",
    )(xbc, xbc, xbc, conv_w, conv_b.reshape(1, width))


def _contributes(q, reverse, target_rows):
    r = lax.broadcasted_iota(I32, (q, q), 0)
    c = lax.broadcasted_iota(I32, (q, q), 1)
    tgt, src = (r, c) if target_rows else (c, r)
    return (src >= tgt) if reverse else (src <= tgt)


def _ssd_direction(xT_ref, b_ref, c_ref, dt_c, dt_r, bias_row, bias_col, a_row, a_col,
                   h_refs, yT_ref, reverse):
    q = CHUNK
    heads = a_row.shape[1]
    dt_c = _softplus(dt_c + bias_row)
    dt_r = _softplus(dt_r + bias_col)
    keep_t = _contributes(q, reverse, False)
    acs_c = jnp.dot(_contributes(q, reverse, True).astype(F32), dt_c * a_row,
                    precision=HIGHEST, preferred_element_type=F32)
    acs_r = jnp.dot(dt_r * a_col, keep_t.astype(F32), precision=HIGHEST, preferred_element_type=F32)
    last = 0 if reverse else q - 1
    total_r = acs_r[:, last:last + 1]
    grow_r = jnp.exp(acs_r)
    to_end_r = jnp.exp(total_r - acs_r)
    carry_r = jnp.exp(total_r)
    per_group = heads // SSM_GROUPS
    p = SSM_HEAD_DIM
    for g in range(SSM_GROUPS):
        bg = b_ref[:, g * SSM_STATE:(g + 1) * SSM_STATE]
        cg = c_ref[:, g * SSM_STATE:(g + 1) * SSM_STATE]
        cb_t = lax.dot_general(bg, cg, NT_DIMS, preferred_element_type=F32)
        h_in = h_refs[g][...]
        y_off = lax.dot_general(h_in.astype(BF16), cg, NT_DIMS, preferred_element_type=F32)
        xw, h_decayed = [], []
        for r in range(per_group):
            h = g * per_group + r
            rs = slice(r * p, (r + 1) * p)
            seg = acs_r[h:h + 1, :] - acs_c[:, h:h + 1]
            decay = jnp.exp(jnp.where(keep_t, seg, -jnp.inf))
            l_t = (cb_t * decay).astype(BF16)
            xdt = xT_ref[h * p:(h + 1) * p, :] * dt_r[h:h + 1, :]
            yT_ref[h * p:(h + 1) * p, :] = _bdot(xdt.astype(BF16), l_t) + y_off[rs, :] * grow_r[h:h + 1, :]
            xw.append((xdt * to_end_r[h:h + 1, :]).astype(BF16))
            h_decayed.append(h_in[rs, :] * carry_r[h:h + 1, :])
        state = _bdot(jnp.concatenate(xw, axis=0), bg)
        h_refs[g][...] = jnp.concatenate(h_decayed, axis=0) + state


def _ssd_kernel(xf, bf, cf, dtf, dtTf, xb, bb, cb, dtb, dtTb, bias_ref, biasT_ref, alog_ref, alogT_ref,
                hf0, hb0, yf_o, yb_o, hf_o, hb_o, *scratch, geom):
    s = pl.program_id(0)
    is_ctx, _, c, nc, _ = geom(s)
    heads = alog_ref.shape[1]
    hf_scr, hb_scr = scratch[:SSM_GROUPS], scratch[SSM_GROUPS:]
    rows_g = hf_scr[0].shape[0]

    @pl.when(c == 0)
    def _():
        for g in range(SSM_GROUPS):
            hf_scr[g][...] = jnp.where(is_ctx, 0.0, hf0[0, g * rows_g:(g + 1) * rows_g, :])
            hb_scr[g][...] = jnp.where(is_ctx, 0.0, hb0[0, g * rows_g:(g + 1) * rows_g, :])

    a_row = -jnp.exp(alog_ref[...])
    a_col = -jnp.exp(alogT_ref[...])
    _ssd_direction(xf, bf, cf, dtf[:, 0:heads], dtTf[0:heads, :], bias_ref[0:1, :], biasT_ref[:, 0:1],
                   a_row[0:1, :], a_col[:, 0:1], hf_scr, yf_o, False)
    _ssd_direction(xb, bb, cb, dtb[:, heads:2 * heads], dtTb[heads:2 * heads, :], bias_ref[1:2, :],
                   biasT_ref[:, 1:2], a_row[1:2, :], a_col[:, 1:2], hb_scr, yb_o, True)

    @pl.when(c == nc - 1)
    def _():
        for g in range(SSM_GROUPS):
            hf_o[0, g * rows_g:(g + 1) * rows_g, :] = hf_scr[g][...]
            hb_o[0, g * rows_g:(g + 1) * rows_g, :] = hb_scr[g][...]


def _ssd(rows, xsT, bm, cm, dt, dtT, dt_bias, a_log, hf0, hb0):
    d_ssm, t = xsT.shape
    gn = bm.shape[1]
    heads2 = dt.shape[1]
    ncc, ncl = rows.len_ctx // CHUNK, rows.len_lat // CHUNK
    s_ctx, s_lat = rows.n_ctx * ncc, rows.n_lat * ncl
    n_seq = rows.n_ctx + rows.n_lat

    def geom(s):
        is_ctx = s < s_ctx
        sl = s - s_ctx
        seq = jnp.where(is_ctx, s // ncc, rows.n_ctx + sl // ncl)
        c = jnp.where(is_ctx, s % ncc, sl % ncl)
        nc = jnp.where(is_ctx, ncc, ncl)
        base = jnp.where(is_ctx, (s // ncc) * ncc, s_ctx + (sl // ncl) * ncl)
        return is_ctx, seq, c, nc, base

    def fwd(s):
        _, _, c, _, base = geom(s)
        return base + c

    def bwd(s):
        _, _, c, nc, base = geom(s)
        return base + nc - 1 - c

    def lat_seq(s):
        return jnp.maximum(geom(s)[1] - rows.n_ctx, 0)

    def side(blk):
        return [pl.BlockSpec((d_ssm, CHUNK), lambda s: (0, blk(s))),
                pl.BlockSpec((CHUNK, gn), lambda s: (blk(s), 0)),
                pl.BlockSpec((CHUNK, gn), lambda s: (blk(s), 0)),
                pl.BlockSpec((CHUNK, heads2), lambda s: (blk(s), 0)),
                pl.BlockSpec((heads2, CHUNK), lambda s: (0, blk(s)))]

    state_shape = (SSM_HEAD_DIM * (heads2 // 2), SSM_STATE)
    group_shape = (state_shape[0] // SSM_GROUPS, SSM_STATE)
    state_spec_in = pl.BlockSpec((1,) + state_shape, lambda s: (lat_seq(s), 0, 0))
    state_spec_out = pl.BlockSpec((1,) + state_shape, lambda s: (geom(s)[1], 0, 0))
    return pl.pallas_call(
        functools.partial(_ssd_kernel, geom=geom),
        grid=(s_ctx + s_lat,),
        in_specs=side(fwd) + side(bwd) + [_const_spec(dt_bias.shape), _const_spec(dt_bias.T.shape),
                                          _const_spec(a_log.shape), _const_spec(a_log.T.shape),
                                          state_spec_in, state_spec_in],
        out_specs=[pl.BlockSpec((d_ssm, CHUNK), lambda s: (0, fwd(s))),
                   pl.BlockSpec((d_ssm, CHUNK), lambda s: (0, bwd(s))),
                   state_spec_out, state_spec_out],
        out_shape=[jax.ShapeDtypeStruct((d_ssm, t), F32), jax.ShapeDtypeStruct((d_ssm, t), F32),
                   jax.ShapeDtypeStruct((n_seq,) + state_shape, F32),
                   jax.ShapeDtypeStruct((n_seq,) + state_shape, F32)],
        scratch_shapes=[pltpu.VMEM(group_shape, F32)] * (2 * SSM_GROUPS),
        compiler_params=_params("arbitrary"),
        name="ssd",
    )(xsT, bm, cm, dt, dtT, xsT, bm, cm, dt, dtT, dt_bias, dt_bias.T, a_log, a_log.T, hf0, hb0)


def _attn_kernel(*refs, tq, tk, has_cache, lam_init):
    if has_cache:
        qs_ref, kc_ref, vct_ref, k_ref, vt_ref, lamv_ref, gain_ref, o_ref = refs
    else:
        qs_ref, k_ref, vt_ref, lamv_ref, gain_ref, o_ref = refs
    q = qs_ref[...].reshape(2 * tq, DA_V_DIM)
    blocks = []
    if has_cache:
        blocks += [(kc_ref, vct_ref, c, True) for c in range(kc_ref.shape[1] // tk)]
    blocks += [(k_ref, vt_ref, c, False) for c in range(k_ref.shape[0] // tk)]
    m = jnp.full((1, 2 * tq), -jnp.inf, F32)
    acc = jnp.zeros((V_ROWS, 2 * tq), F32)
    def scores(blk):
        kref, _, c, cached = blk
        ks = slice(c * tk, (c + 1) * tk)
        kb = kref[0, ks, :] if cached else kref[ks, :]
        return lax.dot_general(kb, q, NT_DIMS, preferred_element_type=F32)

    s_next = scores(blocks[0])
    for n, (_, vref, c, cached) in enumerate(blocks):
        ks = slice(c * tk, (c + 1) * tk)
        vb = vref[0, :, ks] if cached else vref[:, ks]
        s = s_next
        if n + 1 < len(blocks):
            s_next = scores(blocks[n + 1])
        m_new = jnp.maximum(m, jnp.max(s, axis=0, keepdims=True))
        acc = jnp.exp2(m - m_new) * acc + _bdot(vb, jnp.exp2(s - m_new).astype(BF16))
        m = m_new
    lv = lamv_ref[...]
    lam = (jnp.exp(jnp.sum(lv[0:1] * lv[1:2], axis=1, keepdims=True))
           - jnp.exp(jnp.sum(lv[2:3] * lv[3:4], axis=1, keepdims=True)) + lam_init)
    on = acc[0:DA_V_DIM, :] / acc[DA_V_DIM:DA_V_DIM + 1, :]
    ot = on[:, 0:tq] - lam * on[:, tq:2 * tq]
    ot = ot * lax.rsqrt(jnp.mean(ot * ot, axis=0, keepdims=True) + EPS) * gain_ref[...] * (1.0 - lam_init)
    o_ref[...] = ot.T.astype(BF16)


def _attention(qs, kr, vt, lam_vec, gain, lam_init, *, n_seq, length, row0, cache_k=None, cache_vt=None):
    w = kr.shape[1]
    heads = w // DA_V_DIM
    tq = min(length, ATTN_QUERY_TILE)
    tk = min(length, ATTN_KEY_BLOCK)
    assert length % tq == 0 and length % tk == 0 and row0 % length == 0
    nq = length // tq
    has_cache = cache_k is not None
    q_blk0, seq0 = row0 // tq, row0 // length

    in_specs = [pl.BlockSpec((2, tq, DA_V_DIM), lambda b, h, i: (0, q_blk0 + b * nq + i, h))]
    args = [qs]
    if has_cache:
        past = cache_k.shape[1]
        assert past % tk == 0
        in_specs += [pl.BlockSpec((1, past, DA_V_DIM), lambda b, h, i: (b, 0, h)),
                     pl.BlockSpec((1, V_ROWS, past), lambda b, h, i: (b, h, 0))]
        args += [cache_k, cache_vt]
    in_specs += [pl.BlockSpec((length, DA_V_DIM), lambda b, h, i: (seq0 + b, h)),
                 pl.BlockSpec((V_ROWS, length), lambda b, h, i: (h, seq0 + b)),
                 pl.BlockSpec(lam_vec.shape, lambda b, h, i: (0, 0)),
                 pl.BlockSpec((DA_V_DIM, 1), lambda b, h, i: (0, 0))]
    args += [kr, vt, lam_vec, gain.reshape(DA_V_DIM, 1)]
    return pl.pallas_call(
        functools.partial(_attn_kernel, tq=tq, tk=tk, has_cache=has_cache, lam_init=lam_init),
        grid=(n_seq, heads, nq),
        in_specs=in_specs,
        out_specs=pl.BlockSpec((tq, DA_V_DIM), lambda b, h, i: (b * nq + i, h)),
        out_shape=jax.ShapeDtypeStruct((n_seq * length, w), BF16),
        compiler_params=_params("parallel", "parallel", "arbitrary"),
        name="attention_lat" if has_cache else "attention_ctx",
    )(*args)


def _outproj_kernel(yfT_ref, ybT_ref, xsT_ref, zT_ref, oc_ref, ol_ref, x_ref, mod_ref, dsk_ref, gain_ref,
                    wy_ref, wo_ref, out_ref, *, ctx_tiles):
    is_ctx = pl.program_id(0) < ctx_tiles
    y = yfT_ref[...] + ybT_ref[...] + (dsk_ref[0] + dsk_ref[1]) * xsT_ref[...]
    y = y * _silu(zT_ref[...])
    y = (y * lax.rsqrt(jnp.mean(y * y, axis=0, keepdims=True) + EPS) * gain_ref[...]).astype(BF16)
    o = jnp.where(is_ctx, oc_ref[...], ol_ref[...])
    mixed = lax.dot_general(y, wy_ref[...], TN_DIMS, preferred_element_type=F32) + _bdot(o, wo_ref[...])
    out_ref[...] = x_ref[...] + mod_ref[0][2:3] * mixed


def _outproj(rows, yfT, ybT, xsT, zT, o_ctx, o_lat, x, mod_l, d_skip, gain, wy, wo):
    t, d = x.shape
    d_ssm = xsT.shape[0]
    wa = o_ctx.shape[1]
    col_spec = pl.BlockSpec((d_ssm, TOKEN_TILE), lambda i: (0, i))
    return pl.pallas_call(
        functools.partial(_outproj_kernel, ctx_tiles=rows.ctx_tiles),
        grid=(rows.tiles,),
        in_specs=[col_spec] * 4 + [
            pl.BlockSpec((TOKEN_TILE, wa), lambda i: (jnp.minimum(i, rows.ctx_tiles - 1), 0)),
            pl.BlockSpec((TOKEN_TILE, wa), lambda i: (jnp.maximum(i - rows.ctx_tiles, 0), 0)),
            _row_spec(d),
            pl.BlockSpec((1, N_MOD, d), lambda i: (rows.mod_row(i), 0, 0)),
            _const_spec((2, d_ssm, 1)), _const_spec((d_ssm, 1)), _const_spec(wy.shape), _const_spec(wo.shape)],
        out_specs=_row_spec(d),
        out_shape=jax.ShapeDtypeStruct((t, d), F32),
        compiler_params=_params("parallel"),
        name="outproj",
    )(yfT, ybT, xsT, zT, o_ctx, o_lat, x, mod_l, d_skip.reshape(2, d_ssm, 1), gain.reshape(d_ssm, 1), wy, wo)


def _pool_kernel(cur_ref, prev_ref, next_ref, mod_ref, g_ref, w_ref, b_ref, sc_ref, out_ref, ext_ref,
                 *, rows, len_ctx, len_lat):
    i = pl.program_id(0)
    pos, n = rows.seq_pos(i)
    tile = cur_ref.shape[0]
    m = mod_ref[0]
    g = g_ref[...]

    def norm(x):
        return _modnorm(x, g, m[1:2], m[0:1])

    x = cur_ref[...]
    ext_ref[0:CONV_HALO, :] = jnp.where(pos == 0, 0.0, norm(prev_ref[...]))
    ext_ref[CONV_HALO:CONV_HALO + tile, :] = norm(x)
    ext_ref[CONV_HALO + tile:, :] = jnp.where(pos == n - 1, 0.0, norm(next_ref[...]))
    length = jnp.where(i < rows.ctx_tiles, len_ctx, len_lat)
    tpos = pos * tile + lax.broadcasted_iota(I32, (tile, 1), 0)
    pg = cur_ref.shape[1] // len(POOL_WINDOWS)
    for gi, win in enumerate(POOL_WINDOWS):
        cs = slice(gi * pg, (gi + 1) * pg)
        half = win // 2
        acc = ext_ref[CONV_HALO - half:CONV_HALO - half + tile, cs]
        for off in range(1 - half, half):
            acc = acc + ext_ref[CONV_HALO + off:CONV_HALO + off + tile, cs]
        cnt = jnp.minimum(tpos + half, length) - jnp.maximum(tpos - half, 0)
        pooled = acc / cnt.astype(F32) - ext_ref[CONV_HALO:CONV_HALO + tile, cs]
        mixed = (_bdot(pooled.astype(BF16), w_ref[gi]) + b_ref[gi:gi + 1, :]) * sc_ref[:, cs]
        out_ref[:, cs] = x[:, cs] + m[2:3, cs] * mixed


def _pool(rows, x, mod_l, g, pool_w, pool_b, pool_scale):
    t, d = x.shape
    hb = TOKEN_TILE // CONV_HALO
    last = t // CONV_HALO - 1
    return pl.pallas_call(
        functools.partial(_pool_kernel, rows=rows, len_ctx=rows.len_ctx, len_lat=rows.len_lat),
        grid=(rows.tiles,),
        in_specs=[_row_spec(d),
                  pl.BlockSpec((CONV_HALO, d), lambda i: (jnp.maximum(i * hb - 1, 0), 0)),
                  pl.BlockSpec((CONV_HALO, d), lambda i: (jnp.minimum((i + 1) * hb, last), 0)),
                  pl.BlockSpec((1, N_MOD, d), lambda i: (rows.mod_row(i), 0, 0)),
                  _const_spec((1, d)), _const_spec(pool_w.shape), _const_spec(pool_b.shape),
                  _const_spec((1, d))],
        out_specs=_row_spec(d),
        out_shape=jax.ShapeDtypeStruct((t, d), F32),
        scratch_shapes=[pltpu.VMEM((TOKEN_TILE + 2 * CONV_HALO, d), F32)],
        compiler_params=_params("parallel"),
        name="pool",
    )(x, x, x, mod_l, g.reshape(1, d), pool_w, pool_b, pool_scale.reshape(1, d))


def _router_kernel(x_ref, mod_ref, g_ref, w_ref, bias_ref, ri_o, rf_o, cnt_o, cnt_scr,
                   *, n_experts):
    i = pl.program_id(0)

    @pl.when(i == 0)
    def _():
        cnt_scr[...] = jnp.zeros(cnt_scr.shape, F32)

    m = mod_ref[0]
    h = _modnorm(x_ref[...], g_ref[...], m[4:5], m[3:4])
    tm = h.shape[0]
    lg = lax.dot_general(w_ref[...], h.astype(BF16), NT_DIMS, preferred_element_type=F32) + bias_ref[...]
    row = lax.broadcasted_iota(I32, (n_experts, tm), 0)
    big = jnp.int32(n_experts + MOE_GROUPS)
    lc = lg[n_experts:n_experts + 8, :]
    crow = lax.broadcasted_iota(I32, (8, tm), 0)
    lc = jnp.where(crow < MOE_GROUPS, lc, -jnp.inf)
    mc = jnp.max(lc, axis=0, keepdims=True)
    group_p = 1.0 / jnp.sum(jnp.exp(lc - mc), axis=0, keepdims=True)
    g_idx = jnp.min(jnp.where(lc == mc, crow, big), axis=0, keepdims=True)
    lf = jnp.where(row // EXPERTS_PER_GROUP == g_idx, lg[0:n_experts, :], -jnp.inf)
    m1 = jnp.max(lf, axis=0, keepdims=True)
    e1 = jnp.min(jnp.where(lf == m1, row, big), axis=0, keepdims=True)
    lf2 = jnp.where(row == e1, -jnp.inf, lf)
    m2 = jnp.max(lf2, axis=0, keepdims=True)
    e2 = jnp.min(jnp.where(lf2 == m2, row, big), axis=0, keepdims=True)
    r2 = jnp.exp(m2 - m1)
    gate1 = group_p / (1.0 + r2)
    gate2 = group_p * r2 / (1.0 + r2)
    oh1 = (row == e1)
    oh2 = (row == e2)
    oh = jnp.where(oh1 | oh2, 1.0, 0.0)
    before = (lax.broadcasted_iota(I32, (tm, tm), 0) < lax.broadcasted_iota(I32, (tm, tm), 1))
    prior = _bdot(oh.astype(BF16), before.astype(BF16)) + cnt_scr[...]
    rank1 = jnp.sum(jnp.where(oh1, prior, 0.0), axis=0, keepdims=True)
    rank2 = jnp.sum(jnp.where(oh2, prior, 0.0), axis=0, keepdims=True)
    cnt_scr[...] = cnt_scr[...] + jnp.sum(oh, axis=1, keepdims=True)
    zi = jnp.zeros((4, tm), I32)
    ri_o[0] = jnp.concatenate([e1, e2, rank1.astype(I32), rank2.astype(I32), zi], axis=0)
    rf_o[0] = jnp.concatenate([gate1, gate2, jnp.zeros((6, tm), F32)], axis=0)
    cnt_o[...] = jnp.broadcast_to(cnt_scr[...], cnt_o.shape)


def _router(rows, x, mod_l, g, w_t, bias, n_experts):
    t, d = x.shape
    tm = MOE_TILE
    nt = t // tm
    return pl.pallas_call(
        functools.partial(_router_kernel, n_experts=n_experts),
        grid=(nt,),
        in_specs=[_row_spec(d, tm), pl.BlockSpec((1, N_MOD, d), lambda i: (rows.mod_row(i, tm), 0, 0)),
                  _const_spec((1, d)), _const_spec(w_t.shape), _const_spec(bias.shape)],
        out_specs=[pl.BlockSpec((1, 8, tm), lambda i: (i, 0, 0)),
                   pl.BlockSpec((1, 8, tm), lambda i: (i, 0, 0)),
                   _const_spec((n_experts, ROUTER_LANES))],
        out_shape=[jax.ShapeDtypeStruct((nt, 8, tm), I32),
                   jax.ShapeDtypeStruct((nt, 8, tm), F32),
                   jax.ShapeDtypeStruct((n_experts, ROUTER_LANES), F32)],
        scratch_shapes=[pltpu.VMEM((n_experts, 1), F32)],
        compiler_params=_params("arbitrary"),
        name="router",
    )(x, mod_l, g.reshape(1, d), w_t, bias)


def _pack_bf16_pairs(x):
    n = x.shape[1] // 2
    hi = lax.bitcast_convert_type(x[:, :n].astype(BF16).astype(F32), U32)
    lo = lax.bitcast_convert_type(x[:, n:].astype(BF16).astype(F32), U32)
    return (hi & jnp.uint32(0xFFFF0000)) | (lo >> 16)


def _unpack_bf16_pairs(w):
    hi = lax.bitcast_convert_type(w & jnp.uint32(0xFFFF0000), F32)
    lo = lax.bitcast_convert_type(w << 16, F32)
    return hi.astype(BF16), lo.astype(BF16)


def _dispatch_kernel(dest_ref, x_ref, mod_ref, g_ref, xs_in_ref, xs_ref, h_scr, sem):
    del xs_in_ref
    i, n = pl.program_id(0), pl.num_programs(0)
    slot = i % 2
    tm = x_ref.shape[0]

    def drain(s):
        for k in range(MOE_TOP_K):
            pltpu.make_async_copy(h_scr.at[s], xs_ref.at[pl.ds(0, tm)], sem.at[s]).wait()

    @pl.when(i >= 2)
    def _():
        drain(slot)

    m = mod_ref[0]
    h_scr[slot] = _pack_bf16_pairs(_modnorm(x_ref[...], g_ref[...], m[4:5], m[3:4]))

    def start(t, carry):
        for k in range(MOE_TOP_K):
            pltpu.make_async_copy(h_scr.at[slot, pl.ds(t, 1)], xs_ref.at[pl.ds(dest_ref[0, k, t], 1)],
                                  sem.at[slot]).start()
        return carry

    lax.fori_loop(0, tm, start, 0, unroll=DMA_ISSUE_UNROLL)

    @pl.when(i == n - 1)
    def _():
        drain(slot)

    @pl.when((i == n - 1) & (n > 1))
    def _():
        drain(1 - slot)


def _dispatch(rows, x, mod_l, g, dest, n_slots):
    t, d = x.shape
    nt, _, tm = dest.shape
    xs0 = jnp.zeros((n_slots, d // 2), U32)
    return pl.pallas_call(
        _dispatch_kernel,
        grid=(nt,),
        in_specs=[pl.BlockSpec((1, MOE_TOP_K, tm), lambda i: (i, 0, 0), memory_space=pltpu.SMEM),
                  _row_spec(d, tm), pl.BlockSpec((1, N_MOD, d), lambda i: (rows.mod_row(i, tm), 0, 0)),
                  _const_spec((1, d)), pl.BlockSpec(memory_space=pl.ANY)],
        out_specs=pl.BlockSpec(memory_space=pl.ANY),
        out_shape=jax.ShapeDtypeStruct((n_slots, d // 2), U32),
        scratch_shapes=[pltpu.VMEM((2, tm, d // 2), U32), pltpu.SemaphoreType.DMA((2,))],
        input_output_aliases={4: 0},
        compiler_params=_params("arbitrary"),
        name="moe_dispatch",
    )(dest, x, mod_l, g.reshape(1, d), xs0)


def _expert_kernel(be_ref, nu_ref, x_ref, wg_ref, wu_ref, wd_ref, y_ref):
    j = pl.program_id(0)

    @pl.when(j < nu_ref[0])
    def _():
        x_lo, x_hi = _unpack_bf16_pairs(x_ref[...])
        half = x_lo.shape[1]

        def proj(w_ref):
            return _bdot(x_lo, w_ref[0, 0, :half, :].astype(BF16)) + _bdot(x_hi, w_ref[0, 0, half:, :].astype(BF16))

        a = proj(wg_ref)
        u = proj(wu_ref)
        y_ref[...] = _bdot((_silu(a) * u).astype(BF16), wd_ref[0, 0].astype(BF16))

    @pl.when(j >= nu_ref[0])
    def _():
        y_ref[...] = jnp.zeros(y_ref.shape, F32)


def _experts(xs, block_expert, n_used, w_gate, w_up, w_down, layer):
    p = xs.shape[0]
    d, hid = w_gate.shape[2:]
    nb = p // EXPERT_ROWS
    return pl.pallas_call(
        _expert_kernel,
        grid_spec=pltpu.PrefetchScalarGridSpec(
            num_scalar_prefetch=2,
            grid=(nb,),
            in_specs=[pl.BlockSpec((EXPERT_ROWS, xs.shape[1]), lambda j, be, nu: (j, 0)),
                      pl.BlockSpec((1, 1, d, hid), lambda j, be, nu: (layer, be[j], 0, 0)),
                      pl.BlockSpec((1, 1, d, hid), lambda j, be, nu: (layer, be[j], 0, 0)),
                      pl.BlockSpec((1, 1, hid, d), lambda j, be, nu: (layer, be[j], 0, 0))],
            out_specs=pl.BlockSpec((EXPERT_ROWS, d), lambda j, be, nu: (j, 0))),
        out_shape=jax.ShapeDtypeStruct((p, d), F32),
        compiler_params=_params("arbitrary"),
        name="moe_experts",
    )(block_expert, n_used, xs, w_gate, w_up, w_down)


def _combine_kernel(dest_ref, dest_next_ref, x_ref, mod_ref, gates_ref, ys_ref, out_ref, buf, sem):
    i, n = pl.program_id(0), pl.num_programs(0)
    slot = i % 2
    tm = x_ref.shape[0]

    def gather(dref, s):
        def start(t, carry):
            for k in range(MOE_TOP_K):
                pltpu.make_async_copy(ys_ref.at[pl.ds(dref[0, k, t], 1)], buf.at[s, k, pl.ds(t, 1)],
                                      sem.at[s]).start()
            return carry

        lax.fori_loop(0, tm, start, 0, unroll=DMA_ISSUE_UNROLL)

    @pl.when(i == 0)
    def _():
        gather(dest_ref, 0)

    @pl.when(i + 1 < n)
    def _():
        gather(dest_next_ref, 1 - slot)

    eye = (lax.broadcasted_iota(I32, (tm, tm), 0) == lax.broadcasted_iota(I32, (tm, tm), 1))
    gates = gates_ref[0]
    g1 = jnp.sum(jnp.where(eye, gates[0:1, :], 0.0), axis=1, keepdims=True)
    g2 = jnp.sum(jnp.where(eye, gates[1:2, :], 0.0), axis=1, keepdims=True)
    for k in range(MOE_TOP_K):
        pltpu.make_async_copy(ys_ref.at[pl.ds(0, tm)], buf.at[slot, k], sem.at[slot]).wait()
    ffn = g1 * buf[slot, 0] + g2 * buf[slot, 1]
    out_ref[...] = x_ref[...] + mod_ref[0][5:6] * ffn


def _combine(rows, x, mod_l, dest, gates, ys):
    t, d = x.shape
    nt, _, tm = dest.shape
    return pl.pallas_call(
        _combine_kernel,
        grid=(nt,),
        in_specs=[pl.BlockSpec((1, MOE_TOP_K, tm), lambda i: (i, 0, 0), memory_space=pltpu.SMEM),
                  pl.BlockSpec((1, MOE_TOP_K, tm), lambda i: (jnp.minimum(i + 1, nt - 1), 0, 0),
                               memory_space=pltpu.SMEM),
                  _row_spec(d, tm), pl.BlockSpec((1, N_MOD, d), lambda i: (rows.mod_row(i, tm), 0, 0)),
                  pl.BlockSpec((1, 8, tm), lambda i: (i, 0, 0)),
                  pl.BlockSpec(memory_space=pl.ANY)],
        out_specs=_row_spec(d, tm),
        out_shape=jax.ShapeDtypeStruct((t, d), F32),
        scratch_shapes=[pltpu.VMEM((2, MOE_TOP_K, tm, d), F32), pltpu.SemaphoreType.DMA((2,))],
        compiler_params=_params("arbitrary"),
        name="moe_combine",
    )(dest, dest, x, mod_l, gates, ys)


def _moe(rows, x, mod_l, g, w_coarse, b_coarse, w_fine, b_fine, w_gate, w_up, w_down, layer):
    t, d = x.shape
    n_experts = w_fine.shape[1]
    n_pad = ROUTER_LANES - n_experts - MOE_GROUPS
    w_t = jnp.concatenate([w_fine.T, w_coarse.T, jnp.zeros((n_pad, d), F32)], axis=0)
    bias = jnp.concatenate([b_fine, b_coarse, jnp.zeros((n_pad,), F32)]).reshape(ROUTER_LANES, 1)
    ri, gates, counts = _router(rows, x, mod_l, g, w_t.astype(BF16), bias, n_experts)
    counts = counts[:, 0].astype(I32)
    padded = ((counts + EXPERT_ROWS - 1) // EXPERT_ROWS) * EXPERT_ROWS
    pad_end = jnp.cumsum(padded)
    pad_start = pad_end - padded
    nb = -(-(t * MOE_TOP_K) // EXPERT_ROWS) + n_experts
    block_row0 = jnp.arange(nb, dtype=I32) * EXPERT_ROWS
    block_expert = jnp.minimum(jnp.sum((pad_end[None, :] <= block_row0[:, None]).astype(I32), axis=1),
                               n_experts - 1)
    n_used = (pad_end[-1:] // EXPERT_ROWS).astype(I32)
    chosen = ri[:, 0:MOE_TOP_K, :, None] == jnp.arange(n_experts, dtype=I32)
    dest = jnp.sum(jnp.where(chosen, pad_start, 0), axis=-1) + ri[:, MOE_TOP_K:2 * MOE_TOP_K, :]
    xs = _dispatch(rows, x, mod_l, g, dest, nb * EXPERT_ROWS)
    ys = _experts(xs, block_expert, n_used, w_gate, w_up, w_down, layer)
    return _combine(rows, x, mod_l, dest, gates, ys)


def _final_kernel(x_ref, g_ref, o_ref):
    o_ref[...] = _rms(x_ref[...], g_ref[...])


def _final_norm(x, g, row0, n_rows):
    d = x.shape[1]
    blk0 = row0 // TOKEN_TILE
    return pl.pallas_call(
        _final_kernel,
        grid=(n_rows // TOKEN_TILE,),
        in_specs=[pl.BlockSpec((TOKEN_TILE, d), lambda i: (blk0 + i, 0)), _const_spec((1, d))],
        out_specs=_row_spec(d),
        out_shape=jax.ShapeDtypeStruct((n_rows, d), F32),
        compiler_params=_params("parallel"),
        name="final_norm",
    )(x, g.reshape(1, d))


def kernel(x_prompt, x_sample, cache_k, cache_v, state_ssm_fwd, state_ssm_bwd, c, c_ctx, w_mod, b_mod, norm_mix, norm_ffn, norm_final, w_in, conv_w, conv_b, dt_bias, a_log, d_skip, ssm_gain, lam_vec, subln_gain, w_out, pool_w, pool_b, pool_scale, w_coarse, b_coarse, w_fine, b_fine, w_gate, w_up, w_down):
    n_ctx, len_ctx, d = x_prompt.shape
    n_lat, len_lat, _ = x_sample.shape
    rows = _Rows(n_ctx, len_ctx, n_lat, len_lat)
    depth = w_mod.shape[0]
    d_ssm = ssm_gain.shape[1]
    heads = a_log.shape[2]
    da_heads = cache_k.shape[3]
    qk_cols = da_heads * 2 * DA_QK_DIM
    xbc_cols = conv_w.shape[2]
    past = cache_k.shape[2]

    x = jnp.concatenate([x_prompt.reshape(rows.t_ctx, d), x_sample.reshape(rows.t_lat, d)], axis=0)
    n_vec = -(-(1 + n_lat) // 8) * 8
    cvecs = jnp.zeros((n_vec, d), F32).at[0].set(c_ctx).at[1:1 + n_lat].set(c)
    mod = _modulation(cvecs, w_mod, b_mod).reshape(depth, n_vec, N_MOD, d)

    ks, vs, sf, sb = [], [], [], []
    for l in range(depth):
        j = l // 2
        if l % 2 == 0:
            lam_init = 0.8 - 0.6 * math.exp(-0.3 * l)
            splits = [d_ssm, d_ssm + xbc_cols, d_ssm + xbc_cols + 2 * heads]
            splits += [splits[-1] + qk_cols, splits[-1] + 2 * qk_cols]
            wz, wxbc, wdt, wq, wk, wv = [w.astype(BF16) for w in jnp.split(w_in[j], splits, axis=1)]
            zT, xbc, dt, dtT, qs, kr, vt, k_ctx, v_ctx = _inproj(rows, x, mod[l], norm_mix[l],
                                                                 (wz.T, wxbc, wdt, wdt.T, wq, wk, wv))
            xsT, bm, cm = _conv(rows, xbc, conv_w[j], conv_b[j], d_ssm)
            hf0 = state_ssm_fwd[:, j].reshape(n_lat, heads * SSM_HEAD_DIM, SSM_STATE)
            hb0 = state_ssm_bwd[:, j].reshape(n_lat, heads * SSM_HEAD_DIM, SSM_STATE)
            yfT, ybT, hf, hb = _ssd(rows, xsT, bm, cm, dt, dtT, dt_bias[j], a_log[j], hf0, hb0)
            o_ctx = _attention(qs, kr, vt, lam_vec[j], subln_gain[j], lam_init,
                               n_seq=n_ctx, length=len_ctx, row0=0)
            ck = cache_k[:, j].reshape(n_lat, past, qk_cols).astype(BF16)
            cvt = jnp.transpose(cache_v[:, j], (0, 2, 3, 1))
            cvt = jnp.concatenate([cvt, jnp.ones((n_lat, da_heads, V_ROWS - DA_V_DIM, past), F32)], axis=2)
            cvt = cvt.reshape(n_lat, da_heads * V_ROWS, past).astype(BF16)
            o_lat = _attention(qs, kr, vt, lam_vec[j], subln_gain[j], lam_init,
                               n_seq=n_lat, length=len_lat, row0=rows.t_ctx, cache_k=ck, cache_vt=cvt)
            dsk = jnp.repeat(d_skip[j], SSM_HEAD_DIM, axis=1)
            wo = w_out[j].astype(BF16)
            x = _outproj(rows, yfT, ybT, xsT, zT, o_ctx, o_lat, x, mod[l], dsk, ssm_gain[j],
                         wo[:d_ssm], wo[d_ssm:])
            ks.append(k_ctx.reshape(n_ctx, len_ctx, da_heads, 2, DA_QK_DIM))
            vs.append(v_ctx.reshape(n_ctx, len_ctx, da_heads, DA_V_DIM))
            sf.append(hf[:n_ctx].reshape(n_ctx, heads, SSM_HEAD_DIM, SSM_STATE))
            sb.append(hb[:n_ctx].reshape(n_ctx, heads, SSM_HEAD_DIM, SSM_STATE))
        else:
            x = _pool(rows, x, mod[l], norm_mix[l], pool_w[j].astype(BF16), pool_b[j], pool_scale[j])
        x = _moe(rows, x, mod[l], norm_ffn[l], w_coarse[l], b_coarse[l], w_fine[l], b_fine[l],
                 w_gate, w_up, w_down, l)

    y_prompt = _final_norm(x, norm_final, 0, rows.t_ctx).reshape(n_ctx, len_ctx, d)
    y_sample = _final_norm(x, norm_final, rows.t_ctx, rows.t_lat).reshape(n_lat, len_lat, d)
    return (y_prompt, y_sample, jnp.stack(ks, axis=1), jnp.stack(vs, axis=1),
            jnp.stack(sf, axis=1), jnp.stack(sb, axis=1))
```

```python
import functools
import math

import jax
import jax.numpy as jnp
from jax import lax
from jax.experimental import pallas as pl
from jax.experimental.pallas import tpu as pltpu

F32 = jnp.float32
BF16 = jnp.bfloat16
I32 = jnp.int32
U32 = jnp.uint32
HIGHEST = lax.Precision.HIGHEST

EPS = 1e-6
N_MOD = 6
SSM_HEAD_DIM = 64
SSM_GROUPS = 4
SSM_STATE = 128
CHUNK = 128
CONV_WIDTH = 5
CONV_HALO = 8
DA_QK_DIM = 64
DA_V_DIM = 128
GRID_W = 64
ROPE_FREQS = 16
ROPE_THETA = 10000.0
POOL_WINDOWS = (2, 4, 8, 16)
MOE_GROUPS = 4
EXPERTS_PER_GROUP = 8
MOE_TOP_K = 2

TOKEN_TILE = 256
MOE_TILE = 1024
EXPERT_ROWS = 512
ATTN_KEY_BLOCK = 512
ATTN_QUERY_TILE = 1024
V_ROWS = DA_V_DIM + 16
LOG2_E = 1.4426950408889634
DMA_ISSUE_UNROLL = 8
ROUTER_LANES = 128
VMEM_LIMIT_BYTES = 56 * 1024 * 1024

NT_DIMS = (((1,), (1,)), ((), ()))
TN_DIMS = (((0,), (0,)), ((), ()))


def _params(*sem):
    return pltpu.CompilerParams(dimension_semantics=sem, vmem_limit_bytes=VMEM_LIMIT_BYTES)


def _silu(x):
    return x * jax.nn.sigmoid(x)


def _softplus(x):
    return jnp.maximum(x, 0.0) + jnp.log1p(jnp.exp(-jnp.abs(x)))


def _rms(x, g):
    return x * lax.rsqrt(jnp.mean(x * x, axis=-1, keepdims=True) + EPS) * g


def _modnorm(x, g, scale, shift):
    return _rms(x, g) * (1.0 + scale) + shift


def _bdot(a, b):
    return jnp.dot(a, b, preferred_element_type=F32)


class _Rows:
    def __init__(self, n_ctx, len_ctx, n_lat, len_lat):
        self.n_ctx, self.len_ctx, self.n_lat, self.len_lat = n_ctx, len_ctx, n_lat, len_lat
        self.t_ctx = n_ctx * len_ctx
        self.t_lat = n_lat * len_lat
        self.t = self.t_ctx + self.t_lat
        assert len_ctx % TOKEN_TILE == 0 and len_lat % TOKEN_TILE == 0
        self.ctx_tiles = self.t_ctx // TOKEN_TILE
        self.tiles = self.t // TOKEN_TILE
        self.tiles_per_ctx = len_ctx // TOKEN_TILE
        self.tiles_per_lat = len_lat // TOKEN_TILE

    def mod_row(self, i, tile=TOKEN_TILE):
        assert self.t_ctx % tile == 0 and self.len_lat % tile == 0
        ctx_tiles = self.t_ctx // tile
        return jnp.where(i < ctx_tiles, 0, 1 + (i - ctx_tiles) // (self.len_lat // tile))

    def seq_pos(self, i):
        is_ctx = i < self.ctx_tiles
        pos = jnp.where(is_ctx, i % self.tiles_per_ctx, (i - self.ctx_tiles) % self.tiles_per_lat)
        n = jnp.where(is_ctx, self.tiles_per_ctx, self.tiles_per_lat)
        return pos, n


def _row_spec(width, tile=TOKEN_TILE):
    return pl.BlockSpec((tile, width), lambda i: (i, 0))


def _const_spec(shape):
    nd = len(shape)
    return pl.BlockSpec(shape, lambda i: (0,) * nd)


def _mod_kernel(c_ref, w_ref, b_ref, o_ref):
    s = _silu(c_ref[...])
    o_ref[0] = _bdot(s.astype(BF16), w_ref[0].astype(BF16)) + b_ref[0]


def _modulation(cvecs, w_mod, b_mod):
    depth, d, n = w_mod.shape
    tn = n // 4
    return pl.pallas_call(
        _mod_kernel,
        grid=(depth, n // tn),
        in_specs=[pl.BlockSpec(cvecs.shape, lambda l, j: (0, 0)),
                  pl.BlockSpec((1, d, tn), lambda l, j: (l, 0, j)),
                  pl.BlockSpec((1, 1, tn), lambda l, j: (l, 0, j))],
        out_specs=pl.BlockSpec((1, cvecs.shape[0], tn), lambda l, j: (l, 0, j)),
        out_shape=jax.ShapeDtypeStruct((depth, cvecs.shape[0], n), F32),
        compiler_params=_params("parallel", "parallel"),
        name="modulation",
    )(cvecs, w_mod, b_mod.reshape(depth, 1, n))


def _inproj_kernel(x_ref, mod_ref, g_ref, cos_ref, sa_ref, sb_ref, wzT, wxbc, wdt, wdtT, wq, wk, wv,
                   zT_o, xbc_o, dt_o, dtT_o, qs_o, kr_o, vt_o, k_o, v_o, *, ctx_tiles):
    is_ctx = pl.program_id(0) < ctx_tiles
    m = mod_ref[0]
    h = _modnorm(x_ref[...], g_ref[...], m[1:2], m[0:1]).astype(BF16)
    q = _bdot(h, wq[...])
    k = _bdot(h, wk[...])
    v = _bdot(h, wv[...])
    zT_o[...] = lax.dot_general(wzT[...], h, NT_DIMS, preferred_element_type=F32)
    xbc_o[...] = _bdot(h, wxbc[...])
    dt_o[...] = _bdot(h, wdt[...])
    dtT_o[...] = lax.dot_general(wdtT[...], h, NT_DIMS, preferred_element_type=F32)

    cos = jnp.where(is_ctx, 1.0, cos_ref[...])
    sa = jnp.where(is_ctx, 0.0, sa_ref[...])
    sb = jnp.where(is_ctx, 0.0, sb_ref[...])
    tile = q.shape[0]
    first_map = lax.broadcasted_iota(I32, (tile, DA_V_DIM), 1) < DA_QK_DIM
    scale = LOG2_E / math.sqrt(DA_QK_DIM)

    def rope(t):
        return (t * cos + pltpu.roll(t, DA_V_DIM - ROPE_FREQS, 1) * sa + pltpu.roll(t, ROPE_FREQS, 1) * sb)

    for hd in range(q.shape[1] // DA_V_DIM):
        sl = slice(hd * DA_V_DIM, (hd + 1) * DA_V_DIM)
        qr = rope(q[:, sl]) * scale
        qs_o[0, :, sl] = jnp.where(first_map, qr, 0.0).astype(BF16)
        qs_o[1, :, sl] = jnp.where(first_map, 0.0, qr).astype(BF16)
        kr_o[:, sl] = rope(k[:, sl]).astype(BF16)
        r0 = hd * V_ROWS
        vt_o[r0:r0 + DA_V_DIM, :] = v[:, sl].T.astype(BF16)
        vt_o[r0 + DA_V_DIM:r0 + V_ROWS, :] = jnp.ones((V_ROWS - DA_V_DIM, tile), BF16)

    @pl.when(is_ctx)
    def _():
        k_o[...] = k
        v_o[...] = v


def _rope_tables(length):
    t = jnp.arange(length)
    pos = jnp.stack([t // GRID_W, t % GRID_W], axis=-1).astype(F32)
    inv = ROPE_THETA ** (-jnp.arange(ROPE_FREQS, dtype=F32) / ROPE_FREQS)
    ang = pos[..., None] * inv
    lane = jnp.arange(DA_V_DIM)
    axis = (lane % DA_QK_DIM) // (2 * ROPE_FREQS)
    freq = lane % ROPE_FREQS
    second_half = (lane % (2 * ROPE_FREQS)) >= ROPE_FREQS
    a = ang[:, axis, freq]
    cos, sin = jnp.cos(a), jnp.sin(a)
    return cos, jnp.where(second_half, 0.0, -sin), jnp.where(second_half, sin, 0.0)


def _inproj(rows, x, mod_l, g, weights):
    t, d = x.shape
    wzT, wxbc, wdt, wdtT, wq, wk, wv = weights
    w = wq.shape[1]
    vt_rows = wv.shape[1] // DA_V_DIM * V_ROWS
    tables = _rope_tables(rows.len_lat)

    def tab(i):
        return (jnp.maximum(i - rows.ctx_tiles, 0) % rows.tiles_per_lat, 0)

    def col_spec(n):
        return pl.BlockSpec((n, TOKEN_TILE), lambda i: (0, i))

    ctx_spec = pl.BlockSpec((TOKEN_TILE, w), lambda i: (jnp.minimum(i, rows.ctx_tiles - 1), 0))
    return pl.pallas_call(
        functools.partial(_inproj_kernel, ctx_tiles=rows.ctx_tiles),
        grid=(rows.tiles,),
        in_specs=[_row_spec(d),
                  pl.BlockSpec((1, N_MOD, d), lambda i: (rows.mod_row(i), 0, 0)),
                  _const_spec((1, d))] + [pl.BlockSpec((TOKEN_TILE, DA_V_DIM), tab)] * 3
                 + [_const_spec(wt.shape) for wt in weights],
        out_specs=[col_spec(wzT.shape[0]), _row_spec(wxbc.shape[1]), _row_spec(wdt.shape[1]),
                   col_spec(wdtT.shape[0]), pl.BlockSpec((2, TOKEN_TILE, w), lambda i: (0, i, 0)),
                   _row_spec(w), col_spec(vt_rows), ctx_spec, ctx_spec],
        out_shape=[jax.ShapeDtypeStruct((wzT.shape[0], t), F32), jax.ShapeDtypeStruct((t, wxbc.shape[1]), F32),
                   jax.ShapeDtypeStruct((t, wdt.shape[1]), F32), jax.ShapeDtypeStruct((wdtT.shape[0], t), F32),
                   jax.ShapeDtypeStruct((2, t, w), BF16), jax.ShapeDtypeStruct((t, w), BF16),
                   jax.ShapeDtypeStruct((vt_rows, t), BF16),
                   jax.ShapeDtypeStruct((rows.t_ctx, w), F32), jax.ShapeDtypeStruct((rows.t_ctx, w), F32)],
        compiler_params=_params("arbitrary"),
        name="inproj",
    )(x, mod_l, g.reshape(1, d), *tables, *weights)


def _conv_kernel(cur_ref, prev_ref, next_ref, w_ref, b_ref, xsT_o, b_o, c_o, ext_ref, *, rows, d_ssm, gn):
    i = pl.program_id(0)
    pos, n = rows.seq_pos(i)
    tile = cur_ref.shape[0]
    ext_ref[0:CONV_HALO, :] = jnp.where(pos == 0, 0.0, prev_ref[...])
    ext_ref[CONV_HALO:CONV_HALO + tile, :] = cur_ref[...]
    ext_ref[CONV_HALO + tile:, :] = jnp.where(pos == n - 1, 0.0, next_ref[...])
    width = cur_ref.shape[1]
    step = 512
    for c0 in range(0, width, step):
        acc = jnp.broadcast_to(b_ref[:, c0:c0 + step], (tile, step))
        for k in range(CONV_WIDTH):
            off = CONV_HALO - CONV_WIDTH // 2 + k
            acc = acc + w_ref[k:k + 1, c0:c0 + step] * ext_ref[off:off + tile, c0:c0 + step]
        y = _silu(acc)
        if c0 < d_ssm:
            xsT_o[c0:c0 + step, :] = y.T
        elif c0 < d_ssm + gn:
            b_o[:, c0 - d_ssm:c0 - d_ssm + step] = y.astype(BF16)
        else:
            c_o[:, c0 - d_ssm - gn:c0 - d_ssm - gn + step] = y.astype(BF16)


def _conv(rows, xbc, conv_w, conv_b, d_ssm):
    t, width = xbc.shape
    gn = (width - d_ssm) // 2
    hb = TOKEN_TILE // CONV_HALO
    last = t // CONV_HALO - 1
    return pl.pallas_call(
        functools.partial(_conv_kernel, rows=rows, d_ssm=d_ssm, gn=gn),
        grid=(rows.tiles,),
        in_specs=[_row_spec(width),
                  pl.BlockSpec((CONV_HALO, width), lambda i: (jnp.maximum(i * hb - 1, 0), 0)),
                  pl.BlockSpec((CONV_HALO, width), lambda i: (jnp.minimum((i + 1) * hb, last), 0)),
                  _const_spec(conv_w.shape), _const_spec((1, width))],
        out_specs=[pl.BlockSpec((d_ssm, TOKEN_TILE), lambda i: (0, i)), _row_spec(gn), _row_spec(gn)],
        out_shape=[jax.ShapeDtypeStruct((d_ssm, t), F32),
                   jax.ShapeDtypeStruct((t, gn), BF16),
                   jax.ShapeDtypeStruct((t, gn), BF16)],
        scratch_shapes=[pltpu.VMEM((TOKEN_TILE + 2 * CONV_HALO, width), F32)],
        compiler_params=_params("parallel"),
        name="conv",
    )(xbc, xbc, xbc, conv_w, conv_b.reshape(1, width))


def _contributes(q, reverse, target_rows):
    r = lax.broadcasted_iota(I32, (q, q), 0)
    c = lax.broadcasted_iota(I32, (q, q), 1)
    tgt, src = (r, c) if target_rows else (c, r)
    return (src >= tgt) if reverse else (src <= tgt)


def _ssd_direction(xT_ref, b_ref, c_ref, dt_c, dt_r, bias_row, bias_col, a_row, a_col,
                   h_refs, yT_ref, reverse):
    q = CHUNK
    heads = a_row.shape[1]
    dt_c = _softplus(dt_c + bias_row)
    dt_r = _softplus(dt_r + bias_col)
    keep_t = _contributes(q, reverse, False)
    acs_c = jnp.dot(_contributes(q, reverse, True).astype(F32), dt_c * a_row,
                    precision=HIGHEST, preferred_element_type=F32)
    acs_r = jnp.dot(dt_r * a_col, keep_t.astype(F32), precision=HIGHEST, preferred_element_type=F32)
    last = 0 if reverse else q - 1
    total_r = acs_r[:, last:last + 1]
    grow_r = jnp.exp(acs_r)
    to_end_r = jnp.exp(total_r - acs_r)
    carry_r = jnp.exp(total_r)
    per_group = heads // SSM_GROUPS
    p = SSM_HEAD_DIM

    def operands(g):
        bg = b_ref[:, g * SSM_STATE:(g + 1) * SSM_STATE]
        cg = c_ref[:, g * SSM_STATE:(g + 1) * SSM_STATE]
        cb_t = lax.dot_general(bg, cg, NT_DIMS, preferred_element_type=F32)
        h_in = h_refs[g][...]
        y_off = lax.dot_general(h_in.astype(BF16), cg, NT_DIMS, preferred_element_type=F32)
        return bg, cb_t, h_in, y_off

    def group(g, ops):
        bg, cb_t, h_in, y_off = ops
        xw, h_decayed = [], []
        for r in range(per_group):
            h = g * per_group + r
            rs = slice(r * p, (r + 1) * p)
            seg = acs_r[h:h + 1, :] - acs_c[:, h:h + 1]
            decay = jnp.exp(jnp.where(keep_t, seg, -jnp.inf))
            l_t = (cb_t * decay).astype(BF16)
            xdt = xT_ref[h * p:(h + 1) * p, :] * dt_r[h:h + 1, :]
            yT_ref[h * p:(h + 1) * p, :] = _bdot(xdt.astype(BF16), l_t) + y_off[rs, :] * grow_r[h:h + 1, :]
            xw.append((xdt * to_end_r[h:h + 1, :]).astype(BF16))
            h_decayed.append(h_in[rs, :] * carry_r[h:h + 1, :])
        state = _bdot(jnp.concatenate(xw, axis=0), bg)
        h_refs[g][...] = jnp.concatenate(h_decayed, axis=0) + state

    return operands, group


def _ssd_kernel(xf, bf, cf, dtf, dtTf, xb, bb, cb, dtb, dtTb, bias_ref, biasT_ref, alog_ref, alogT_ref,
                hf0, hb0, yf_o, yb_o, hf_o, hb_o, *scratch, geom):
    s = pl.program_id(0)
    is_ctx, _, c, nc, _ = geom(s)
    heads = alog_ref.shape[1]
    hf_scr, hb_scr = scratch[:SSM_GROUPS], scratch[SSM_GROUPS:]
    rows_g = hf_scr[0].shape[0]

    @pl.when(c == 0)
    def _():
        for g in range(SSM_GROUPS):
            hf_scr[g][...] = jnp.where(is_ctx, 0.0, hf0[0, g * rows_g:(g + 1) * rows_g, :])
            hb_scr[g][...] = jnp.where(is_ctx, 0.0, hb0[0, g * rows_g:(g + 1) * rows_g, :])

    a_row = -jnp.exp(alog_ref[...])
    a_col = -jnp.exp(alogT_ref[...])
    fwd = _ssd_direction(xf, bf, cf, dtf[:, 0:heads], dtTf[0:heads, :], bias_ref[0:1, :], biasT_ref[:, 0:1],
                         a_row[0:1, :], a_col[:, 0:1], hf_scr, yf_o, False)
    bwd = _ssd_direction(xb, bb, cb, dtb[:, heads:2 * heads], dtTb[heads:2 * heads, :], bias_ref[1:2, :],
                         biasT_ref[:, 1:2], a_row[1:2, :], a_col[:, 1:2], hb_scr, yb_o, True)
    (f_ops, f_group), (b_ops, b_group) = fwd, bwd
    ops = [(f_ops(g), b_ops(g)) for g in range(SSM_GROUPS)]
    for g in range(SSM_GROUPS):
        f_group(g, ops[g][0])
        b_group(g, ops[g][1])

    @pl.when(c == nc - 1)
    def _():
        for g in range(SSM_GROUPS):
            hf_o[0, g * rows_g:(g + 1) * rows_g, :] = hf_scr[g][...]
            hb_o[0, g * rows_g:(g + 1) * rows_g, :] = hb_scr[g][...]


def _ssd(rows, xsT, bm, cm, dt, dtT, dt_bias, a_log, hf0, hb0):
    d_ssm, t = xsT.shape
    gn = bm.shape[1]
    heads2 = dt.shape[1]
    ncc, ncl = rows.len_ctx // CHUNK, rows.len_lat // CHUNK
    s_ctx, s_lat = rows.n_ctx * ncc, rows.n_lat * ncl
    n_seq = rows.n_ctx + rows.n_lat

    def geom(s):
        is_ctx = s < s_ctx
        sl = s - s_ctx
        seq = jnp.where(is_ctx, s // ncc, rows.n_ctx + sl // ncl)
        c = jnp.where(is_ctx, s % ncc, sl % ncl)
        nc = jnp.where(is_ctx, ncc, ncl)
        base = jnp.where(is_ctx, (s // ncc) * ncc, s_ctx + (sl // ncl) * ncl)
        return is_ctx, seq, c, nc, base

    def fwd(s):
        _, _, c, _, base = geom(s)
        return base + c

    def bwd(s):
        _, _, c, nc, base = geom(s)
        return base + nc - 1 - c

    def lat_seq(s):
        return jnp.maximum(geom(s)[1] - rows.n_ctx, 0)

    def side(blk):
        return [pl.BlockSpec((d_ssm, CHUNK), lambda s: (0, blk(s))),
                pl.BlockSpec((CHUNK, gn), lambda s: (blk(s), 0)),
                pl.BlockSpec((CHUNK, gn), lambda s: (blk(s), 0)),
                pl.BlockSpec((CHUNK, heads2), lambda s: (blk(s), 0)),
                pl.BlockSpec((heads2, CHUNK), lambda s: (0, blk(s)))]

    state_shape = (SSM_HEAD_DIM * (heads2 // 2), SSM_STATE)
    group_shape = (state_shape[0] // SSM_GROUPS, SSM_STATE)
    state_spec_in = pl.BlockSpec((1,) + state_shape, lambda s: (lat_seq(s), 0, 0))
    state_spec_out = pl.BlockSpec((1,) + state_shape, lambda s: (geom(s)[1], 0, 0))
    return pl.pallas_call(
        functools.partial(_ssd_kernel, geom=geom),
        grid=(s_ctx + s_lat,),
        in_specs=side(fwd) + side(bwd) + [_const_spec(dt_bias.shape), _const_spec(dt_bias.T.shape),
                                          _const_spec(a_log.shape), _const_spec(a_log.T.shape),
                                          state_spec_in, state_spec_in],
        out_specs=[pl.BlockSpec((d_ssm, CHUNK), lambda s: (0, fwd(s))),
                   pl.BlockSpec((d_ssm, CHUNK), lambda s: (0, bwd(s))),
                   state_spec_out, state_spec_out],
        out_shape=[jax.ShapeDtypeStruct((d_ssm, t), F32), jax.ShapeDtypeStruct((d_ssm, t), F32),
                   jax.ShapeDtypeStruct((n_seq,) + state_shape, F32),
                   jax.ShapeDtypeStruct((n_seq,) + state_shape, F32)],
        scratch_shapes=[pltpu.VMEM(group_shape, F32)] * (2 * SSM_GROUPS),
        compiler_params=_params("arbitrary"),
        name="ssd",
    )(xsT, bm, cm, dt, dtT, xsT, bm, cm, dt, dtT, dt_bias, dt_bias.T, a_log, a_log.T, hf0, hb0)


def _attn_head(q, blocks, tq, tk, lam, gain, lam_init):
    m = jnp.full((1, 2 * tq), -jnp.inf, F32)
    acc = jnp.zeros((V_ROWS, 2 * tq), F32)

    def scores(blk):
        return lax.dot_general(blk[0](), q, NT_DIMS, preferred_element_type=F32)

    s_next = scores(blocks[0])
    for n, (_, load_vt) in enumerate(blocks):
        s = s_next
        if n + 1 < len(blocks):
            s_next = scores(blocks[n + 1])
        m_new = jnp.maximum(m, jnp.max(s, axis=0, keepdims=True))
        acc = jnp.exp2(m - m_new) * acc + _bdot(load_vt(), jnp.exp2(s - m_new).astype(BF16))
        m = m_new
    on = acc[0:DA_V_DIM, :] / acc[DA_V_DIM:DA_V_DIM + 1, :]
    ot = on[:, 0:tq] - lam * on[:, tq:2 * tq]
    ot = ot * lax.rsqrt(jnp.mean(ot * ot, axis=0, keepdims=True) + EPS) * gain * (1.0 - lam_init)
    return ot.T


def _attn_kernel(*refs, tq, tk, has_cache, heads_per_step, lam_init):
    if has_cache:
        qs_ref, kc_ref, vct_ref, k_ref, vt_ref, lamv_ref, gain_ref, o_ref = refs
    else:
        qs_ref, k_ref, vt_ref, lamv_ref, gain_ref, o_ref = refs
    lv = lamv_ref[...]
    lam = (jnp.exp(jnp.sum(lv[0:1] * lv[1:2], axis=1, keepdims=True))
           - jnp.exp(jnp.sum(lv[2:3] * lv[3:4], axis=1, keepdims=True)) + lam_init)
    for hd in range(heads_per_step):
        cols = slice(hd * DA_V_DIM, (hd + 1) * DA_V_DIM)
        vrows = slice(hd * V_ROWS, (hd + 1) * V_ROWS)
        blocks = []
        if has_cache:
            for c in range(kc_ref.shape[1] // tk):
                ks = slice(c * tk, (c + 1) * tk)
                blocks.append((lambda ks=ks, cols=cols: kc_ref[0, ks, cols],
                               lambda ks=ks, vrows=vrows: vct_ref[0, vrows, ks]))
        for c in range(k_ref.shape[0] // tk):
            ks = slice(c * tk, (c + 1) * tk)
            blocks.append((lambda ks=ks, cols=cols: k_ref[ks, cols], lambda ks=ks, vrows=vrows: vt_ref[vrows, ks]))
        q = qs_ref[:, :, cols].reshape(2 * tq, DA_V_DIM)
        o_ref[:, cols] = _attn_head(q, blocks, tq, tk, lam, gain_ref[...], lam_init).astype(BF16)


def _attention(qs, kr, vt, lam_vec, gain, lam_init, *, n_seq, length, row0, cache_k=None, cache_vt=None):
    w = kr.shape[1]
    heads = w // DA_V_DIM
    tq = min(length, ATTN_QUERY_TILE)
    tk = min(length, ATTN_KEY_BLOCK)
    assert length % tq == 0 and length % tk == 0 and row0 % length == 0
    nq = length // tq
    has_cache = cache_k is not None
    hps = heads if length <= tk else 1
    q_blk0, seq0 = row0 // tq, row0 // length

    in_specs = [pl.BlockSpec((2, tq, hps * DA_V_DIM), lambda b, h, i: (0, q_blk0 + b * nq + i, h))]
    args = [qs]
    if has_cache:
        past = cache_k.shape[1]
        assert past % tk == 0
        in_specs += [pl.BlockSpec((1, past, hps * DA_V_DIM), lambda b, h, i: (b, 0, h)),
                     pl.BlockSpec((1, hps * V_ROWS, past), lambda b, h, i: (b, h, 0))]
        args += [cache_k, cache_vt]
    in_specs += [pl.BlockSpec((length, hps * DA_V_DIM), lambda b, h, i: (seq0 + b, h)),
                 pl.BlockSpec((hps * V_ROWS, length), lambda b, h, i: (h, seq0 + b)),
                 pl.BlockSpec(lam_vec.shape, lambda b, h, i: (0, 0)),
                 pl.BlockSpec((DA_V_DIM, 1), lambda b, h, i: (0, 0))]
    args += [kr, vt, lam_vec, gain.reshape(DA_V_DIM, 1)]
    return pl.pallas_call(
        functools.partial(_attn_kernel, tq=tq, tk=tk, has_cache=has_cache, heads_per_step=hps,
                          lam_init=lam_init),
        grid=(n_seq, heads // hps, nq),
        in_specs=in_specs,
        out_specs=pl.BlockSpec((tq, hps * DA_V_DIM), lambda b, h, i: (b * nq + i, h)),
        out_shape=jax.ShapeDtypeStruct((n_seq * length, w), BF16),
        compiler_params=_params("parallel", "parallel", "arbitrary"),
        name="attention_lat" if has_cache else "attention_ctx",
    )(*args)


def _outproj_kernel(yfT_ref, ybT_ref, xsT_ref, zT_ref, oc_ref, ol_ref, x_ref, mod_ref, dsk_ref, gain_ref,
                    wy_ref, wo_ref, out_ref, *, ctx_tiles):
    is_ctx = pl.program_id(0) < ctx_tiles
    y = yfT_ref[...] + ybT_ref[...] + (dsk_ref[0] + dsk_ref[1]) * xsT_ref[...]
    y = y * _silu(zT_ref[...])
    y = (y * lax.rsqrt(jnp.mean(y * y, axis=0, keepdims=True) + EPS) * gain_ref[...]).astype(BF16)
    o = jnp.where(is_ctx, oc_ref[...], ol_ref[...])
    mixed = lax.dot_general(y, wy_ref[...], TN_DIMS, preferred_element_type=F32) + _bdot(o, wo_ref[...])
    out_ref[...] = x_ref[...] + mod_ref[0][2:3] * mixed


def _outproj(rows, yfT, ybT, xsT, zT, o_ctx, o_lat, x, mod_l, d_skip, gain, wy, wo):
    t, d = x.shape
    d_ssm = xsT.shape[0]
    wa = o_ctx.shape[1]
    col_spec = pl.BlockSpec((d_ssm, TOKEN_TILE), lambda i: (0, i))
    return pl.pallas_call(
        functools.partial(_outproj_kernel, ctx_tiles=rows.ctx_tiles),
        grid=(rows.tiles,),
        in_specs=[col_spec] * 4 + [
            pl.BlockSpec((TOKEN_TILE, wa), lambda i: (jnp.minimum(i, rows.ctx_tiles - 1), 0)),
            pl.BlockSpec((TOKEN_TILE, wa), lambda i: (jnp.maximum(i - rows.ctx_tiles, 0), 0)),
            _row_spec(d),
            pl.BlockSpec((1, N_MOD, d), lambda i: (rows.mod_row(i), 0, 0)),
            _const_spec((2, d_ssm, 1)), _const_spec((d_ssm, 1)), _const_spec(wy.shape), _const_spec(wo.shape)],
        out_specs=_row_spec(d),
        out_shape=jax.ShapeDtypeStruct((t, d), F32),
        compiler_params=_params("parallel"),
        name="outproj",
    )(yfT, ybT, xsT, zT, o_ctx, o_lat, x, mod_l, d_skip.reshape(2, d_ssm, 1), gain.reshape(d_ssm, 1), wy, wo)


def _pool_kernel(cur_ref, prev_ref, next_ref, mod_ref, g_ref, w_ref, b_ref, sc_ref, out_ref, ext_ref,
                 *, rows, len_ctx, len_lat):
    i = pl.program_id(0)
    pos, n = rows.seq_pos(i)
    tile = cur_ref.shape[0]
    m = mod_ref[0]
    g = g_ref[...]

    def norm(x):
        return _modnorm(x, g, m[1:2], m[0:1])

    x = cur_ref[...]
    ext_ref[0:CONV_HALO, :] = jnp.where(pos == 0, 0.0, norm(prev_ref[...]))
    ext_ref[CONV_HALO:CONV_HALO + tile, :] = norm(x)
    ext_ref[CONV_HALO + tile:, :] = jnp.where(pos == n - 1, 0.0, norm(next_ref[...]))
    length = jnp.where(i < rows.ctx_tiles, len_ctx, len_lat)
    tpos = pos * tile + lax.broadcasted_iota(I32, (tile, 1), 0)
    pg = cur_ref.shape[1] // len(POOL_WINDOWS)
    for gi, win in enumerate(POOL_WINDOWS):
        cs = slice(gi * pg, (gi + 1) * pg)
        half = win // 2
        acc = ext_ref[CONV_HALO - half:CONV_HALO - half + tile, cs]
        for off in range(1 - half, half):
            acc = acc + ext_ref[CONV_HALO + off:CONV_HALO + off + tile, cs]
        cnt = jnp.minimum(tpos + half, length) - jnp.maximum(tpos - half, 0)
        pooled = acc / cnt.astype(F32) - ext_ref[CONV_HALO:CONV_HALO + tile, cs]
        mixed = (_bdot(pooled.astype(BF16), w_ref[gi]) + b_ref[gi:gi + 1, :]) * sc_ref[:, cs]
        out_ref[:, cs] = x[:, cs] + m[2:3, cs] * mixed


def _pool(rows, x, mod_l, g, pool_w, pool_b, pool_scale):
    t, d = x.shape
    hb = TOKEN_TILE // CONV_HALO
    last = t // CONV_HALO - 1
    return pl.pallas_call(
        functools.partial(_pool_kernel, rows=rows, len_ctx=rows.len_ctx, len_lat=rows.len_lat),
        grid=(rows.tiles,),
        in_specs=[_row_spec(d),
                  pl.BlockSpec((CONV_HALO, d), lambda i: (jnp.maximum(i * hb - 1, 0), 0)),
                  pl.BlockSpec((CONV_HALO, d), lambda i: (jnp.minimum((i + 1) * hb, last), 0)),
                  pl.BlockSpec((1, N_MOD, d), lambda i: (rows.mod_row(i), 0, 0)),
                  _const_spec((1, d)), _const_spec(pool_w.shape), _const_spec(pool_b.shape),
                  _const_spec((1, d))],
        out_specs=_row_spec(d),
        out_shape=jax.ShapeDtypeStruct((t, d), F32),
        scratch_shapes=[pltpu.VMEM((TOKEN_TILE + 2 * CONV_HALO, d), F32)],
        compiler_params=_params("parallel"),
        name="pool",
    )(x, x, x, mod_l, g.reshape(1, d), pool_w, pool_b, pool_scale.reshape(1, d))


def _router_kernel(x_ref, mod_ref, g_ref, w_ref, bias_ref, ri_o, rf_o, cnt_o, cnt_scr,
                   *, n_experts):
    i = pl.program_id(0)

    @pl.when(i == 0)
    def _():
        cnt_scr[...] = jnp.zeros(cnt_scr.shape, F32)

    m = mod_ref[0]
    h = _modnorm(x_ref[...], g_ref[...], m[4:5], m[3:4])
    tm = h.shape[0]
    lg = lax.dot_general(w_ref[...], h.astype(BF16), NT_DIMS, preferred_element_type=F32) + bias_ref[...]
    row = lax.broadcasted_iota(I32, (n_experts, tm), 0)
    big = jnp.int32(n_experts + MOE_GROUPS)
    lc = lg[n_experts:n_experts + 8, :]
    crow = lax.broadcasted_iota(I32, (8, tm), 0)
    lc = jnp.where(crow < MOE_GROUPS, lc, -jnp.inf)
    mc = jnp.max(lc, axis=0, keepdims=True)
    group_p = 1.0 / jnp.sum(jnp.exp(lc - mc), axis=0, keepdims=True)
    g_idx = jnp.min(jnp.where(lc == mc, crow, big), axis=0, keepdims=True)
    lf = jnp.where(row // EXPERTS_PER_GROUP == g_idx, lg[0:n_experts, :], -jnp.inf)
    m1 = jnp.max(lf, axis=0, keepdims=True)
    e1 = jnp.min(jnp.where(lf == m1, row, big), axis=0, keepdims=True)
    lf2 = jnp.where(row == e1, -jnp.inf, lf)
    m2 = jnp.max(lf2, axis=0, keepdims=True)
    e2 = jnp.min(jnp.where(lf2 == m2, row, big), axis=0, keepdims=True)
    r2 = jnp.exp(m2 - m1)
    gate1 = group_p / (1.0 + r2)
    gate2 = group_p * r2 / (1.0 + r2)
    oh1 = (row == e1)
    oh2 = (row == e2)
    oh = jnp.where(oh1 | oh2, 1.0, 0.0)
    before = (lax.broadcasted_iota(I32, (tm, tm), 0) < lax.broadcasted_iota(I32, (tm, tm), 1))
    prior = _bdot(oh.astype(BF16), before.astype(BF16)) + cnt_scr[...]
    rank1 = jnp.sum(jnp.where(oh1, prior, 0.0), axis=0, keepdims=True)
    rank2 = jnp.sum(jnp.where(oh2, prior, 0.0), axis=0, keepdims=True)
    cnt_scr[...] = cnt_scr[...] + jnp.sum(oh, axis=1, keepdims=True)
    zi = jnp.zeros((4, tm), I32)
    ri_o[0] = jnp.concatenate([e1, e2, rank1.astype(I32), rank2.astype(I32), zi], axis=0)
    rf_o[0] = jnp.concatenate([gate1, gate2, jnp.zeros((6, tm), F32)], axis=0)
    cnt_o[...] = jnp.broadcast_to(cnt_scr[...], cnt_o.shape)


def _router(rows, x, mod_l, g, w_t, bias, n_experts):
    t, d = x.shape
    tm = MOE_TILE
    nt = t // tm
    return pl.pallas_call(
        functools.partial(_router_kernel, n_experts=n_experts),
        grid=(nt,),
        in_specs=[_row_spec(d, tm), pl.BlockSpec((1, N_MOD, d), lambda i: (rows.mod_row(i, tm), 0, 0)),
                  _const_spec((1, d)), _const_spec(w_t.shape), _const_spec(bias.shape)],
        out_specs=[pl.BlockSpec((1, 8, tm), lambda i: (i, 0, 0)),
                   pl.BlockSpec((1, 8, tm), lambda i: (i, 0, 0)),
                   _const_spec((n_experts, ROUTER_LANES))],
        out_shape=[jax.ShapeDtypeStruct((nt, 8, tm), I32),
                   jax.ShapeDtypeStruct((nt, 8, tm), F32),
                   jax.ShapeDtypeStruct((n_experts, ROUTER_LANES), F32)],
        scratch_shapes=[pltpu.VMEM((n_experts, 1), F32)],
        compiler_params=_params("arbitrary"),
        name="router",
    )(x, mod_l, g.reshape(1, d), w_t, bias)


def _pack_bf16_pairs(x):
    n = x.shape[1] // 2
    hi = lax.bitcast_convert_type(x[:, :n].astype(BF16).astype(F32), U32)
    lo = lax.bitcast_convert_type(x[:, n:].astype(BF16).astype(F32), U32)
    return (hi & jnp.uint32(0xFFFF0000)) | (lo >> 16)


def _unpack_bf16_pairs(w):
    hi = lax.bitcast_convert_type(w & jnp.uint32(0xFFFF0000), F32)
    lo = lax.bitcast_convert_type(w << 16, F32)
    return hi.astype(BF16), lo.astype(BF16)


def _dispatch_kernel(dest_ref, x_ref, mod_ref, g_ref, xs_in_ref, xs_ref, h_scr, sem):
    del xs_in_ref
    i, n = pl.program_id(0), pl.num_programs(0)
    slot = i % 2
    tm = x_ref.shape[0]

    def drain(s):
        for k in range(MOE_TOP_K):
            pltpu.make_async_copy(h_scr.at[s], xs_ref.at[pl.ds(0, tm)], sem.at[s]).wait()

    @pl.when(i >= 2)
    def _():
        drain(slot)

    m = mod_ref[0]
    h_scr[slot] = _pack_bf16_pairs(_modnorm(x_ref[...], g_ref[...], m[4:5], m[3:4]))

    def start(t, carry):
        for k in range(MOE_TOP_K):
            pltpu.make_async_copy(h_scr.at[slot, pl.ds(t, 1)], xs_ref.at[pl.ds(dest_ref[0, k, t], 1)],
                                  sem.at[slot]).start()
        return carry

    lax.fori_loop(0, tm, start, 0, unroll=DMA_ISSUE_UNROLL)

    @pl.when(i == n - 1)
    def _():
        drain(slot)

    @pl.when((i == n - 1) & (n > 1))
    def _():
        drain(1 - slot)


def _dispatch(rows, x, mod_l, g, dest, n_slots):
    t, d = x.shape
    nt, _, tm = dest.shape
    xs0 = jnp.zeros((n_slots, d // 2), U32)
    return pl.pallas_call(
        _dispatch_kernel,
        grid=(nt,),
        in_specs=[pl.BlockSpec((1, MOE_TOP_K, tm), lambda i: (i, 0, 0), memory_space=pltpu.SMEM),
                  _row_spec(d, tm), pl.BlockSpec((1, N_MOD, d), lambda i: (rows.mod_row(i, tm), 0, 0)),
                  _const_spec((1, d)), pl.BlockSpec(memory_space=pl.ANY)],
        out_specs=pl.BlockSpec(memory_space=pl.ANY),
        out_shape=jax.ShapeDtypeStruct((n_slots, d // 2), U32),
        scratch_shapes=[pltpu.VMEM((2, tm, d // 2), U32), pltpu.SemaphoreType.DMA((2,))],
        input_output_aliases={4: 0},
        compiler_params=_params("arbitrary"),
        name="moe_dispatch",
    )(dest, x, mod_l, g.reshape(1, d), xs0)


def _expert_kernel(be_ref, nu_ref, x_ref, wg_ref, wu_ref, wd_ref, y_ref):
    j = pl.program_id(0)

    @pl.when(j < nu_ref[0])
    def _():
        x_lo, x_hi = _unpack_bf16_pairs(x_ref[...])
        half = x_lo.shape[1]

        def proj(w_ref):
            return _bdot(x_lo, w_ref[0, 0, :half, :].astype(BF16)) + _bdot(x_hi, w_ref[0, 0, half:, :].astype(BF16))

        a = proj(wg_ref)
        u = proj(wu_ref)
        y_ref[...] = _bdot((_silu(a) * u).astype(BF16), wd_ref[0, 0].astype(BF16))

    @pl.when(j >= nu_ref[0])
    def _():
        y_ref[...] = jnp.zeros(y_ref.shape, F32)


def _experts(xs, block_expert, n_used, w_gate, w_up, w_down, layer):
    p = xs.shape[0]
    d, hid = w_gate.shape[2:]
    nb = p // EXPERT_ROWS
    return pl.pallas_call(
        _expert_kernel,
        grid_spec=pltpu.PrefetchScalarGridSpec(
            num_scalar_prefetch=2,
            grid=(nb,),
            in_specs=[pl.BlockSpec((EXPERT_ROWS, xs.shape[1]), lambda j, be, nu: (j, 0)),
                      pl.BlockSpec((1, 1, d, hid), lambda j, be, nu: (layer, be[j], 0, 0)),
                      pl.BlockSpec((1, 1, d, hid), lambda j, be, nu: (layer, be[j], 0, 0)),
                      pl.BlockSpec((1, 1, hid, d), lambda j, be, nu: (layer, be[j], 0, 0))],
            out_specs=pl.BlockSpec((EXPERT_ROWS, d), lambda j, be, nu: (j, 0))),
        out_shape=jax.ShapeDtypeStruct((p, d), F32),
        compiler_params=_params("arbitrary"),
        name="moe_experts",
    )(block_expert, n_used, xs, w_gate, w_up, w_down)


def _combine_kernel(dest_ref, dest_next_ref, x_ref, mod_ref, gates_ref, ys_ref, out_ref, buf, sem):
    i, n = pl.program_id(0), pl.num_programs(0)
    slot = i % 2
    tm = x_ref.shape[0]

    def gather(dref, s):
        def start(t, carry):
            for k in range(MOE_TOP_K):
                pltpu.make_async_copy(ys_ref.at[pl.ds(dref[0, k, t], 1)], buf.at[s, k, pl.ds(t, 1)],
                                      sem.at[s]).start()
            return carry

        lax.fori_loop(0, tm, start, 0, unroll=DMA_ISSUE_UNROLL)

    @pl.when(i == 0)
    def _():
        gather(dest_ref, 0)

    @pl.when(i + 1 < n)
    def _():
        gather(dest_next_ref, 1 - slot)

    eye = (lax.broadcasted_iota(I32, (tm, tm), 0) == lax.broadcasted_iota(I32, (tm, tm), 1))
    gates = gates_ref[0]
    g1 = jnp.sum(jnp.where(eye, gates[0:1, :], 0.0), axis=1, keepdims=True)
    g2 = jnp.sum(jnp.where(eye, gates[1:2, :], 0.0), axis=1, keepdims=True)
    for k in range(MOE_TOP_K):
        pltpu.make_async_copy(ys_ref.at[pl.ds(0, tm)], buf.at[slot, k], sem.at[slot]).wait()
    ffn = g1 * buf[slot, 0] + g2 * buf[slot, 1]
    out_ref[...] = x_ref[...] + mod_ref[0][5:6] * ffn


def _combine(rows, x, mod_l, dest, gates, ys):
    t, d = x.shape
    nt, _, tm = dest.shape
    return pl.pallas_call(
        _combine_kernel,
        grid=(nt,),
        in_specs=[pl.BlockSpec((1, MOE_TOP_K, tm), lambda i: (i, 0, 0), memory_space=pltpu.SMEM),
                  pl.BlockSpec((1, MOE_TOP_K, tm), lambda i: (jnp.minimum(i + 1, nt - 1), 0, 0),
                               memory_space=pltpu.SMEM),
                  _row_spec(d, tm), pl.BlockSpec((1, N_MOD, d), lambda i: (rows.mod_row(i, tm), 0, 0)),
                  pl.BlockSpec((1, 8, tm), lambda i: (i, 0, 0)),
                  pl.BlockSpec(memory_space=pl.ANY)],
        out_specs=_row_spec(d, tm),
        out_shape=jax.ShapeDtypeStruct((t, d), F32),
        scratch_shapes=[pltpu.VMEM((2, MOE_TOP_K, tm, d), F32), pltpu.SemaphoreType.DMA((2,))],
        compiler_params=_params("arbitrary"),
        name="moe_combine",
    )(dest, dest, x, mod_l, gates, ys)


def _moe(rows, x, mod_l, g, w_coarse, b_coarse, w_fine, b_fine, w_gate, w_up, w_down, layer):
    t, d = x.shape
    n_experts = w_fine.shape[1]
    n_pad = ROUTER_LANES - n_experts - MOE_GROUPS
    w_t = jnp.concatenate([w_fine.T, w_coarse.T, jnp.zeros((n_pad, d), F32)], axis=0)
    bias = jnp.concatenate([b_fine, b_coarse, jnp.zeros((n_pad,), F32)]).reshape(ROUTER_LANES, 1)
    ri, gates, counts = _router(rows, x, mod_l, g, w_t.astype(BF16), bias, n_experts)
    counts = counts[:, 0].astype(I32)
    padded = ((counts + EXPERT_ROWS - 1) // EXPERT_ROWS) * EXPERT_ROWS
    pad_end = jnp.cumsum(padded)
    pad_start = pad_end - padded
    nb = -(-(t * MOE_TOP_K) // EXPERT_ROWS) + n_experts
    block_row0 = jnp.arange(nb, dtype=I32) * EXPERT_ROWS
    block_expert = jnp.minimum(jnp.sum((pad_end[None, :] <= block_row0[:, None]).astype(I32), axis=1),
                               n_experts - 1)
    n_used = (pad_end[-1:] // EXPERT_ROWS).astype(I32)
    chosen = ri[:, 0:MOE_TOP_K, :, None] == jnp.arange(n_experts, dtype=I32)
    dest = jnp.sum(jnp.where(chosen, pad_start, 0), axis=-1) + ri[:, MOE_TOP_K:2 * MOE_TOP_K, :]
    xs = _dispatch(rows, x, mod_l, g, dest, nb * EXPERT_ROWS)
    ys = _experts(xs, block_expert, n_used, w_gate, w_up, w_down, layer)
    return _combine(rows, x, mod_l, dest, gates, ys)


def _final_kernel(x_ref, g_ref, o_ref):
    o_ref[...] = _rms(x_ref[...], g_ref[...])


def _final_norm(x, g, row0, n_rows):
    d = x.shape[1]
    blk0 = row0 // TOKEN_TILE
    return pl.pallas_call(
        _final_kernel,
        grid=(n_rows // TOKEN_TILE,),
        in_specs=[pl.BlockSpec((TOKEN_TILE, d), lambda i: (blk0 + i, 0)), _const_spec((1, d))],
        out_specs=_row_spec(d),
        out_shape=jax.ShapeDtypeStruct((n_rows, d), F32),
        compiler_params=_params("parallel"),
        name="final_norm",
    )(x, g.reshape(1, d))


def kernel(x_prompt, x_sample, cache_k, cache_v, state_ssm_fwd, state_ssm_bwd, c, c_ctx, w_mod, b_mod, norm_mix, norm_ffn, norm_final, w_in, conv_w, conv_b, dt_bias, a_log, d_skip, ssm_gain, lam_vec, subln_gain, w_out, pool_w, pool_b, pool_scale, w_coarse, b_coarse, w_fine, b_fine, w_gate, w_up, w_down):
    n_ctx, len_ctx, d = x_prompt.shape
    n_lat, len_lat, _ = x_sample.shape
    rows = _Rows(n_ctx, len_ctx, n_lat, len_lat)
    depth = w_mod.shape[0]
    d_ssm = ssm_gain.shape[1]
    heads = a_log.shape[2]
    da_heads = cache_k.shape[3]
    qk_cols = da_heads * 2 * DA_QK_DIM
    xbc_cols = conv_w.shape[2]
    past = cache_k.shape[2]

    x = jnp.concatenate([x_prompt.reshape(rows.t_ctx, d), x_sample.reshape(rows.t_lat, d)], axis=0)
    n_vec = -(-(1 + n_lat) // 8) * 8
    cvecs = jnp.zeros((n_vec, d), F32).at[0].set(c_ctx).at[1:1 + n_lat].set(c)
    mod = _modulation(cvecs, w_mod, b_mod).reshape(depth, n_vec, N_MOD, d)

    ks, vs, sf, sb = [], [], [], []
    for l in range(depth):
        j = l // 2
        if l % 2 == 0:
            lam_init = 0.8 - 0.6 * math.exp(-0.3 * l)
            splits = [d_ssm, d_ssm + xbc_cols, d_ssm + xbc_cols + 2 * heads]
            splits += [splits[-1] + qk_cols, splits[-1] + 2 * qk_cols]
            wz, wxbc, wdt, wq, wk, wv = [w.astype(BF16) for w in jnp.split(w_in[j], splits, axis=1)]
            zT, xbc, dt, dtT, qs, kr, vt, k_ctx, v_ctx = _inproj(rows, x, mod[l], norm_mix[l],
                                                                 (wz.T, wxbc, wdt, wdt.T, wq, wk, wv))
            xsT, bm, cm = _conv(rows, xbc, conv_w[j], conv_b[j], d_ssm)
            hf0 = state_ssm_fwd[:, j].reshape(n_lat, heads * SSM_HEAD_DIM, SSM_STATE)
            hb0 = state_ssm_bwd[:, j].reshape(n_lat, heads * SSM_HEAD_DIM, SSM_STATE)
            yfT, ybT, hf, hb = _ssd(rows, xsT, bm, cm, dt, dtT, dt_bias[j], a_log[j], hf0, hb0)
            o_ctx = _attention(qs, kr, vt, lam_vec[j], subln_gain[j], lam_init,
                               n_seq=n_ctx, length=len_ctx, row0=0)
            ck = cache_k[:, j].reshape(n_lat, past, qk_cols).astype(BF16)
            cvt = jnp.transpose(cache_v[:, j], (0, 2, 3, 1))
            cvt = jnp.concatenate([cvt, jnp.ones((n_lat, da_heads, V_ROWS - DA_V_DIM, past), F32)], axis=2)
            cvt = cvt.reshape(n_lat, da_heads * V_ROWS, past).astype(BF16)
            o_lat = _attention(qs, kr, vt, lam_vec[j], subln_gain[j], lam_init,
                               n_seq=n_lat, length=len_lat, row0=rows.t_ctx, cache_k=ck, cache_vt=cvt)
            dsk = jnp.repeat(d_skip[j], SSM_HEAD_DIM, axis=1)
            wo = w_out[j].astype(BF16)
            x = _outproj(rows, yfT, ybT, xsT, zT, o_ctx, o_lat, x, mod[l], dsk, ssm_gain[j],
                         wo[:d_ssm], wo[d_ssm:])
            ks.append(k_ctx.reshape(n_ctx, len_ctx, da_heads, 2, DA_QK_DIM))
            vs.append(v_ctx.reshape(n_ctx, len_ctx, da_heads, DA_V_DIM))
            sf.append(hf[:n_ctx].reshape(n_ctx, heads, SSM_HEAD_DIM, SSM_STATE))
            sb.append(hb[:n_ctx].reshape(n_ctx, heads, SSM_HEAD_DIM, SSM_STATE))
        else:
            x = _pool(rows, x, mod[l], norm_mix[l], pool_w[j].astype(BF16), pool_b[j], pool_scale[j])
        x = _moe(rows, x, mod[l], norm_ffn[l], w_coarse[l], b_coarse[l], w_fine[l], b_fine[l],
                 w_gate, w_up, w_down, l)

    y_prompt = _final_norm(x, norm_final, 0, rows.t_ctx).reshape(n_ctx, len_ctx, d)
    y_sample = _final_norm(x, norm_final, rows.t_ctx, rows.t_lat).reshape(n_lat, len_lat, d)
    return (y_prompt, y_sample, jnp.stack(ks, axis=1), jnp.stack(vs, axis=1),
            jnp.stack(sf, axis=1), jnp.stack(sb, axis=1))
```
